```python
import jax
import jax.numpy as jnp
from jax import lax
import numpy as np

D_MODEL = 2048
BATCH = 4
SEQ = 4096
DEPTH = 1

MIX_WIDTH = D_MODEL
ATT_WIDTH = MIX_WIDTH // 2
ATT_HEADS = 8
ATT_HEAD_DIM = ATT_WIDTH // ATT_HEADS
ATT_BLOCK_Q = 128
RWKV_WIDTH = MIX_WIDTH - ATT_WIDTH
RWKV_HEAD_DIM = 64
RWKV_HEADS = RWKV_WIDTH // RWKV_HEAD_DIM
RWKV_DECAY_LORA = max(32, int(round(1.8 * RWKV_WIDTH ** 0.5 / 32)) * 32)
RWKV_ICLR_LORA = max(32, int(round(1.8 * RWKV_WIDTH ** 0.5 / 32)) * 32)
RWKV_GATE_LORA = max(32, int(round(0.6 * RWKV_WIDTH ** 0.8 / 32)) * 32)
RWKV_LN_EPS = 64e-5
RWKV_SPLITS = [RWKV_WIDTH, 2 * RWKV_WIDTH, 3 * RWKV_WIDTH,
               3 * RWKV_WIDTH + RWKV_DECAY_LORA,
               3 * RWKV_WIDTH + RWKV_DECAY_LORA + RWKV_ICLR_LORA]
RWKV_PROJ = 3 * RWKV_WIDTH + RWKV_DECAY_LORA + RWKV_ICLR_LORA + RWKV_GATE_LORA
IN_PROJ = 3 * ATT_WIDTH + RWKV_PROJ
N_GROUPS = 8
EXPERTS_PER_GROUP = 8
N_EXPERTS = N_GROUPS * EXPERTS_PER_GROUP
TOP_K_IN_GROUP = 2
D_EXPERT = D_MODEL // 4
MOE_BLOCK = 128
RMS_EPS = 1e-6
N_MOD = 6

kernel_name = "hybrid_stickbreak_rwkv7_hmoe_layer"


def rms_norm(x, gain):
    xf = x.astype(jnp.float32)
    y = xf * lax.rsqrt(jnp.mean(xf * xf, axis=-1, keepdims=True) + RMS_EPS)
    return (y * gain.astype(jnp.float32)).astype(x.dtype)


def token_shift(p, mix):
    prev = jnp.pad(p, ((0, 0), (1, 0), (0, 0)))[:, :-1]
    return p + (prev - p) * mix


def stick_breaking_attention(q, k, v):
    seq = q.shape[2]
    scale = q.shape[-1] ** -0.5
    outs = []
    for blk in range(seq // ATT_BLOCK_Q):
        t0 = blk * ATT_BLOCK_Q
        t1 = t0 + ATT_BLOCK_Q
        z = jnp.einsum('bhtd,bhsd->bhts', q[:, :, t0:t1], k[:, :, :t1],
                       preferred_element_type=jnp.float32) * scale
        causal = jnp.arange(t1)[None, :] < jnp.arange(t0, t1)[:, None]
        log_not_break = jnp.where(causal, jax.nn.log_sigmoid(-z), 0.0)
        log_survive = lax.cumsum(log_not_break, axis=3, reverse=True) - log_not_break
        weights = jnp.where(causal, jnp.exp(jax.nn.log_sigmoid(z) + log_survive), 0.0)
        outs.append(jnp.einsum('bhts,bhsd->bhtd', weights.astype(v.dtype), v[:, :, :t1]))
    return jnp.concatenate(outs, axis=2)


def rwkv7_scan(r, w, k, v, a, b):
    bsz, _, n_heads, n = r.shape

    def step(state, inp):
        r_t, w_t, k_t, v_t, a_t, b_t = inp
        sa = jnp.einsum('bhvk,bhk->bhv', state, a_t)
        state = (state * w_t[:, :, None, :] + sa[..., None] * b_t[:, :, None, :]
                 + v_t[..., None] * k_t[:, :, None, :])
        return state, jnp.einsum('bhvk,bhk->bhv', state, r_t)

    xs = tuple(jnp.moveaxis(t, 1, 0) for t in (r, w, k, v, a, b))
    s0 = jnp.zeros((bsz, n_heads, n, n), jnp.float32)
    _, ys = lax.scan(step, s0, xs)
    return jnp.moveaxis(ys, 0, 1)


def rwkv7_time_mix(p, w0, w2, a0, a2, g2, k_k, k_a, r_k, ln_w, ln_b):
    bsz, seq, _ = p.shape
    p = p.astype(jnp.float32)
    r, k, v, wd, ad, gd = jnp.split(p, RWKV_SPLITS, axis=-1)
    w_log = -jax.nn.softplus(-(w0 + jnp.tanh(wd) @ w2)) - 0.5
    decay = jnp.exp(-jnp.exp(w_log))
    a = jax.nn.sigmoid(a0 + ad @ a2)
    g = jax.nn.sigmoid(gd) @ g2

    def heads(t):
        return t.reshape(bsz, seq, RWKV_HEADS, RWKV_HEAD_DIM)

    kk = heads(k * k_k)
    kk = kk / jnp.maximum(jnp.linalg.norm(kk, axis=-1, keepdims=True), 1e-12)
    k = k * (1.0 + (a - 1.0) * k_a)
    r_h, k_h, v_h, a_h = heads(r), heads(k), heads(v), heads(a)
    y = rwkv7_scan(r_h, heads(decay), k_h, v_h, -kk, kk * a_h)
    mu = jnp.mean(y, axis=-1, keepdims=True)
    var = jnp.mean(jnp.square(y - mu), axis=-1, keepdims=True)
    y = ((y - mu) * lax.rsqrt(var + RWKV_LN_EPS)).reshape(bsz, seq, RWKV_WIDTH) * ln_w + ln_b
    bonus = jnp.sum(r_h * k_h * r_k, axis=-1, keepdims=True) * v_h
    y = y + bonus.reshape(bsz, seq, RWKV_WIDTH)
    return y * g


def hybrid_mixer(h, w_in, w_out, attn_out_gain, shift_mix, w0, w2, a0, a2, g2,
                 k_k, k_a, r_k, ln_w, ln_b):
    bsz, seq, _ = h.shape
    proj = h @ w_in
    att = proj[..., :3 * ATT_WIDTH].reshape(bsz, seq, 3, ATT_HEADS, ATT_HEAD_DIM)
    q, k, v = jnp.transpose(att, (2, 0, 3, 1, 4))
    o_att = stick_breaking_attention(q, k, v).transpose(0, 2, 1, 3)
    o_att = rms_norm(o_att, attn_out_gain.reshape(ATT_HEADS, ATT_HEAD_DIM))
    o_att = o_att.reshape(bsz, seq, ATT_WIDTH)
    p = token_shift(proj[..., 3 * ATT_WIDTH:], shift_mix)
    o_rwkv = rwkv7_time_mix(p, w0, w2, a0, a2, g2, k_k, k_a, r_k, ln_w, ln_b).astype(h.dtype)
    return jnp.concatenate([o_att, o_rwkv], axis=-1) @ w_out


def hierarchical_moe(h, wg_router, bg_router, we_router, be_router, w_gate, w_up, w_down):
    bsz, seq, d = h.shape
    xt = h.reshape(-1, d)
    n_tok = xt.shape[0]
    g_logits = (xt @ wg_router + bg_router).astype(jnp.float32)
    g_prob = jax.nn.softmax(g_logits, axis=-1)
    g_sel = jnp.argmax(g_logits, axis=-1)
    p_group = jnp.take_along_axis(g_prob, g_sel[:, None], axis=1)
    e_logits = (xt @ we_router + be_router).astype(jnp.float32)
    e_logits = e_logits.reshape(n_tok, N_GROUPS, EXPERTS_PER_GROUP)
    e_logits = jnp.take_along_axis(e_logits, g_sel[:, None, None], axis=1)[:, 0]
    top_p, top_i = lax.top_k(jax.nn.softmax(e_logits, axis=-1), TOP_K_IN_GROUP)
    gates = p_group * top_p / jnp.sum(top_p, axis=-1, keepdims=True)
    expert_id = g_sel[:, None] * EXPERTS_PER_GROUP + top_i
    n_rows = n_tok * TOP_K_IN_GROUP
    flat_e = expert_id.reshape(-1)
    order = jnp.argsort(flat_e)
    sorted_e = flat_e[order]
    sorted_tok = order // TOP_K_IN_GROUP
    sorted_gate = gates.reshape(-1)[order]
    xs = xt[sorted_tok]
    counts = jnp.bincount(flat_e, length=N_EXPERTS)
    ends = jnp.cumsum(counts)
    starts = ends - counts
    first_blk = starts // MOE_BLOCK
    n_items = jnp.where(counts > 0, (ends - 1) // MOE_BLOCK - first_blk + 1, 0)
    item_end = jnp.cumsum(n_items)
    item_start = item_end - n_items
    n_work = n_rows // MOE_BLOCK + N_EXPERTS - 1
    w_idx = jnp.arange(n_work)
    item_e = jnp.minimum(jnp.searchsorted(item_end, w_idx, side='right'), N_EXPERTS - 1)
    item_valid = w_idx < item_end[-1]
    item_blk = jnp.where(item_valid, first_blk[item_e] + w_idx - item_start[item_e], 0)

    def expert_block(args):
        e, blk, valid = args
        start = blk * MOE_BLOCK
        rows = lax.dynamic_slice_in_dim(xs, start, MOE_BLOCK, axis=0)
        keep = (lax.dynamic_slice_in_dim(sorted_e, start, MOE_BLOCK) == e) & valid
        gate = lax.dynamic_slice_in_dim(sorted_gate, start, MOE_BLOCK)
        hid = jax.nn.silu(rows @ w_gate[e]) * (rows @ w_up[e])
        out = hid @ w_down[e]
        return jnp.where(keep[:, None], out * gate[:, None].astype(out.dtype), 0.0).astype(out.dtype)

    contrib = lax.map(expert_block, (item_e, item_blk, item_valid))
    row_idx = (item_blk[:, None] * MOE_BLOCK + jnp.arange(MOE_BLOCK)[None, :]).reshape(-1)
    y_rows = jnp.zeros((n_rows, d), contrib.dtype).at[row_idx].add(contrib.reshape(-1, d))
    y = jax.ops.segment_sum(y_rows, sorted_tok, num_segments=n_tok)
    return y.reshape(bsz, seq, d)


def setup_inputs(seed: int = 0) -> dict:
    key = jax.random.key(seed)
    ks = jax.random.split(key, 32)
    L = DEPTH

    def nrm(k, shape, scale):
        return jax.random.normal(k, shape, jnp.float32) * scale

    return {
        "x": nrm(ks[0], (BATCH, SEQ, D_MODEL), 1.0),
        "c": nrm(ks[1], (BATCH, D_MODEL), 1.0),
        "w_ada": nrm(ks[2], (L, D_MODEL, N_MOD * D_MODEL), 0.5 * D_MODEL ** -0.5),
        "b_ada": nrm(ks[3], (L, N_MOD * D_MODEL), 0.02),
        "g_pre_mix": 1.0 + nrm(ks[4], (L, D_MODEL), 0.02),
        "g_post_mix": 1.0 + nrm(ks[5], (L, D_MODEL), 0.02),
        "g_pre_ffn": 1.0 + nrm(ks[6], (L, D_MODEL), 0.02),
        "g_post_ffn": 1.0 + nrm(ks[7], (L, D_MODEL), 0.02),
        "w_in": nrm(ks[8], (L, D_MODEL, IN_PROJ), D_MODEL ** -0.5),
        "w_out": nrm(ks[9], (L, MIX_WIDTH, D_MODEL), MIX_WIDTH ** -0.5),
        "attn_out_gain": 1.0 + nrm(ks[10], (L, ATT_WIDTH), 0.02),
        "rwkv_shift_mix": jax.random.uniform(ks[11], (L, RWKV_PROJ), jnp.float32),
        "rwkv_w0": jax.random.uniform(ks[12], (L, RWKV_WIDTH), jnp.float32, -6.0, 1.0),
        "rwkv_w2": nrm(ks[13], (L, RWKV_DECAY_LORA, RWKV_WIDTH), 0.5 * RWKV_DECAY_LORA ** -0.5),
        "rwkv_a0": nrm(ks[14], (L, RWKV_WIDTH), 0.1),
        "rwkv_a2": nrm(ks[15], (L, RWKV_ICLR_LORA, RWKV_WIDTH), RWKV_ICLR_LORA ** -0.5),
        "rwkv_g2": nrm(ks[16], (L, RWKV_GATE_LORA, RWKV_WIDTH), RWKV_GATE_LORA ** -0.5),
        "rwkv_k_k": 0.85 + nrm(ks[17], (L, RWKV_WIDTH), 0.02),
        "rwkv_k_a": 1.0 + nrm(ks[18], (L, RWKV_WIDTH), 0.02),
        "rwkv_r_k": nrm(ks[19], (L, RWKV_HEADS, RWKV_HEAD_DIM), 0.1),
        "rwkv_ln_w": 1.0 + nrm(ks[20], (L, RWKV_WIDTH), 0.02),
        "rwkv_ln_b": nrm(ks[21], (L, RWKV_WIDTH), 0.02),
        "router_group_w": nrm(ks[22], (L, D_MODEL, N_GROUPS), D_MODEL ** -0.5),
        "router_group_b": nrm(ks[23], (L, N_GROUPS), 0.01),
        "router_expert_w": nrm(ks[24], (L, D_MODEL, N_EXPERTS), D_MODEL ** -0.5),
        "router_expert_b": nrm(ks[25], (L, N_EXPERTS), 0.01),
        "expert_w_gate": nrm(ks[26], (L, N_EXPERTS, D_MODEL, D_EXPERT), D_MODEL ** -0.5),
        "expert_w_up": nrm(ks[27], (L, N_EXPERTS, D_MODEL, D_EXPERT), D_MODEL ** -0.5),
        "expert_w_down": nrm(ks[28], (L, N_EXPERTS, D_EXPERT, D_MODEL), D_EXPERT ** -0.5),
    }


def reference(x, c, w_ada, b_ada, g_pre_mix, g_post_mix, g_pre_ffn, g_post_ffn, w_in, w_out,
              attn_out_gain, rwkv_shift_mix, rwkv_w0, rwkv_w2, rwkv_a0, rwkv_a2, rwkv_g2,
              rwkv_k_k, rwkv_k_a, rwkv_r_k, rwkv_ln_w, rwkv_ln_b, router_group_w, router_group_b,
              router_expert_w, router_expert_b, expert_w_gate, expert_w_up, expert_w_down):
    for layer in range(DEPTH):
        mod = jax.nn.silu(c) @ w_ada[layer] + b_ada[layer]
        sh_m, sc_m, gt_m, sh_f, sc_f, gt_f = jnp.split(mod[:, None, :], N_MOD, axis=-1)
        h = rms_norm(x, g_pre_mix[layer]) * (1.0 + sc_m) + sh_m
        y = hybrid_mixer(h, w_in[layer], w_out[layer], attn_out_gain[layer], rwkv_shift_mix[layer],
                         rwkv_w0[layer], rwkv_w2[layer], rwkv_a0[layer], rwkv_a2[layer],
                         rwkv_g2[layer], rwkv_k_k[layer], rwkv_k_a[layer], rwkv_r_k[layer],
                         rwkv_ln_w[layer], rwkv_ln_b[layer])
        x = x + gt_m * rms_norm(y, g_post_mix[layer])
        h = rms_norm(x, g_pre_ffn[layer]) * (1.0 + sc_f) + sh_f
        y = hierarchical_moe(h, router_group_w[layer], router_group_b[layer],
                             router_expert_w[layer], router_expert_b[layer],
                             expert_w_gate[layer], expert_w_up[layer], expert_w_down[layer])
        x = x + gt_f * rms_norm(y, g_post_ffn[layer])
    return x
```

```python
import functools

import jax
import jax.numpy as jnp
from jax import lax
from jax.experimental import pallas as pl
from jax.experimental.pallas import tpu as pltpu

F32 = jnp.float32
BF16 = jnp.bfloat16

LANES = 128
VMEM_LIMIT_BYTES = 56 * 1024 * 1024

ATT_HEADS = 8
ATT_HEAD_DIM = 128
ATT_WIDTH = ATT_HEADS * ATT_HEAD_DIM
RWKV_HEAD_DIM = 64
RWKV_WIDTH = 1024
RWKV_PAIRS = RWKV_WIDTH // LANES
RWKV_CHUNK = 64
RWKV_LN_EPS = 64e-5
RMS_EPS = 1e-6
N_GROUPS = 8
EXPERTS_PER_GROUP = 8
N_EXPERTS = 64
TOP_K = 2
N_MOD = 6


def _dot(a, b):
    return jnp.dot(a, b, preferred_element_type=F32)


def _dot_nt(a, b):
    return lax.dot_general(a, b, (((1,), (1,)), ((), ())), preferred_element_type=F32)


def _dot_tn(a, b):
    return lax.dot_general(a, b, (((0,), (0,)), ((), ())), preferred_element_type=F32)


def _split(x):
    hi = x.astype(BF16)
    lo = (x - hi.astype(F32)).astype(BF16)
    return hi, lo


def _dot3(a, b, dot=_dot):
    ah, al = _split(a)
    bh, bl = _split(b)
    return dot(ah, bh) + dot(ah, bl) + dot(al, bh)


def _params(*sem):
    return pltpu.CompilerParams(dimension_semantics=sem, vmem_limit_bytes=VMEM_LIMIT_BYTES)


def _adaln_kernel(c_ref, w_ref, b_ref, o_ref):
    c = c_ref[...]
    s = (c * jax.nn.sigmoid(c)).astype(BF16)
    o_ref[...] = _dot(s, w_ref[...].astype(BF16)) + b_ref[...]


def _adaln(c, w, b):
    bsz, d = c.shape
    n = w.shape[1]
    rows = 8
    tn = 1536
    cp = jnp.zeros((rows, d), F32).at[:bsz].set(c)
    out = pl.pallas_call(
        _adaln_kernel,
        grid=(n // tn,),
        in_specs=[
            pl.BlockSpec((rows, d), lambda j: (0, 0)),
            pl.BlockSpec((d, tn), lambda j: (0, j)),
            pl.BlockSpec((1, tn), lambda j: (0, j)),
        ],
        out_specs=pl.BlockSpec((rows, tn), lambda j: (0, j)),
        out_shape=jax.ShapeDtypeStruct((rows, n), F32),
        compiler_params=_params("arbitrary"),
        name="adaln",
    )(cp, w, b.reshape(1, n))
    return out[:bsz]


def _in_proj_kernel(x_ref, g_ref, sc_ref, sh_ref, wa_ref, wr_ref, wl_ref,
                    oa_ref, or_ref, ol_ref):
    x = x_ref[...]
    ms = jnp.mean(x * x, axis=-1, keepdims=True)
    y = x * lax.rsqrt(ms + RMS_EPS) * g_ref[...]
    h = (y * (1.0 + sc_ref[...]) + sh_ref[...]).astype(BF16)
    oa_ref[...] = _dot(h, wa_ref[...]).astype(oa_ref.dtype)
    or_ref[...] = _dot(h, wr_ref[...])
    ol_ref[...] = _dot(h, wl_ref[...])


def _in_proj(x, g, sc, sh, w_att, w_rkv, w_lora, tm=256):
    bsz, seq, d = x.shape
    na, nr, nl = w_att.shape[1], w_rkv.shape[1], w_lora.shape[1]
    const = lambda b, i: (0, 0)
    row = lambda b, i: (b, i, 0)
    per_b = lambda b, i: (b, 0, 0)
    return pl.pallas_call(
        _in_proj_kernel,
        grid=(bsz, seq // tm),
        in_specs=[
            pl.BlockSpec((None, tm, d), row),
            pl.BlockSpec((1, d), const),
            pl.BlockSpec((None, 1, d), per_b),
            pl.BlockSpec((None, 1, d), per_b),
            pl.BlockSpec((d, na), const, pipeline_mode=pl.Buffered(1)),
            pl.BlockSpec((d, nr), const, pipeline_mode=pl.Buffered(1)),
            pl.BlockSpec((d, nl), const, pipeline_mode=pl.Buffered(1)),
        ],
        out_specs=[
            pl.BlockSpec((None, tm, na), row),
            pl.BlockSpec((None, tm, nr), row),
            pl.BlockSpec((None, tm, nl), row),
        ],
        out_shape=[
            jax.ShapeDtypeStruct((bsz, seq, na), BF16),
            jax.ShapeDtypeStruct((bsz, seq, nr), F32),
            jax.ShapeDtypeStruct((bsz, seq, nl), F32),
        ],
        compiler_params=_params("arbitrary", "arbitrary"),
        name="in_proj",
    )(x, g.reshape(1, d), sc, sh, w_att, w_rkv, w_lora)


def _att_kernel(q_ref, k_ref, v_ref, gain_ref, o_ref, *, tile, scale):
    qi = pl.program_id(2)
    q = q_ref[...]
    row = lax.broadcasted_iota(jnp.int32, (tile, tile), 0)
    col = lax.broadcasted_iota(jnp.int32, (tile, tile), 1)
    causal = col < row
    r2 = lax.broadcasted_iota(jnp.int32, (2 * tile, 2 * tile), 0) & (tile - 1)
    c2 = lax.broadcasted_iota(jnp.int32, (2 * tile, 2 * tile), 1)
    tri_ones = jnp.where((c2 >= tile) | (r2 > c2), 1.0, 0.0).astype(BF16)

    def tile_step(kj, carry, acc, masked):
        start = pl.multiple_of(kj * tile, tile)
        k = k_ref[pl.ds(start, tile), :]
        v = v_ref[pl.ds(start, tile), :]
        z = _dot_nt(q, k) * scale
        sp = jnp.maximum(z, 0.0) + jnp.log(1.0 + jnp.exp(-jnp.abs(z)))
        if masked:
            sp = jnp.where(causal, sp, 0.0)
        hi, lo = _split(sp)
        cs = _dot(jnp.concatenate([hi, lo], axis=1), tri_ones)
        logw = z - sp - cs[:, :tile] - carry
        w = jnp.exp(logw)
        if masked:
            w = jnp.where(causal, w, 0.0)
        acc = acc + _dot(w.astype(BF16), v)
        return carry + cs[:, tile:], acc

    zeros = jnp.zeros((tile, tile), F32)
    carry, acc = tile_step(qi, zeros, zeros, True)

    def body(i, state):
        return tile_step(qi - 1 - i, state[0], state[1], False)

    carry, acc = lax.fori_loop(0, qi, body, (carry, acc))
    ms = jnp.mean(acc * acc, axis=-1, keepdims=True)
    o_ref[...] = (acc * lax.rsqrt(ms + RMS_EPS) * gain_ref[...]).astype(o_ref.dtype)


def _attention(qkv, gain, tile=128):
    bsz, seq, _ = qkv.shape
    nh = ATT_HEADS
    kern = functools.partial(_att_kernel, tile=tile, scale=ATT_HEAD_DIM ** -0.5)
    return pl.pallas_call(
        kern,
        grid=(bsz, nh, seq // tile),
        in_specs=[
            pl.BlockSpec((None, tile, ATT_HEAD_DIM), lambda b, h, i: (b, i, h)),
            pl.BlockSpec((None, seq, ATT_HEAD_DIM), lambda b, h, i: (b, 0, nh + h)),
            pl.BlockSpec((None, seq, ATT_HEAD_DIM), lambda b, h, i: (b, 0, 2 * nh + h)),
            pl.BlockSpec((1, ATT_HEAD_DIM), lambda b, h, i: (0, h)),
        ],
        out_specs=pl.BlockSpec((None, tile, ATT_HEAD_DIM), lambda b, h, i: (b, i, h)),
        out_shape=jax.ShapeDtypeStruct((bsz, seq, ATT_WIDTH), BF16),
        compiler_params=_params("arbitrary", "arbitrary", "arbitrary"),
        name="attention",
    )(qkv, qkv, qkv, gain.reshape(1, ATT_WIDTH))


def _rwkv_kernel(r_ref, k_ref, v_ref, lw_ref, a_ref, b_ref, y_ref, s_ref, *, n_chunks):
    C = RWKV_CHUNK
    N = RWKV_HEAD_DIM

    @pl.when(pl.program_id(2) == 0)
    def _():
        s_ref[...] = jnp.zeros_like(s_ref)

    row = lax.broadcasted_iota(jnp.int32, (C, LANES), 0)
    lane = lax.broadcasted_iota(jnp.int32, (C, LANES), 1)
    col = lane & (N - 1)
    head0 = lane < N
    strict = row > col
    incl = row >= col
    eye = jnp.where(row == col, 1.0, 0.0)
    blk16 = (row >> 4) == (col >> 4)
    blk32 = (row >> 5) == (col >> 5)
    tri2 = jnp.where(incl, 1.0, 0.0).astype(BF16)
    ones = jnp.ones((C, LANES), BF16)
    r128 = lax.broadcasted_iota(jnp.int32, (LANES, LANES), 0)
    c128 = lax.broadcasted_iota(jnp.int32, (LANES, LANES), 1)
    same_head = (r128 < N) == (c128 < N)

    def bd(x):
        z = jnp.zeros_like(x)
        return jnp.concatenate([jnp.where(head0, x, z), jnp.where(head0, z, x)], axis=0)

    def mm3(x, y):
        xh, xl = _split(x)
        yh, yl = _split(y)
        byh, byl = bd(yh), bd(yl)
        return _dot(xh, byh) + _dot(xh, byl) + _dot(xl, byh)

    def mm1(x, y_bd):
        return _dot(x.astype(BF16), y_bd)

    for c in range(n_chunks):
        sl = pl.ds(c * C, C)
        r, k, v = r_ref[sl, :], k_ref[sl, :], v_ref[sl, :]
        lw, a, b = lw_ref[sl, :], a_ref[sl, :], b_ref[sl, :]
        lw_hi, lw_lo = _split(lw)
        cum = _dot(tri2, jnp.concatenate([lw_hi, lw_lo], axis=0))
        last = cum[C - 1:C, :]
        wcol = jnp.exp(_dot_tn(lw_hi, ones) + _dot_tn(lw_lo, ones))
        e_neg = jnp.exp(-cum)
        e_rem = jnp.exp(last - cum)
        rt = r * jnp.exp(cum)
        at = a * jnp.exp(cum - lw)
        kt = k * e_neg
        bt = b * e_neg
        kw = k * e_rem
        bw = b * e_rem

        bt_h, bt_l = _split(bt)
        at_h, at_l = _split(at)
        bbt_h = bd(bt_h)
        a_ab = _dot_nt(at_h, bbt_h) + _dot_nt(at_h, bd(bt_l)) + _dot_nt(at_l, bbt_h)
        a_ab = jnp.where(strict, a_ab, 0.0)
        rt_b = rt.astype(BF16)
        bkt = bd(kt.astype(BF16))
        a_rb = jnp.where(incl, _dot_nt(rt_b, bbt_h), 0.0)
        a_ak = jnp.where(strict, _dot_nt(at_h, bkt), 0.0)
        a_rk = jnp.where(incl, _dot_nt(rt_b, bkt), 0.0)

        d = jnp.where(blk16, a_ab, 0.0)
        x = eye + d
        y = mm3(d, d)
        x = x + mm3(x, y)
        y = mm3(y, y)
        x = x + mm3(x, y)
        y = mm3(y, y)
        x = x + mm3(x, y)
        off = jnp.where(blk32 & jnp.logical_not(blk16), a_ab, 0.0)
        x = x + mm3(mm3(x, off), x)
        off = jnp.where(blk32, 0.0, a_ab)
        t_inv = x + mm3(mm3(x, off), x)

        ap = mm3(t_inv, at)
        ta = mm3(t_inv, a_ak)
        bv = bd(v.astype(BF16))
        u_loc = mm1(ta, bv)
        y_loc = mm1(a_rk, bv)

        s = s_ref[...]
        s_b = s.astype(BF16)
        u = mm1(ap, s_b) + u_loc
        out = mm1(rt, s_b) + mm1(a_rb, bd(u.astype(BF16))) + y_loc
        y_ref[sl, :] = out
        ds = _dot_tn(jnp.concatenate([bw, kw], axis=0).astype(BF16),
                     jnp.concatenate([u, v], axis=0).astype(BF16))
        s_ref[...] = s * wcol + jnp.where(same_head, ds, 0.0)


def _rwkv_scan(r, k, v, lw, a, b, ts=256):
    bsz, seq, width = r.shape
    spec = pl.BlockSpec((None, ts, LANES), lambda bb, p, t: (bb, t, p))
    kern = functools.partial(_rwkv_kernel, n_chunks=ts // RWKV_CHUNK)
    return pl.pallas_call(
        kern,
        grid=(bsz, width // LANES, seq // ts),
        in_specs=[spec] * 6,
        out_specs=spec,
        out_shape=jax.ShapeDtypeStruct((bsz, seq, width), F32),
        scratch_shapes=[pltpu.VMEM((LANES, LANES), F32)],
        compiler_params=_params("arbitrary", "arbitrary", "arbitrary"),
        name="rwkv",
    )(r, k, v, lw, a, b)


def _out_proj_kernel(oa_ref, or_ref, x_ref, w_ref, gpost_ref, gt_ref, gpre_ref, sc_ref, sh_ref,
                     wr_ref, x1_ref, h2_ref, lg_ref):
    half = oa_ref.shape[-1]
    y = _dot(oa_ref[...], w_ref[:half, :]) + _dot(or_ref[...], w_ref[half:, :])
    ms = jnp.mean(y * y, axis=-1, keepdims=True)
    x1 = x_ref[...] + gt_ref[...] * (y * lax.rsqrt(ms + RMS_EPS) * gpost_ref[...])
    x1_ref[...] = x1
    ms1 = jnp.mean(x1 * x1, axis=-1, keepdims=True)
    h2 = (x1 * lax.rsqrt(ms1 + RMS_EPS) * gpre_ref[...]) * (1.0 + sc_ref[...]) + sh_ref[...]
    h2_ref[...] = h2.astype(BF16)
    lg_ref[...] = _dot3(h2, wr_ref[...])


def _out_proj(o_att, o_rwkv, x, w_out, g_post, gt, g_pre, sc, sh, w_router, tm=256):
    bsz, seq, d = x.shape
    half = o_att.shape[-1]
    nr = w_router.shape[1]
    const = lambda b, i: (0, 0)
    row = lambda b, i: (b, i, 0)
    per_b = lambda b, i: (b, 0, 0)
    vec = pl.BlockSpec((1, d), const)
    mod = pl.BlockSpec((None, 1, d), per_b)
    return pl.pallas_call(
        _out_proj_kernel,
        grid=(bsz, seq // tm),
        in_specs=[
            pl.BlockSpec((None, tm, half), row),
            pl.BlockSpec((None, tm, half), row),
            pl.BlockSpec((None, tm, d), row),
            pl.BlockSpec((d, d), const, pipeline_mode=pl.Buffered(1)),
            vec, mod, vec, mod, mod,
            pl.BlockSpec((d, nr), const),
        ],
        out_specs=[
            pl.BlockSpec((None, tm, d), row),
            pl.BlockSpec((None, tm, d), row),
            pl.BlockSpec((None, tm, nr), row),
        ],
        out_shape=[
            jax.ShapeDtypeStruct((bsz, seq, d), F32),
            jax.ShapeDtypeStruct((bsz, seq, d), BF16),
            jax.ShapeDtypeStruct((bsz, seq, nr), F32),
        ],
        compiler_params=_params("arbitrary", "arbitrary"),
        name="out_proj",
    )(o_att, o_rwkv, x, w_out, g_post.reshape(1, d), gt, g_pre.reshape(1, d), sc, sh, w_router)


def _expert_kernel(te_ref, nv_ref, x_ref, gate_ref, wg_ref, wu_ref, wd_ref, o_ref,
                   wg_b, wu_b, wd_b):
    i = pl.program_id(0)
    valid = i < nv_ref[0]
    prev = te_ref[jnp.maximum(i - 1, 0)]
    new_expert = (i == 0) | (te_ref[i] != prev)

    @pl.when(valid & new_expert)
    def _():
        wg_b[...] = wg_ref[...].astype(BF16)
        wu_b[...] = wu_ref[...].astype(BF16)
        wd_b[...] = wd_ref[...].astype(BF16)

    @pl.when(valid)
    def _():
        x = x_ref[...]
        g = _dot(x, wg_b[...])
        u = _dot(x, wu_b[...])
        hid = (g * jax.nn.sigmoid(g) * u).astype(BF16)
        o_ref[...] = _dot(hid, wd_b[...]) * gate_ref[...]

    @pl.when(jnp.logical_not(valid))
    def _():
        o_ref[...] = jnp.zeros_like(o_ref)


def _experts(xs, gate, tile_expert, n_valid, w_gate, w_up, w_down, tm):
    m_pad, d = xs.shape
    de = w_gate.shape[-1]
    n_tiles = m_pad // tm
    grid_spec = pltpu.PrefetchScalarGridSpec(
        num_scalar_prefetch=2,
        grid=(n_tiles,),
        in_specs=[
            pl.BlockSpec((tm, d), lambda i, te, nv: (i, 0)),
            pl.BlockSpec((tm, 1), lambda i, te, nv: (i, 0)),
            pl.BlockSpec((None, d, de), lambda i, te, nv: (te[i], 0, 0)),
            pl.BlockSpec((None, d, de), lambda i, te, nv: (te[i], 0, 0)),
            pl.BlockSpec((None, de, d), lambda i, te, nv: (te[i], 0, 0)),
        ],
        out_specs=pl.BlockSpec((tm, d), lambda i, te, nv: (i, 0)),
        scratch_shapes=[
            pltpu.VMEM((d, de), BF16),
            pltpu.VMEM((d, de), BF16),
            pltpu.VMEM((de, d), BF16),
        ],
    )
    return pl.pallas_call(
        _expert_kernel,
        grid_spec=grid_spec,
        out_shape=jax.ShapeDtypeStruct((m_pad, d), F32),
        compiler_params=_params("arbitrary"),
        name="experts",
    )(tile_expert, n_valid, xs, gate, w_gate, w_up, w_down)


def _final_kernel(x_ref, y0_ref, y1_ref, g_ref, gt_ref, o_ref):
    y = y0_ref[...] + y1_ref[...]
    ms = jnp.mean(y * y, axis=-1, keepdims=True)
    o_ref[...] = x_ref[...] + gt_ref[...] * (y * lax.rsqrt(ms + RMS_EPS) * g_ref[...])


def _final(x1, y0, y1, g, gt, tm=512):
    bsz, seq, d = x1.shape
    row = lambda b, i: (b, i, 0)
    blk = pl.BlockSpec((None, tm, d), row)
    return pl.pallas_call(
        _final_kernel,
        grid=(bsz, seq // tm),
        in_specs=[blk, blk, blk,
                  pl.BlockSpec((1, d), lambda b, i: (0, 0)),
                  pl.BlockSpec((None, 1, d), lambda b, i: (b, 0, 0))],
        out_specs=blk,
        out_shape=jax.ShapeDtypeStruct((bsz, seq, d), F32),
        compiler_params=_params("arbitrary", "arbitrary"),
        name="final",
    )(x1, y0, y1, g.reshape(1, d), gt)


def _rwkv_prep(p, shift_mix, w0, w2, a0, a2, g2, k_k, k_a):
    bsz, seq, _ = p.shape
    W = RWKV_WIDTH
    prev = jnp.pad(p, ((0, 0), (1, 0), (0, 0)))[:, :-1]
    p = p + (prev - p) * shift_mix
    n_w, n_a = w2.shape[0], a2.shape[0]
    r, k, v = p[..., :W], p[..., W:2 * W], p[..., 2 * W:3 * W]
    wd = p[..., 3 * W:3 * W + n_w]
    ad = p[..., 3 * W + n_w:3 * W + n_w + n_a]
    gd = p[..., 3 * W + n_w + n_a:]
    w_log = -jax.nn.softplus(-(w0 + jnp.tanh(wd) @ w2)) - 0.5
    lw = -jnp.exp(w_log)
    a_h = jax.nn.sigmoid(a0 + ad @ a2)
    g = jax.nn.sigmoid(gd) @ g2
    kk = (k * k_k).reshape(bsz, seq, -1, RWKV_HEAD_DIM)
    kk = kk / jnp.maximum(jnp.linalg.norm(kk, axis=-1, keepdims=True), 1e-12)
    kk = kk.reshape(bsz, seq, W)
    k = k * (1.0 + (a_h - 1.0) * k_a)
    return r, k, v, lw, -kk, kk * a_h, g


def _rwkv_post(y, r, k, v, g, r_k, ln_w, ln_b):
    bsz, seq, W = y.shape
    yh = y.reshape(bsz, seq, -1, RWKV_HEAD_DIM)
    mu = jnp.mean(yh, axis=-1, keepdims=True)
    var = jnp.mean(jnp.square(yh - mu), axis=-1, keepdims=True)
    yn = ((yh - mu) * lax.rsqrt(var + RWKV_LN_EPS)).reshape(bsz, seq, W) * ln_w + ln_b
    rh = r.reshape(yh.shape)
    kh = k.reshape(yh.shape)
    vh = v.reshape(yh.shape)
    bonus = jnp.sum(rh * kh * r_k, axis=-1, keepdims=True) * vh
    return (yn + bonus.reshape(bsz, seq, W)) * g


def _route(logits, bg, be):
    n_tok = logits.shape[0]
    g_logits = logits[:, :N_GROUPS] + bg
    e_logits = logits[:, N_GROUPS:N_GROUPS + N_EXPERTS] + be
    g_prob = jax.nn.softmax(g_logits, axis=-1)
    g_sel = jnp.argmax(g_logits, axis=-1)
    p_group = jnp.take_along_axis(g_prob, g_sel[:, None], axis=1)
    e_logits = e_logits.reshape(n_tok, N_GROUPS, EXPERTS_PER_GROUP)
    e_logits = jnp.take_along_axis(e_logits, g_sel[:, None, None], axis=1)[:, 0]
    top_p, top_i = lax.top_k(jax.nn.softmax(e_logits, axis=-1), TOP_K)
    gates = p_group * top_p / jnp.sum(top_p, axis=-1, keepdims=True)
    expert_id = g_sel[:, None] * EXPERTS_PER_GROUP + top_i
    return gates, expert_id.astype(jnp.int32)


def _dispatch(expert_id, gates, tm):
    n_tok = expert_id.shape[0]
    n_rows = n_tok * TOP_K
    n_tiles = n_rows // tm + N_EXPERTS
    flat_e = expert_id.reshape(-1)
    order = jnp.argsort(flat_e).astype(jnp.int32)
    sorted_e = flat_e[order]
    counts = jnp.bincount(flat_e, length=N_EXPERTS).astype(jnp.int32)
    tiles_e = (counts + tm - 1) // tm
    tile_end = jnp.cumsum(tiles_e)
    pad_start = (tile_end - tiles_e) * tm
    start = jnp.cumsum(counts) - counts
    dest = pad_start[sorted_e] + jnp.arange(n_rows, dtype=jnp.int32) - start[sorted_e]
    src_tok = jnp.zeros((n_tiles * tm,), jnp.int32).at[dest].set(order // TOP_K)
    gate_pad = jnp.zeros((n_tiles * tm,), F32).at[dest].set(gates.reshape(-1)[order])
    slot_pos = jnp.zeros((n_rows,), jnp.int32).at[order].set(dest).reshape(n_tok, TOP_K)
    n_valid = tile_end[-1]
    t_idx = jnp.arange(n_tiles, dtype=jnp.int32)
    tile_expert = jnp.searchsorted(tile_end, jnp.minimum(t_idx, n_valid - 1), side='right')
    tile_expert = jnp.minimum(tile_expert, N_EXPERTS - 1).astype(jnp.int32)
    return src_tok, gate_pad, slot_pos, tile_expert, n_valid.reshape(1).astype(jnp.int32)


def kernel(x, c, w_ada, b_ada, g_pre_mix, g_post_mix, g_pre_ffn, g_post_ffn, w_in, w_out,
           attn_out_gain, rwkv_shift_mix, rwkv_w0, rwkv_w2, rwkv_a0, rwkv_a2, rwkv_g2,
           rwkv_k_k, rwkv_k_a, rwkv_r_k, rwkv_ln_w, rwkv_ln_b, router_group_w, router_group_b,
           router_expert_w, router_expert_b, expert_w_gate, expert_w_up, expert_w_down):
    bsz, seq, d = x.shape
    depth = w_ada.shape[0]
    tm_moe = 256
    for l in range(depth):
        mod = _adaln(c, w_ada[l], b_ada[l])
        sh_m, sc_m, gt_m, sh_f, sc_f, gt_f = jnp.split(mod[:, None, :], N_MOD, axis=-1)

        w = w_in[l]
        n_att = 3 * ATT_WIDTH
        n_rkv = 3 * RWKV_WIDTH
        n_lora = w.shape[1] - n_att - n_rkv
        n_lora_pad = -(-n_lora // LANES) * LANES
        w_att = w[:, :n_att].astype(BF16)
        w_rkv = w[:, n_att:n_att + n_rkv].astype(BF16)
        w_lora = jnp.pad(w[:, n_att + n_rkv:], ((0, 0), (0, n_lora_pad - n_lora))).astype(BF16)
        qkv, p_rkv, p_lora = _in_proj(x, g_pre_mix[l], sc_m, sh_m, w_att, w_rkv, w_lora)
        o_att = _attention(qkv, attn_out_gain[l])
        p = jnp.concatenate([p_rkv, p_lora[..., :n_lora]], axis=-1)
        r, k, v, lw, a, b, g = _rwkv_prep(p, rwkv_shift_mix[l], rwkv_w0[l], rwkv_w2[l], rwkv_a0[l],
                                          rwkv_a2[l], rwkv_g2[l], rwkv_k_k[l], rwkv_k_a[l])
        y = _rwkv_scan(r, k, v, lw, a, b)
        o_rwkv = _rwkv_post(y, r, k, v, g, rwkv_r_k[l], rwkv_ln_w[l], rwkv_ln_b[l]).astype(BF16)

        n_route = N_GROUPS + N_EXPERTS
        w_router = jnp.concatenate([router_group_w[l], router_expert_w[l]], axis=1)
        w_router = jnp.pad(w_router, ((0, 0), (0, LANES - n_route)))
        x1, h2, logits = _out_proj(o_att, o_rwkv, x, w_out[l].astype(BF16), g_post_mix[l], gt_m,
                                   g_pre_ffn[l], sc_f, sh_f, w_router)

        gates, expert_id = _route(logits.reshape(bsz * seq, LANES), router_group_b[l], router_expert_b[l])
        src_tok, gate_pad, slot_pos, tile_expert, n_valid = _dispatch(expert_id, gates, tm_moe)
        xs = h2.reshape(bsz * seq, d)[src_tok]
        rows = _experts(xs, gate_pad[:, None], tile_expert, n_valid,
                        expert_w_gate[l], expert_w_up[l], expert_w_down[l], tm_moe)
        y0 = rows[slot_pos[:, 0]].reshape(bsz, seq, d)
        y1 = rows[slot_pos[:, 1]].reshape(bsz, seq, d)
        x = _final(x1, y0, y1, g_post_ffn[l], gt_f)
    return x
```

```python
import functools

import jax
import jax.numpy as jnp
from jax import lax
from jax.experimental import pallas as pl
from jax.experimental.pallas import tpu as pltpu

F32 = jnp.float32
BF16 = jnp.bfloat16

LANES = 128
VMEM_LIMIT_BYTES = 56 * 1024 * 1024

ATT_HEADS = 8
ATT_HEAD_DIM = 128
ATT_WIDTH = ATT_HEADS * ATT_HEAD_DIM
RWKV_HEAD_DIM = 64
RWKV_WIDTH = 1024
RWKV_PAIRS = RWKV_WIDTH // LANES
RWKV_CHUNK = 64
RWKV_LN_EPS = 64e-5
RMS_EPS = 1e-6
LOG2_E = 1.4426950408889634
EXP2_CLAMP = 126.0
N_GROUPS = 8
EXPERTS_PER_GROUP = 8
N_EXPERTS = 64
TOP_K = 2
N_MOD = 6


def _dot(a, b):
    return jnp.dot(a, b, preferred_element_type=F32)


def _dot_nt(a, b):
    return lax.dot_general(a, b, (((1,), (1,)), ((), ())), preferred_element_type=F32)


def _dot_tn(a, b):
    return lax.dot_general(a, b, (((0,), (0,)), ((), ())), preferred_element_type=F32)


def _split(x):
    hi = x.astype(BF16)
    lo = (x - hi.astype(F32)).astype(BF16)
    return hi, lo


def _dot3(a, b, dot=_dot):
    ah, al = _split(a)
    bh, bl = _split(b)
    return dot(ah, bh) + dot(ah, bl) + dot(al, bh)


def _params(*sem):
    return pltpu.CompilerParams(dimension_semantics=sem, vmem_limit_bytes=VMEM_LIMIT_BYTES)


def _adaln_kernel(c_ref, w_ref, b_ref, o_ref):
    c = c_ref[...]
    s = (c * jax.nn.sigmoid(c)).astype(BF16)
    o_ref[...] = _dot(s, w_ref[...].astype(BF16)) + b_ref[...]


def _adaln(c, w, b):
    bsz, d = c.shape
    n = w.shape[1]
    rows = 8
    tn = 1536
    cp = jnp.zeros((rows, d), F32).at[:bsz].set(c)
    out = pl.pallas_call(
        _adaln_kernel,
        grid=(n // tn,),
        in_specs=[
            pl.BlockSpec((rows, d), lambda j: (0, 0)),
            pl.BlockSpec((d, tn), lambda j: (0, j)),
            pl.BlockSpec((1, tn), lambda j: (0, j)),
        ],
        out_specs=pl.BlockSpec((rows, tn), lambda j: (0, j)),
        out_shape=jax.ShapeDtypeStruct((rows, n), F32),
        compiler_params=_params("arbitrary"),
        name="adaln",
    )(cp, w, b.reshape(1, n))
    return out[:bsz]


def _in_proj_kernel(x_ref, g_ref, sc_ref, sh_ref, wa_ref, wr_ref, wl_ref,
                    oa_ref, or_ref, ol_ref):
    x = x_ref[...]
    ms = jnp.mean(x * x, axis=-1, keepdims=True)
    y = x * lax.rsqrt(ms + RMS_EPS) * g_ref[...]
    h = (y * (1.0 + sc_ref[...]) + sh_ref[...]).astype(BF16)
    oa_ref[...] = _dot(h, wa_ref[...]).astype(oa_ref.dtype)
    or_ref[...] = _dot(h, wr_ref[...])
    ol_ref[...] = _dot(h, wl_ref[...])


def _in_proj(x, g, sc, sh, w_att, w_rkv, w_lora, tm=256):
    bsz, seq, d = x.shape
    na, nr, nl = w_att.shape[1], w_rkv.shape[1], w_lora.shape[1]
    const = lambda b, i: (0, 0)
    row = lambda b, i: (b, i, 0)
    per_b = lambda b, i: (b, 0, 0)
    return pl.pallas_call(
        _in_proj_kernel,
        grid=(bsz, seq // tm),
        in_specs=[
            pl.BlockSpec((None, tm, d), row),
            pl.BlockSpec((1, d), const),
            pl.BlockSpec((None, 1, d), per_b),
            pl.BlockSpec((None, 1, d), per_b),
            pl.BlockSpec((d, na), const, pipeline_mode=pl.Buffered(1)),
            pl.BlockSpec((d, nr), const, pipeline_mode=pl.Buffered(1)),
            pl.BlockSpec((d, nl), const, pipeline_mode=pl.Buffered(1)),
        ],
        out_specs=[
            pl.BlockSpec((None, tm, na), row),
            pl.BlockSpec((None, tm, nr), row),
            pl.BlockSpec((None, tm, nl), row),
        ],
        out_shape=[
            jax.ShapeDtypeStruct((bsz, seq, na), BF16),
            jax.ShapeDtypeStruct((bsz, seq, nr), F32),
            jax.ShapeDtypeStruct((bsz, seq, nl), F32),
        ],
        compiler_params=_params("arbitrary", "arbitrary"),
        name="in_proj",
    )(x, g.reshape(1, d), sc, sh, w_att, w_rkv, w_lora)


def _att_kernel(q_ref, k_ref, v_ref, gain_ref, o_ref, *, tq, tk, scale):
    sub = LANES
    n_sub = tk // sub
    qi = pl.program_id(2)
    q = q_ref[...]
    r2 = lax.broadcasted_iota(jnp.int32, (2 * sub, 2 * sub), 0) & (sub - 1)
    c2 = lax.broadcasted_iota(jnp.int32, (2 * sub, 2 * sub), 1)
    tri_ones = jnp.where((c2 >= sub) | (r2 >= c2), 1.0, 0.0).astype(BF16)

    def tile_step(ks, carry, acc, masked):
        start = pl.multiple_of(ks * tk, tk)
        k = k_ref[pl.ds(start, tk), :]
        v = v_ref[pl.ds(start, tk), :]
        z = _dot_nt(q, k) * (scale * LOG2_E)
        sp = jnp.maximum(jnp.log2(1.0 + jnp.exp2(jnp.minimum(z, EXP2_CLAMP))), z)
        if masked:
            row = lax.broadcasted_iota(jnp.int32, (tq, tk), 0) + qi * tq
            col = lax.broadcasted_iota(jnp.int32, (tq, tk), 1) + ks * tk
            causal = col < row
            sp = jnp.where(causal, sp, 0.0)
        hi, lo = _split(sp)
        cs = [_dot(jnp.concatenate([hi[:, u * sub:(u + 1) * sub], lo[:, u * sub:(u + 1) * sub]], axis=1),
                   tri_ones) for u in range(n_sub)]
        carries = [None] * n_sub
        for u in reversed(range(n_sub)):
            carries[u] = carry
            carry = carry + cs[u][:, sub:]
        later = jnp.concatenate([cs[u][:, :sub] + carries[u] for u in range(n_sub)], axis=1)
        w = jnp.exp2(z - later)
        if masked:
            w = jnp.where(causal, w, 0.0)
        acc = acc + _dot(w.astype(BF16), v)
        return carry, acc

    zeros = jnp.zeros((tq, sub), F32)
    top = (qi * tq) // tk
    carry, acc = tile_step(top, zeros, zeros, True)

    def body(i, state):
        return tile_step(top - 1 - i, state[0], state[1], False)

    carry, acc = lax.fori_loop(0, top, body, (carry, acc))
    ms = jnp.mean(acc * acc, axis=-1, keepdims=True)
    o_ref[...] = (acc * lax.rsqrt(ms + RMS_EPS) * gain_ref[...]).astype(o_ref.dtype)


def _attention(qkv, gain, tq=512, tk=512):
    bsz, seq, _ = qkv.shape
    nh = ATT_HEADS
    assert tk % tq == 0 and seq % tk == 0
    kern = functools.partial(_att_kernel, tq=tq, tk=tk, scale=ATT_HEAD_DIM ** -0.5)
    return pl.pallas_call(
        kern,
        grid=(bsz, nh, seq // tq),
        in_specs=[
            pl.BlockSpec((None, tq, ATT_HEAD_DIM), lambda b, h, i: (b, i, h)),
            pl.BlockSpec((None, seq, ATT_HEAD_DIM), lambda b, h, i: (b, 0, nh + h)),
            pl.BlockSpec((None, seq, ATT_HEAD_DIM), lambda b, h, i: (b, 0, 2 * nh + h)),
            pl.BlockSpec((1, ATT_HEAD_DIM), lambda b, h, i: (0, h)),
        ],
        out_specs=pl.BlockSpec((None, tq, ATT_HEAD_DIM), lambda b, h, i: (b, i, h)),
        out_shape=jax.ShapeDtypeStruct((bsz, seq, ATT_WIDTH), BF16),
        compiler_params=_params("arbitrary", "arbitrary", "arbitrary"),
        name="attention",
    )(qkv, qkv, qkv, gain.reshape(1, ATT_WIDTH))


def _rwkv_kernel(r_ref, k_ref, v_ref, lw_ref, a_ref, b_ref, y_ref, s_ref, *, n_chunks):
    C = RWKV_CHUNK
    N = RWKV_HEAD_DIM

    @pl.when(pl.program_id(2) == 0)
    def _():
        s_ref[...] = jnp.zeros_like(s_ref)

    row = lax.broadcasted_iota(jnp.int32, (C, LANES), 0)
    lane = lax.broadcasted_iota(jnp.int32, (C, LANES), 1)
    col = lane & (N - 1)
    head0 = lane < N
    strict = row > col
    incl = row >= col
    eye = jnp.where(row == col, 1.0, 0.0)
    blk16 = (row >> 4) == (col >> 4)
    blk32 = (row >> 5) == (col >> 5)
    tri2 = jnp.where(incl, 1.0, 0.0).astype(BF16)
    ones = jnp.ones((C, LANES), BF16)
    r128 = lax.broadcasted_iota(jnp.int32, (LANES, LANES), 0)
    c128 = lax.broadcasted_iota(jnp.int32, (LANES, LANES), 1)
    same_head = (r128 < N) == (c128 < N)

    def bd(x):
        z = jnp.zeros_like(x)
        return jnp.concatenate([jnp.where(head0, x, z), jnp.where(head0, z, x)], axis=0)

    def mm3(x, y):
        xh, xl = _split(x)
        yh, yl = _split(y)
        byh, byl = bd(yh), bd(yl)
        rhs = jnp.concatenate([jnp.concatenate([byh, byl], axis=1),
                               jnp.concatenate([byh, jnp.zeros_like(byl)], axis=1)], axis=0)
        p = _dot(jnp.concatenate([xh, xl], axis=1), rhs)
        return p[:, :LANES] + p[:, LANES:]

    def mm1(x, y_bd):
        return _dot(x.astype(BF16), y_bd)

    cs = range(n_chunks)
    sls = [pl.ds(c * C, C) for c in cs]
    r = [r_ref[sl, :] for sl in sls]
    k = [k_ref[sl, :] for sl in sls]
    v = [v_ref[sl, :] for sl in sls]
    lw = [lw_ref[sl, :] for sl in sls]
    a = [a_ref[sl, :] for sl in sls]
    b = [b_ref[sl, :] for sl in sls]
    lws = [_split(x) for x in lw]
    cum = [_dot(tri2, jnp.concatenate([h, l], axis=0)) for h, l in lws]
    wcol = [jnp.exp(_dot_tn(h, ones) + _dot_tn(l, ones)) for h, l in lws]
    e_neg = [jnp.exp(-x) for x in cum]
    e_rem = [jnp.exp(x[C - 1:C, :] - x) for x in cum]
    rt = [r[c] * jnp.exp(cum[c]) for c in cs]
    at = [a[c] * jnp.exp(cum[c] - lw[c]) for c in cs]
    kt = [k[c] * e_neg[c] for c in cs]
    bt = [b[c] * e_neg[c] for c in cs]
    kw = [k[c] * e_rem[c] for c in cs]
    bw = [b[c] * e_rem[c] for c in cs]

    bts = [_split(x) for x in bt]
    ats = [_split(x) for x in at]
    bbt_h = [bd(h) for h, _ in bts]
    bbt_l = [bd(l) for _, l in bts]
    rt_b = [x.astype(BF16) for x in rt]
    bkt = [bd(x.astype(BF16)) for x in kt]
    lhs = [jnp.concatenate([ats[c][0], ats[c][1], rt_b[c]], axis=0) for c in cs]
    rhs = [jnp.concatenate([bbt_h[c], bbt_l[c], bkt[c]], axis=0) for c in cs]
    aa = [_dot_nt(lhs[c], rhs[c]) for c in cs]
    L = LANES
    a_ab = [jnp.where(strict, x[:C, :L] + x[:C, L:2 * L] + x[C:2 * C, :L], 0.0) for x in aa]
    a_ak = [jnp.where(strict, x[:C, 2 * L:] + x[C:2 * C, 2 * L:], 0.0) for x in aa]
    a_rb = [jnp.where(incl, x[2 * C:, :L], 0.0) for x in aa]
    a_rk = [jnp.where(incl, x[2 * C:, 2 * L:], 0.0) for x in aa]

    d = [jnp.where(blk16, x, 0.0) for x in a_ab]
    x = [eye + dd for dd in d]
    y = [mm3(dd, dd) for dd in d]
    for _ in range(2):
        x = [x[c] + mm3(x[c], y[c]) for c in cs]
        y = [mm3(yy, yy) for yy in y]
    x = [x[c] + mm3(x[c], y[c]) for c in cs]
    off = [jnp.where(blk32 & jnp.logical_not(blk16), aa, 0.0) for aa in a_ab]
    xo = [mm3(x[c], off[c]) for c in cs]
    x = [x[c] + mm3(xo[c], x[c]) for c in cs]
    off = [jnp.where(blk32, 0.0, aa) for aa in a_ab]
    xo = [mm3(x[c], off[c]) for c in cs]
    t_inv = [x[c] + mm3(xo[c], x[c]) for c in cs]

    ap = [mm3(t_inv[c], at[c]).astype(BF16) for c in cs]
    ta = [mm3(t_inv[c], a_ak[c]) for c in cs]
    bv = [bd(x.astype(BF16)) for x in v]
    u_loc = [mm1(ta[c], bv[c]) for c in cs]
    y_loc = [mm1(a_rk[c], bv[c]) for c in cs]
    a_rb = [x.astype(BF16) for x in a_rb]
    bkw = [jnp.concatenate([bw[c], kw[c]], axis=0).astype(BF16) for c in cs]

    s = s_ref[...]
    for c in cs:
        s_b = s.astype(BF16)
        u = _dot(ap[c], s_b) + u_loc[c]
        y_ref[sls[c], :] = _dot(rt_b[c], s_b) + _dot(a_rb[c], bd(u.astype(BF16))) + y_loc[c]
        ds = _dot_tn(bkw[c], jnp.concatenate([u, v[c]], axis=0).astype(BF16))
        s = s * wcol[c] + jnp.where(same_head, ds, 0.0)
    s_ref[...] = s


def _rwkv_scan(r, k, v, lw, a, b, ts=512):
    bsz, seq, width = r.shape
    spec = pl.BlockSpec((None, ts, LANES), lambda bb, p, t: (bb, t, p))
    kern = functools.partial(_rwkv_kernel, n_chunks=ts // RWKV_CHUNK)
    return pl.pallas_call(
        kern,
        grid=(bsz, width // LANES, seq // ts),
        in_specs=[spec] * 6,
        out_specs=spec,
        out_shape=jax.ShapeDtypeStruct((bsz, seq, width), F32),
        scratch_shapes=[pltpu.VMEM((LANES, LANES), F32)],
        compiler_params=_params("arbitrary", "arbitrary", "arbitrary"),
        name="rwkv",
    )(r, k, v, lw, a, b)


def _out_proj_kernel(oa_ref, or_ref, x_ref, w_ref, gpost_ref, gt_ref, gpre_ref, sc_ref, sh_ref,
                     wr_ref, x1_ref, h2_ref, lg_ref):
    half = oa_ref.shape[-1]
    y = _dot(oa_ref[...], w_ref[:half, :]) + _dot(or_ref[...], w_ref[half:, :])
    ms = jnp.mean(y * y, axis=-1, keepdims=True)
    x1 = x_ref[...] + gt_ref[...] * (y * lax.rsqrt(ms + RMS_EPS) * gpost_ref[...])
    x1_ref[...] = x1
    ms1 = jnp.mean(x1 * x1, axis=-1, keepdims=True)
    h2 = (x1 * lax.rsqrt(ms1 + RMS_EPS) * gpre_ref[...]) * (1.0 + sc_ref[...]) + sh_ref[...]
    h2_ref[...] = h2.astype(BF16)
    lg_ref[...] = _dot3(h2, wr_ref[...])


def _out_proj(o_att, o_rwkv, x, w_out, g_post, gt, g_pre, sc, sh, w_router, tm=256):
    bsz, seq, d = x.shape
    half = o_att.shape[-1]
    nr = w_router.shape[1]
    const = lambda b, i: (0, 0)
    row = lambda b, i: (b, i, 0)
    per_b = lambda b, i: (b, 0, 0)
    vec = pl.BlockSpec((1, d), const)
    mod = pl.BlockSpec((None, 1, d), per_b)
    return pl.pallas_call(
        _out_proj_kernel,
        grid=(bsz, seq // tm),
        in_specs=[
            pl.BlockSpec((None, tm, half), row),
            pl.BlockSpec((None, tm, half), row),
            pl.BlockSpec((None, tm, d), row),
            pl.BlockSpec((d, d), const, pipeline_mode=pl.Buffered(1)),
            vec, mod, vec, mod, mod,
            pl.BlockSpec((d, nr), const),
        ],
        out_specs=[
            pl.BlockSpec((None, tm, d), row),
            pl.BlockSpec((None, tm, d), row),
            pl.BlockSpec((None, tm, nr), row),
        ],
        out_shape=[
            jax.ShapeDtypeStruct((bsz, seq, d), F32),
            jax.ShapeDtypeStruct((bsz, seq, d), BF16),
            jax.ShapeDtypeStruct((bsz, seq, nr), F32),
        ],
        compiler_params=_params("arbitrary", "arbitrary"),
        name="out_proj",
    )(o_att, o_rwkv, x, w_out, g_post.reshape(1, d), gt, g_pre.reshape(1, d), sc, sh, w_router)


def _expert_kernel(te_ref, nv_ref, x_ref, gate_ref, wg_ref, wu_ref, wd_ref, o_ref,
                   wg_b, wu_b, wd_b):
    i = pl.program_id(0)
    valid = i < nv_ref[0]
    prev = te_ref[jnp.maximum(i - 1, 0)]
    new_expert = (i == 0) | (te_ref[i] != prev)

    @pl.when(valid & new_expert)
    def _():
        wg_b[...] = wg_ref[...].astype(BF16)
        wu_b[...] = wu_ref[...].astype(BF16)
        wd_b[...] = wd_ref[...].astype(BF16)

    @pl.when(valid)
    def _():
        x = x_ref[...]
        g = _dot(x, wg_b[...])
        u = _dot(x, wu_b[...])
        hid = (g * jax.nn.sigmoid(g) * u).astype(BF16)
        o_ref[...] = _dot(hid, wd_b[...]) * gate_ref[...]

    @pl.when(jnp.logical_not(valid))
    def _():
        o_ref[...] = jnp.zeros_like(o_ref)


def _experts(xs, gate, tile_expert, n_valid, w_gate, w_up, w_down, tm):
    m_pad, d = xs.shape
    de = w_gate.shape[-1]
    n_tiles = m_pad // tm
    grid_spec = pltpu.PrefetchScalarGridSpec(
        num_scalar_prefetch=2,
        grid=(n_tiles,),
        in_specs=[
            pl.BlockSpec((tm, d), lambda i, te, nv: (i, 0)),
            pl.BlockSpec((tm, 1), lambda i, te, nv: (i, 0)),
            pl.BlockSpec((None, d, de), lambda i, te, nv: (te[i], 0, 0)),
            pl.BlockSpec((None, d, de), lambda i, te, nv: (te[i], 0, 0)),
            pl.BlockSpec((None, de, d), lambda i, te, nv: (te[i], 0, 0)),
        ],
        out_specs=pl.BlockSpec((tm, d), lambda i, te, nv: (i, 0)),
        scratch_shapes=[
            pltpu.VMEM((d, de), BF16),
            pltpu.VMEM((d, de), BF16),
            pltpu.VMEM((de, d), BF16),
        ],
    )
    return pl.pallas_call(
        _expert_kernel,
        grid_spec=grid_spec,
        out_shape=jax.ShapeDtypeStruct((m_pad, d), F32),
        compiler_params=_params("arbitrary"),
        name="experts",
    )(tile_expert, n_valid, xs, gate, w_gate, w_up, w_down)


def _final_kernel(x_ref, y0_ref, y1_ref, g_ref, gt_ref, o_ref):
    y = y0_ref[...] + y1_ref[...]
    ms = jnp.mean(y * y, axis=-1, keepdims=True)
    o_ref[...] = x_ref[...] + gt_ref[...] * (y * lax.rsqrt(ms + RMS_EPS) * g_ref[...])


def _final(x1, y0, y1, g, gt, tm=512):
    bsz, seq, d = x1.shape
    row = lambda b, i: (b, i, 0)
    blk = pl.BlockSpec((None, tm, d), row)
    return pl.pallas_call(
        _final_kernel,
        grid=(bsz, seq // tm),
        in_specs=[blk, blk, blk,
                  pl.BlockSpec((1, d), lambda b, i: (0, 0)),
                  pl.BlockSpec((None, 1, d), lambda b, i: (b, 0, 0))],
        out_specs=blk,
        out_shape=jax.ShapeDtypeStruct((bsz, seq, d), F32),
        compiler_params=_params("arbitrary", "arbitrary"),
        name="final",
    )(x1, y0, y1, g.reshape(1, d), gt)


def _rwkv_prep(p, shift_mix, w0, w2, a0, a2, g2, k_k, k_a):
    bsz, seq, _ = p.shape
    W = RWKV_WIDTH
    prev = jnp.pad(p, ((0, 0), (1, 0), (0, 0)))[:, :-1]
    p = p + (prev - p) * shift_mix
    n_w, n_a = w2.shape[0], a2.shape[0]
    r, k, v = p[..., :W], p[..., W:2 * W], p[..., 2 * W:3 * W]
    wd = p[..., 3 * W:3 * W + n_w]
    ad = p[..., 3 * W + n_w:3 * W + n_w + n_a]
    gd = p[..., 3 * W + n_w + n_a:]
    w_log = -jax.nn.softplus(-(w0 + jnp.tanh(wd) @ w2)) - 0.5
    lw = -jnp.exp(w_log)
    a_h = jax.nn.sigmoid(a0 + ad @ a2)
    g = jax.nn.sigmoid(gd) @ g2
    kk = (k * k_k).reshape(bsz, seq, -1, RWKV_HEAD_DIM)
    kk = kk / jnp.maximum(jnp.linalg.norm(kk, axis=-1, keepdims=True), 1e-12)
    kk = kk.reshape(bsz, seq, W)
    k = k * (1.0 + (a_h - 1.0) * k_a)
    return r, k, v, lw, -kk, kk * a_h, g


def _rwkv_post(y, r, k, v, g, r_k, ln_w, ln_b):
    bsz, seq, W = y.shape
    yh = y.reshape(bsz, seq, -1, RWKV_HEAD_DIM)
    mu = jnp.mean(yh, axis=-1, keepdims=True)
    var = jnp.mean(jnp.square(yh - mu), axis=-1, keepdims=True)
    yn = ((yh - mu) * lax.rsqrt(var + RWKV_LN_EPS)).reshape(bsz, seq, W) * ln_w + ln_b
    rh = r.reshape(yh.shape)
    kh = k.reshape(yh.shape)
    vh = v.reshape(yh.shape)
    bonus = jnp.sum(rh * kh * r_k, axis=-1, keepdims=True) * vh
    return (yn + bonus.reshape(bsz, seq, W)) * g


def _route(logits, bg, be):
    n_tok = logits.shape[0]
    g_logits = logits[:, :N_GROUPS] + bg
    e_logits = logits[:, N_GROUPS:N_GROUPS + N_EXPERTS] + be
    g_prob = jax.nn.softmax(g_logits, axis=-1)
    g_sel = jnp.argmax(g_logits, axis=-1)
    p_group = jnp.take_along_axis(g_prob, g_sel[:, None], axis=1)
    e_logits = e_logits.reshape(n_tok, N_GROUPS, EXPERTS_PER_GROUP)
    e_logits = jnp.take_along_axis(e_logits, g_sel[:, None, None], axis=1)[:, 0]
    top_p, top_i = lax.top_k(jax.nn.softmax(e_logits, axis=-1), TOP_K)
    gates = p_group * top_p / jnp.sum(top_p, axis=-1, keepdims=True)
    expert_id = g_sel[:, None] * EXPERTS_PER_GROUP + top_i
    return gates, expert_id.astype(jnp.int32)


def _dispatch(expert_id, gates, tm):
    n_tok = expert_id.shape[0]
    n_rows = n_tok * TOP_K
    n_tiles = n_rows // tm + N_EXPERTS
    flat_e = expert_id.reshape(-1)
    order = jnp.argsort(flat_e).astype(jnp.int32)
    sorted_e = flat_e[order]
    counts = jnp.bincount(flat_e, length=N_EXPERTS).astype(jnp.int32)
    tiles_e = (counts + tm - 1) // tm
    tile_end = jnp.cumsum(tiles_e)
    pad_start = (tile_end - tiles_e) * tm
    start = jnp.cumsum(counts) - counts
    dest = pad_start[sorted_e] + jnp.arange(n_rows, dtype=jnp.int32) - start[sorted_e]
    src_tok = jnp.zeros((n_tiles * tm,), jnp.int32).at[dest].set(order // TOP_K)
    gate_pad = jnp.zeros((n_tiles * tm,), F32).at[dest].set(gates.reshape(-1)[order])
    slot_pos = jnp.zeros((n_rows,), jnp.int32).at[order].set(dest).reshape(n_tok, TOP_K)
    n_valid = tile_end[-1]
    t_idx = jnp.arange(n_tiles, dtype=jnp.int32)
    tile_expert = jnp.searchsorted(tile_end, jnp.minimum(t_idx, n_valid - 1), side='right')
    tile_expert = jnp.minimum(tile_expert, N_EXPERTS - 1).astype(jnp.int32)
    return src_tok, gate_pad, slot_pos, tile_expert, n_valid.reshape(1).astype(jnp.int32)


def kernel(x, c, w_ada, b_ada, g_pre_mix, g_post_mix, g_pre_ffn, g_post_ffn, w_in, w_out,
           attn_out_gain, rwkv_shift_mix, rwkv_w0, rwkv_w2, rwkv_a0, rwkv_a2, rwkv_g2,
           rwkv_k_k, rwkv_k_a, rwkv_r_k, rwkv_ln_w, rwkv_ln_b, router_group_w, router_group_b,
           router_expert_w, router_expert_b, expert_w_gate, expert_w_up, expert_w_down):
    bsz, seq, d = x.shape
    depth = w_ada.shape[0]
    tm_moe = 256
    for l in range(depth):
        mod = _adaln(c, w_ada[l], b_ada[l])
        sh_m, sc_m, gt_m, sh_f, sc_f, gt_f = jnp.split(mod[:, None, :], N_MOD, axis=-1)

        w = w_in[l]
        n_att = 3 * ATT_WIDTH
        n_rkv = 3 * RWKV_WIDTH
        n_lora = w.shape[1] - n_att - n_rkv
        n_lora_pad = -(-n_lora // LANES) * LANES
        w_att = w[:, :n_att].astype(BF16)
        w_rkv = w[:, n_att:n_att + n_rkv].astype(BF16)
        w_lora = jnp.pad(w[:, n_att + n_rkv:], ((0, 0), (0, n_lora_pad - n_lora))).astype(BF16)
        qkv, p_rkv, p_lora = _in_proj(x, g_pre_mix[l], sc_m, sh_m, w_att, w_rkv, w_lora)
        o_att = _attention(qkv, attn_out_gain[l])
        p = jnp.concatenate([p_rkv, p_lora[..., :n_lora]], axis=-1)
        r, k, v, lw, a, b, g = _rwkv_prep(p, rwkv_shift_mix[l], rwkv_w0[l], rwkv_w2[l], rwkv_a0[l],
                                          rwkv_a2[l], rwkv_g2[l], rwkv_k_k[l], rwkv_k_a[l])
        y = _rwkv_scan(r, k, v, lw, a, b)
        o_rwkv = _rwkv_post(y, r, k, v, g, rwkv_r_k[l], rwkv_ln_w[l], rwkv_ln_b[l]).astype(BF16)

        n_route = N_GROUPS + N_EXPERTS
        w_router = jnp.concatenate([router_group_w[l], router_expert_w[l]], axis=1)
        w_router = jnp.pad(w_router, ((0, 0), (0, LANES - n_route)))
        x1, h2, logits = _out_proj(o_att, o_rwkv, x, w_out[l].astype(BF16), g_post_mix[l], gt_m,
                                   g_pre_ffn[l], sc_f, sh_f, w_router)

        gates, expert_id = _route(logits.reshape(bsz * seq, LANES), router_group_b[l], router_expert_b[l])
        src_tok, gate_pad, slot_pos, tile_expert, n_valid = _dispatch(expert_id, gates, tm_moe)
        xs = h2.reshape(bsz * seq, d)[src_tok]
        rows = _experts(xs, gate_pad[:, None], tile_expert, n_valid,
                        expert_w_gate[l], expert_w_up[l], expert_w_down[l], tm_moe)
        y0 = rows[slot_pos[:, 0]].reshape(bsz, seq, d)
        y1 = rows[slot_pos[:, 1]].reshape(bsz, seq, d)
        x = _final(x1, y0, y1, g_post_ffn[l], gt_f)
    return x
```

```python
import functools

import jax
import jax.numpy as jnp
from jax import lax
from jax.experimental import pallas as pl
from jax.experimental.pallas import tpu as pltpu

F32 = jnp.float32
BF16 = jnp.bfloat16
U32 = jnp.uint32

LANES = 128
VMEM_LIMIT_BYTES = 56 * 1024 * 1024

ATT_HEADS = 8
ATT_HEAD_DIM = 128
ATT_WIDTH = ATT_HEADS * ATT_HEAD_DIM
RWKV_HEAD_DIM = 64
RWKV_WIDTH = 1024
RWKV_CHUNK = 64
RWKV_LN_EPS = 64e-5
RMS_EPS = 1e-6
LOG2_E = 1.4426950408889634
EXP2_CLAMP = 126.0
N_GROUPS = 8
EXPERTS_PER_GROUP = 8
N_EXPERTS = 64
TOP_K = 2
N_MOD = 6
NEG_BIG = -1e30
META_E0, META_E1, META_RANK0, META_RANK1, META_GATE0, META_GATE1 = range(6)


def _dot(a, b):
    return jnp.dot(a, b, preferred_element_type=F32)


def _dot_nt(a, b):
    return lax.dot_general(a, b, (((1,), (1,)), ((), ())), preferred_element_type=F32)


def _dot_tn(a, b):
    return lax.dot_general(a, b, (((0,), (0,)), ((), ())), preferred_element_type=F32)


def _split(x):
    hi = x.astype(BF16)
    lo = (x - hi.astype(F32)).astype(BF16)
    return hi, lo


def _dot3(a, b):
    ah, al = _split(a)
    bh, bl = _split(b)
    return _dot(ah, bh) + _dot(ah, bl) + _dot(al, bh)


def _dot2(x, m2):
    hi, lo = _split(x)
    return _dot(jnp.concatenate([hi, lo], axis=1), m2)


def _params(*sem):
    return pltpu.CompilerParams(dimension_semantics=sem, vmem_limit_bytes=VMEM_LIMIT_BYTES)


def _adaln_kernel(c_ref, w_ref, b_ref, o_ref):
    c = c_ref[...]
    s = (c * jax.nn.sigmoid(c)).astype(BF16)
    o_ref[...] = _dot(s, w_ref[...].astype(BF16)) + b_ref[...]


def _adaln(c, w, b):
    bsz, d = c.shape
    n = w.shape[1]
    rows = 8
    tn = 1536
    cp = jnp.zeros((rows, d), F32).at[:bsz].set(c)
    out = pl.pallas_call(
        _adaln_kernel,
        grid=(n // tn,),
        in_specs=[
            pl.BlockSpec((rows, d), lambda j: (0, 0)),
            pl.BlockSpec((d, tn), lambda j: (0, j)),
            pl.BlockSpec((1, tn), lambda j: (0, j)),
        ],
        out_specs=pl.BlockSpec((rows, tn), lambda j: (0, j)),
        out_shape=jax.ShapeDtypeStruct((rows, n), F32),
        compiler_params=_params("arbitrary"),
        name="adaln",
    )(cp, w, b.reshape(1, n))
    return out[:bsz]


def _in_proj_kernel(x_ref, g_ref, sc_ref, sh_ref, wa_ref, wr_ref, wl_ref,
                    oa_ref, or_ref, ol_ref):
    x = x_ref[...]
    ms = jnp.mean(x * x, axis=-1, keepdims=True)
    y = x * lax.rsqrt(ms + RMS_EPS) * g_ref[...]
    h = (y * (1.0 + sc_ref[...]) + sh_ref[...]).astype(BF16)
    oa_ref[...] = _dot(h, wa_ref[...]).astype(oa_ref.dtype)
    or_ref[...] = _dot(h, wr_ref[...])
    ol_ref[...] = _dot(h, wl_ref[...])


def _in_proj(x, g, sc, sh, w_att, w_rkv, w_lora, tm=256):
    bsz, seq, d = x.shape
    na, nr, nl = w_att.shape[1], w_rkv.shape[1], w_lora.shape[1]
    const = lambda b, i: (0, 0)
    row = lambda b, i: (b, i, 0)
    per_b = lambda b, i: (b, 0, 0)
    return pl.pallas_call(
        _in_proj_kernel,
        grid=(bsz, seq // tm),
        in_specs=[
            pl.BlockSpec((None, tm, d), row),
            pl.BlockSpec((1, d), const),
            pl.BlockSpec((None, 1, d), per_b),
            pl.BlockSpec((None, 1, d), per_b),
            pl.BlockSpec((d, na), const, pipeline_mode=pl.Buffered(1)),
            pl.BlockSpec((d, nr), const, pipeline_mode=pl.Buffered(1)),
            pl.BlockSpec((d, nl), const, pipeline_mode=pl.Buffered(1)),
        ],
        out_specs=[
            pl.BlockSpec((None, tm, na), row),
            pl.BlockSpec((None, tm, nr), row),
            pl.BlockSpec((None, tm, nl), row),
        ],
        out_shape=[
            jax.ShapeDtypeStruct((bsz, seq, na), BF16),
            jax.ShapeDtypeStruct((bsz, seq, nr), F32),
            jax.ShapeDtypeStruct((bsz, seq, nl), F32),
        ],
        compiler_params=_params("arbitrary", "arbitrary"),
        name="in_proj",
    )(x, g.reshape(1, d), sc, sh, w_att, w_rkv, w_lora)


def _att_kernel(q_ref, k_ref, v_ref, gain_ref, o_ref, *, tq, tk, scale):
    sub = LANES
    n_sub = tk // sub
    qi = pl.program_id(2)
    q = q_ref[...]
    r2 = lax.broadcasted_iota(jnp.int32, (2 * sub, 2 * sub), 0) & (sub - 1)
    c2 = lax.broadcasted_iota(jnp.int32, (2 * sub, 2 * sub), 1)
    tri_ones = jnp.where((c2 >= sub) | (r2 >= c2), 1.0, 0.0).astype(BF16)

    def tile_step(ks, carry, acc, masked):
        start = pl.multiple_of(ks * tk, tk)
        k = k_ref[pl.ds(start, tk), :]
        v = v_ref[pl.ds(start, tk), :]
        z = _dot_nt(q, k) * (scale * LOG2_E)
        sp = jnp.maximum(jnp.log2(1.0 + jnp.exp2(jnp.minimum(z, EXP2_CLAMP))), z)
        if masked:
            row = lax.broadcasted_iota(jnp.int32, (tq, tk), 0) + qi * tq
            col = lax.broadcasted_iota(jnp.int32, (tq, tk), 1) + ks * tk
            causal = col < row
            sp = jnp.where(causal, sp, 0.0)
        hi, lo = _split(sp)
        cs = [_dot(jnp.concatenate([hi[:, u * sub:(u + 1) * sub], lo[:, u * sub:(u + 1) * sub]], axis=1),
                   tri_ones) for u in range(n_sub)]
        carries = [None] * n_sub
        for u in reversed(range(n_sub)):
            carries[u] = carry
            carry = carry + cs[u][:, sub:]
        later = jnp.concatenate([cs[u][:, :sub] + carries[u] for u in range(n_sub)], axis=1)
        w = jnp.exp2(z - later)
        if masked:
            w = jnp.where(causal, w, 0.0)
        acc = acc + _dot(w.astype(BF16), v)
        return carry, acc

    zeros = jnp.zeros((tq, sub), F32)
    top = (qi * tq) // tk
    carry, acc = tile_step(top, zeros, zeros, True)

    def body(i, state):
        return tile_step(top - 1 - i, state[0], state[1], False)

    carry, acc = lax.fori_loop(0, top, body, (carry, acc))
    ms = jnp.mean(acc * acc, axis=-1, keepdims=True)
    o_ref[...] = (acc * lax.rsqrt(ms + RMS_EPS) * gain_ref[...]).astype(o_ref.dtype)


def _attention(qkv, gain, tq=512, tk=512):
    bsz, seq, _ = qkv.shape
    nh = ATT_HEADS
    assert tk % tq == 0 and seq % tk == 0
    kern = functools.partial(_att_kernel, tq=tq, tk=tk, scale=ATT_HEAD_DIM ** -0.5)
    return pl.pallas_call(
        kern,
        grid=(bsz, nh, seq // tq),
        in_specs=[
            pl.BlockSpec((None, tq, ATT_HEAD_DIM), lambda b, h, i: (b, i, h)),
            pl.BlockSpec((None, seq, ATT_HEAD_DIM), lambda b, h, i: (b, 0, nh + h)),
            pl.BlockSpec((None, seq, ATT_HEAD_DIM), lambda b, h, i: (b, 0, 2 * nh + h)),
            pl.BlockSpec((1, ATT_HEAD_DIM), lambda b, h, i: (0, h)),
        ],
        out_specs=pl.BlockSpec((None, tq, ATT_HEAD_DIM), lambda b, h, i: (b, i, h)),
        out_shape=jax.ShapeDtypeStruct((bsz, seq, ATT_WIDTH), BF16),
        compiler_params=_params("arbitrary", "arbitrary", "arbitrary"),
        name="attention",
    )(qkv, qkv, qkv, gain.reshape(1, ATT_WIDTH))


def _rwkv_kernel(pr_ref, pk_ref, pv_ref, pl_ref, mr_ref, mk_ref, mv_ref, ml_ref,
                 w2_ref, a2_ref, g2_ref, w0_ref, a0_ref, kk_ref, ka_ref, rk_ref, lnw_ref, lnb_ref,
                 o_ref, s_ref, prev_ref, prevl_ref, y_ref, *, ts):
    C = RWKV_CHUNK
    N = RWKV_HEAD_DIM
    n_chunks = ts // C

    @pl.when(pl.program_id(2) == 0)
    def _():
        s_ref[...] = jnp.zeros_like(s_ref)
        prev_ref[...] = jnp.zeros_like(prev_ref)
        prevl_ref[...] = jnp.zeros_like(prevl_ref)

    first_row = lax.broadcasted_iota(jnp.int32, (ts, 1), 0) == 0

    def shift(x, last_prev, mix):
        prev = jnp.where(first_row, last_prev, pltpu.roll(x, 1, 0))
        return x + (prev - x) * mix

    raw = [pr_ref[...], pk_ref[...], pv_ref[...]]
    raw_l = pl_ref[...]
    r_all = shift(raw[0], prev_ref[0:1, :], mr_ref[...])
    k_all = shift(raw[1], prev_ref[1:2, :], mk_ref[...])
    v_all = shift(raw[2], prev_ref[2:3, :], mv_ref[...])
    p_l = shift(raw_l, prevl_ref[...], ml_ref[...])
    for i in range(3):
        prev_ref[i:i + 1, :] = raw[i][ts - 1:ts, :]
    prevl_ref[...] = raw_l[ts - 1:ts, :]

    r128 = lax.broadcasted_iota(jnp.int32, (LANES, LANES), 0)
    c128 = lax.broadcasted_iota(jnp.int32, (LANES, LANES), 1)
    same_head = (r128 < N) == (c128 < N)
    head_ones = jnp.where(same_head, 1.0, 0.0).astype(BF16)
    head_ones2 = jnp.concatenate([head_ones, head_ones], axis=0)

    lw_pre = w0_ref[...] + _dot(jnp.tanh(p_l[:, :LANES]).astype(BF16), w2_ref[...])
    w_log = -(jnp.maximum(-lw_pre, 0.0) + jnp.log(1.0 + jnp.exp(-jnp.abs(lw_pre)))) - 0.5
    lw_all = -jnp.exp(w_log)
    rate = jax.nn.sigmoid(a0_ref[...] + _dot(p_l[:, :LANES].astype(BF16), a2_ref[...]))
    gate = _dot(jax.nn.sigmoid(p_l[:, LANES:]).astype(BF16), g2_ref[...])
    kk = k_all * kk_ref[...]
    norm = jnp.sqrt(_dot2(kk * kk, head_ones2))
    kk = kk / jnp.maximum(norm, 1e-12)
    k_all = k_all * (1.0 + (rate - 1.0) * ka_ref[...])
    a_all = -kk
    b_all = kk * rate

    row = lax.broadcasted_iota(jnp.int32, (C, LANES), 0)
    lane = lax.broadcasted_iota(jnp.int32, (C, LANES), 1)
    col = lane & (N - 1)
    head0 = lane < N
    strict = row > col
    incl = row >= col
    eye = jnp.where(row == col, 1.0, 0.0)
    blk16 = (row >> 4) == (col >> 4)
    blk32 = (row >> 5) == (col >> 5)
    tri2 = jnp.where(incl, 1.0, 0.0).astype(BF16)
    ones = jnp.ones((C, LANES), BF16)

    def bd(x):
        z = jnp.zeros_like(x)
        return jnp.concatenate([jnp.where(head0, x, z), jnp.where(head0, z, x)], axis=0)

    def mm3(x, y):
        xh, xl = _split(x)
        yh, yl = _split(y)
        byh, byl = bd(yh), bd(yl)
        rhs = jnp.concatenate([jnp.concatenate([byh, byl], axis=1),
                               jnp.concatenate([byh, jnp.zeros_like(byl)], axis=1)], axis=0)
        p = _dot(jnp.concatenate([xh, xl], axis=1), rhs)
        return p[:, :LANES] + p[:, LANES:]

    def mm1(x, y_bd):
        return _dot(x.astype(BF16), y_bd)

    cs = range(n_chunks)
    chunk = lambda x: [x[c * C:(c + 1) * C, :] for c in cs]
    r, k, v, lw, a, b = (chunk(x) for x in (r_all, k_all, v_all, lw_all, a_all, b_all))
    lws = [_split(x) for x in lw]
    cum = [_dot(tri2, jnp.concatenate([h, l], axis=0)) for h, l in lws]
    wcol = [jnp.exp(_dot_tn(h, ones) + _dot_tn(l, ones)) for h, l in lws]
    e_neg = [jnp.exp(-x) for x in cum]
    e_rem = [jnp.exp(x[C - 1:C, :] - x) for x in cum]
    rt = [r[c] * jnp.exp(cum[c]) for c in cs]
    at = [a[c] * jnp.exp(cum[c] - lw[c]) for c in cs]
    kt = [k[c] * e_neg[c] for c in cs]
    bt = [b[c] * e_neg[c] for c in cs]
    kw = [k[c] * e_rem[c] for c in cs]
    bw = [b[c] * e_rem[c] for c in cs]

    bts = [_split(x) for x in bt]
    ats = [_split(x) for x in at]
    bbt_h = [bd(h) for h, _ in bts]
    bbt_l = [bd(l) for _, l in bts]
    rt_b = [x.astype(BF16) for x in rt]
    bkt = [bd(x.astype(BF16)) for x in kt]
    lhs = [jnp.concatenate([ats[c][0], ats[c][1], rt_b[c]], axis=0) for c in cs]
    rhs = [jnp.concatenate([bbt_h[c], bbt_l[c], bkt[c]], axis=0) for c in cs]
    aa = [_dot_nt(lhs[c], rhs[c]) for c in cs]
    L = LANES
    a_ab = [jnp.where(strict, x[:C, :L] + x[:C, L:2 * L] + x[C:2 * C, :L], 0.0) for x in aa]
    a_ak = [jnp.where(strict, x[:C, 2 * L:] + x[C:2 * C, 2 * L:], 0.0) for x in aa]
    a_rb = [jnp.where(incl, x[2 * C:, :L], 0.0) for x in aa]
    a_rk = [jnp.where(incl, x[2 * C:, 2 * L:], 0.0) for x in aa]

    d = [jnp.where(blk16, x, 0.0) for x in a_ab]
    x = [eye + dd for dd in d]
    y = [mm3(dd, dd) for dd in d]
    for _ in range(2):
        x = [x[c] + mm3(x[c], y[c]) for c in cs]
        y = [mm3(yy, yy) for yy in y]
    x = [x[c] + mm3(x[c], y[c]) for c in cs]
    off = [jnp.where(blk32 & jnp.logical_not(blk16), aa_, 0.0) for aa_ in a_ab]
    xo = [mm3(x[c], off[c]) for c in cs]
    x = [x[c] + mm3(xo[c], x[c]) for c in cs]
    off = [jnp.where(blk32, 0.0, aa_) for aa_ in a_ab]
    xo = [mm3(x[c], off[c]) for c in cs]
    t_inv = [x[c] + mm3(xo[c], x[c]) for c in cs]

    ap = [mm3(t_inv[c], at[c]).astype(BF16) for c in cs]
    ta = [mm3(t_inv[c], a_ak[c]) for c in cs]
    bv = [bd(x_.astype(BF16)) for x_ in v]
    u_loc = [mm1(ta[c], bv[c]) for c in cs]
    y_loc = [mm1(a_rk[c], bv[c]) for c in cs]
    a_rb = [x_.astype(BF16) for x_ in a_rb]
    bkw = [jnp.concatenate([bw[c], kw[c]], axis=0).astype(BF16) for c in cs]

    s = s_ref[...]
    for c in cs:
        s_b = s.astype(BF16)
        u = _dot(ap[c], s_b) + u_loc[c]
        y_ref[c * C:(c + 1) * C, :] = _dot(rt_b[c], s_b) + _dot(a_rb[c], bd(u.astype(BF16))) + y_loc[c]
        ds = _dot_tn(bkw[c], jnp.concatenate([u, v[c]], axis=0).astype(BF16))
        s = s * wcol[c] + jnp.where(same_head, ds, 0.0)
    s_ref[...] = s

    inv_n = 1.0 / N
    y_all = y_ref[...]
    mu = _dot2(y_all, head_ones2) * inv_n
    dev = y_all - mu
    var = _dot2(dev * dev, head_ones2) * inv_n
    yn = dev * lax.rsqrt(var + RWKV_LN_EPS) * lnw_ref[...] + lnb_ref[...]
    bonus = _dot2(r_all * k_all * rk_ref[...], head_ones2) * v_all
    o_ref[...] = ((yn + bonus) * gate).astype(o_ref.dtype)


def _rwkv(p_rkv, p_lora, mix_rkv, mix_lora, w2p, a2p, g2p, w0, a0, k_k, k_a, r_k, ln_w, ln_b, ts=512):
    bsz, seq, _ = p_rkv.shape
    W = RWKV_WIDTH
    nl = p_lora.shape[-1]
    npair = W // LANES
    col = lambda j: pl.BlockSpec((None, ts, LANES), lambda bb, p, t, j=j: (bb, t, j * npair + p))
    vec_col = lambda j: pl.BlockSpec((1, LANES), lambda bb, p, t, j=j: (0, j * npair + p))
    vec = pl.BlockSpec((1, LANES), lambda bb, p, t: (0, p))
    mat = lambda rows: pl.BlockSpec((rows, LANES), lambda bb, p, t: (0, p))
    kern = functools.partial(_rwkv_kernel, ts=ts)
    return pl.pallas_call(
        kern,
        grid=(bsz, npair, seq // ts),
        in_specs=[
            col(0), col(1), col(2),
            pl.BlockSpec((None, ts, nl), lambda bb, p, t: (bb, t, 0)),
            vec_col(0), vec_col(1), vec_col(2),
            pl.BlockSpec((1, nl), lambda bb, p, t: (0, 0)),
            mat(LANES), mat(LANES), mat(nl - LANES),
            vec, vec, vec, vec, vec, vec, vec,
        ],
        out_specs=pl.BlockSpec((None, ts, LANES), lambda bb, p, t: (bb, t, p)),
        out_shape=jax.ShapeDtypeStruct((bsz, seq, W), BF16),
        scratch_shapes=[
            pltpu.VMEM((LANES, LANES), F32),
            pltpu.VMEM((8, LANES), F32),
            pltpu.VMEM((1, nl), F32),
            pltpu.VMEM((ts, LANES), F32),
        ],
        compiler_params=_params("arbitrary", "arbitrary", "arbitrary"),
        name="rwkv",
    )(p_rkv, p_rkv, p_rkv, p_lora, mix_rkv, mix_rkv, mix_rkv, mix_lora,
      w2p, a2p, g2p, w0, a0, k_k, k_a, r_k, ln_w, ln_b)


def _out_proj_kernel(oa_ref, or_ref, x_ref, w_ref, gpost_ref, gt_ref, gpre_ref, sc_ref, sh_ref,
                     wr_ref, br_ref, x1_ref, h2_ref, meta_ref, cnt_ref, run_ref):
    tm = x_ref.shape[0]
    half = oa_ref.shape[-1]

    @pl.when((pl.program_id(0) == 0) & (pl.program_id(1) == 0))
    def _():
        run_ref[...] = jnp.zeros_like(run_ref)

    y = _dot(oa_ref[...], w_ref[:half, :]) + _dot(or_ref[...], w_ref[half:, :])
    ms = jnp.mean(y * y, axis=-1, keepdims=True)
    x1 = x_ref[...] + gt_ref[...] * (y * lax.rsqrt(ms + RMS_EPS) * gpost_ref[...])
    x1_ref[...] = x1
    ms1 = jnp.mean(x1 * x1, axis=-1, keepdims=True)
    h2 = (x1 * lax.rsqrt(ms1 + RMS_EPS) * gpre_ref[...]) * (1.0 + sc_ref[...]) + sh_ref[...]
    d2 = h2.shape[-1] // 2
    lo_bits = lax.bitcast_convert_type(h2[:, :d2].astype(BF16).astype(F32), U32)
    hi_bits = lax.bitcast_convert_type(h2[:, d2:].astype(BF16).astype(F32), U32)
    h2_ref[...] = (lo_bits >> 16) | hi_bits

    lg = _dot3(h2, wr_ref[...]) + br_ref[...]
    lane_i = lax.broadcasted_iota(jnp.int32, lg.shape, 1)
    lane = lane_i.astype(F32)
    lane_group = ((lane_i - N_GROUPS) >> 3).astype(F32)
    first = lambda mask: jnp.min(jnp.where(mask, lane, 4.0 * LANES), axis=-1, keepdims=True)
    gl = jnp.where(lane_i < N_GROUPS, lg, NEG_BIG)
    gmax = jnp.max(gl, axis=-1, keepdims=True)
    g_sel = first(gl == gmax)
    p_group = 1.0 / jnp.sum(jnp.exp(gl - gmax), axis=-1, keepdims=True)
    in_group = (lane_i >= N_GROUPS) & (lane_group == g_sel)
    el = jnp.where(in_group, lg, NEG_BIG)
    m0 = jnp.max(el, axis=-1, keepdims=True)
    i0 = first(el == m0)
    el = jnp.where(lane == i0, NEG_BIG, el)
    m1 = jnp.max(el, axis=-1, keepdims=True)
    i1 = first(el == m1)
    t = jnp.exp(m1 - m0)
    gate0 = p_group / (1.0 + t)
    gate1 = p_group * t / (1.0 + t)
    e0 = i0 - N_GROUPS
    e1 = i1 - N_GROUPS
    hit = (lane == e0) | (lane == e1)
    onehot = jnp.where(hit, 1.0, 0.0).astype(BF16)
    rr = lax.broadcasted_iota(jnp.int32, (tm, tm), 0)
    cc = lax.broadcasted_iota(jnp.int32, (tm, tm), 1)
    before = jnp.where(cc < rr, 1.0, 0.0).astype(BF16)
    seen = _dot(before, onehot) + run_ref[...]
    rank0 = jnp.sum(jnp.where(lane == e0, seen, 0.0), axis=-1, keepdims=True)
    rank1 = jnp.sum(jnp.where(lane == e1, seen, 0.0), axis=-1, keepdims=True)
    total = seen[tm - 1:tm, :] + jnp.where(hit[tm - 1:tm, :], 1.0, 0.0)
    run_ref[...] = total
    cnt_ref[...] = total
    meta = jnp.zeros(lg.shape, F32)
    for idx, val in ((META_E0, e0.astype(F32)), (META_E1, e1.astype(F32)), (META_RANK0, rank0),
                     (META_RANK1, rank1), (META_GATE0, gate0), (META_GATE1, gate1)):
        meta = jnp.where(lane == idx, val, meta)
    meta_ref[...] = meta


def _out_proj(o_att, o_rwkv, x, w_out, g_post, gt, g_pre, sc, sh, w_router, b_router, tm=256):
    bsz, seq, d = x.shape
    half = o_att.shape[-1]
    const = lambda b, i: (0, 0)
    row = lambda b, i: (b, i, 0)
    per_b = lambda b, i: (b, 0, 0)
    vec = pl.BlockSpec((1, d), const)
    mod = pl.BlockSpec((None, 1, d), per_b)
    return pl.pallas_call(
        _out_proj_kernel,
        grid=(bsz, seq // tm),
        in_specs=[
            pl.BlockSpec((None, tm, half), row),
            pl.BlockSpec((None, tm, half), row),
            pl.BlockSpec((None, tm, d), row),
            pl.BlockSpec((d, d), const, pipeline_mode=pl.Buffered(1)),
            vec, mod, vec, mod, mod,
            pl.BlockSpec((d, LANES), const),
            pl.BlockSpec((1, LANES), const),
        ],
        out_specs=[
            pl.BlockSpec((None, tm, d), row),
            pl.BlockSpec((None, tm, d // 2), row),
            pl.BlockSpec((None, tm, LANES), row),
            pl.BlockSpec((1, LANES), const),
        ],
        out_shape=[
            jax.ShapeDtypeStruct((bsz, seq, d), F32),
            jax.ShapeDtypeStruct((bsz, seq, d // 2), U32),
            jax.ShapeDtypeStruct((bsz, seq, LANES), F32),
            jax.ShapeDtypeStruct((1, LANES), F32),
        ],
        scratch_shapes=[pltpu.VMEM((1, LANES), F32)],
        compiler_params=_params("arbitrary", "arbitrary"),
        name="out_proj",
    )(o_att, o_rwkv, x, w_out, g_post.reshape(1, d), gt, g_pre.reshape(1, d), sc, sh, w_router, b_router)


def _expert_kernel(te_ref, nv_ref, x_ref, wg_ref, wu_ref, wd_ref, o_ref, wg_b, wu_b, wd_b):
    i = pl.program_id(0)
    valid = i < nv_ref[0]
    prev = te_ref[jnp.maximum(i - 1, 0)]
    new_expert = (i == 0) | (te_ref[i] != prev)

    @pl.when(valid & new_expert)
    def _():
        wg_b[...] = wg_ref[...].astype(BF16)
        wu_b[...] = wu_ref[...].astype(BF16)
        wd_b[...] = wd_ref[...].astype(BF16)

    @pl.when(valid)
    def _():
        words = x_ref[...]
        d2 = words.shape[-1]
        x_lo = lax.bitcast_convert_type(words << 16, F32).astype(BF16)
        x_hi = lax.bitcast_convert_type(words & jnp.uint32(0xFFFF0000), F32).astype(BF16)
        g = _dot(x_lo, wg_b[:d2, :]) + _dot(x_hi, wg_b[d2:, :])
        u = _dot(x_lo, wu_b[:d2, :]) + _dot(x_hi, wu_b[d2:, :])
        hid = (g * jax.nn.sigmoid(g) * u).astype(BF16)
        o_ref[...] = _dot(hid, wd_b[...])

    @pl.when(jnp.logical_not(valid))
    def _():
        o_ref[...] = jnp.zeros_like(o_ref)


def _experts(xs, tile_expert, n_valid, w_gate, w_up, w_down, tm):
    m_pad, d2 = xs.shape
    d = 2 * d2
    de = w_gate.shape[-1]
    n_tiles = m_pad // tm
    grid_spec = pltpu.PrefetchScalarGridSpec(
        num_scalar_prefetch=2,
        grid=(n_tiles,),
        in_specs=[
            pl.BlockSpec((tm, d2), lambda i, te, nv: (i, 0)),
            pl.BlockSpec((None, d, de), lambda i, te, nv: (te[i], 0, 0)),
            pl.BlockSpec((None, d, de), lambda i, te, nv: (te[i], 0, 0)),
            pl.BlockSpec((None, de, d), lambda i, te, nv: (te[i], 0, 0)),
        ],
        out_specs=pl.BlockSpec((tm, d), lambda i, te, nv: (i, 0)),
        scratch_shapes=[
            pltpu.VMEM((d, de), BF16),
            pltpu.VMEM((d, de), BF16),
            pltpu.VMEM((de, d), BF16),
        ],
    )
    return pl.pallas_call(
        _expert_kernel,
        grid_spec=grid_spec,
        out_shape=jax.ShapeDtypeStruct((m_pad, d), F32),
        compiler_params=_params("arbitrary"),
        name="experts",
    )(tile_expert, n_valid, xs, w_gate, w_up, w_down)


def _final_kernel(x_ref, y0_ref, y1_ref, meta_ref, g_ref, gt_ref, o_ref):
    meta = meta_ref[...]
    y = y0_ref[...] * meta[:, META_GATE0:META_GATE0 + 1] + y1_ref[...] * meta[:, META_GATE1:META_GATE1 + 1]
    ms = jnp.mean(y * y, axis=-1, keepdims=True)
    o_ref[...] = x_ref[...] + gt_ref[...] * (y * lax.rsqrt(ms + RMS_EPS) * g_ref[...])


def _final(x1, y0, y1, meta, g, gt, tm=512):
    bsz, seq, d = x1.shape
    row = lambda b, i: (b, i, 0)
    blk = pl.BlockSpec((None, tm, d), row)
    return pl.pallas_call(
        _final_kernel,
        grid=(bsz, seq // tm),
        in_specs=[blk, blk, blk,
                  pl.BlockSpec((None, tm, LANES), row),
                  pl.BlockSpec((1, d), lambda b, i: (0, 0)),
                  pl.BlockSpec((None, 1, d), lambda b, i: (b, 0, 0))],
        out_specs=blk,
        out_shape=jax.ShapeDtypeStruct((bsz, seq, d), F32),
        compiler_params=_params("arbitrary", "arbitrary"),
        name="final",
    )(x1, y0, y1, meta, g.reshape(1, d), gt)


def _dispatch(meta, counts, tm):
    n_tok = meta.shape[0]
    n_rows = n_tok * TOP_K
    n_tiles = n_rows // tm + N_EXPERTS
    expert = meta[:, META_E0:META_E1 + 1].astype(jnp.int32)
    rank = meta[:, META_RANK0:META_RANK1 + 1].astype(jnp.int32)
    counts = counts[0, :N_EXPERTS].astype(jnp.int32)
    tiles_e = (counts + tm - 1) // tm
    tile_end = jnp.cumsum(tiles_e)
    pad_start = (tile_end - tiles_e) * tm
    slot_pos = pad_start[expert] + rank
    tok = jnp.broadcast_to(jnp.arange(n_tok, dtype=jnp.int32)[:, None], (n_tok, TOP_K))
    src_tok = jnp.zeros((n_tiles * tm,), jnp.int32).at[slot_pos.reshape(-1)].set(tok.reshape(-1))
    n_valid = tile_end[-1]
    t_idx = jnp.arange(n_tiles, dtype=jnp.int32)
    tile_expert = jnp.searchsorted(tile_end, jnp.minimum(t_idx, n_valid - 1), side='right')
    tile_expert = jnp.minimum(tile_expert, N_EXPERTS - 1).astype(jnp.int32)
    return src_tok, slot_pos, tile_expert, n_valid.reshape(1).astype(jnp.int32)


def _pad_rows(w, rows_before, rows_total):
    return jnp.pad(w, ((rows_before, rows_total - rows_before - w.shape[0]), (0, 0))).astype(BF16)


def kernel(x, c, w_ada, b_ada, g_pre_mix, g_post_mix, g_pre_ffn, g_post_ffn, w_in, w_out,
           attn_out_gain, rwkv_shift_mix, rwkv_w0, rwkv_w2, rwkv_a0, rwkv_a2, rwkv_g2,
           rwkv_k_k, rwkv_k_a, rwkv_r_k, rwkv_ln_w, rwkv_ln_b, router_group_w, router_group_b,
           router_expert_w, router_expert_b, expert_w_gate, expert_w_up, expert_w_down):
    bsz, seq, d = x.shape
    depth = w_ada.shape[0]
    tm_moe = 256
    W = RWKV_WIDTH
    for l in range(depth):
        mod = _adaln(c, w_ada[l], b_ada[l])
        sh_m, sc_m, gt_m, sh_f, sc_f, gt_f = jnp.split(mod[:, None, :], N_MOD, axis=-1)

        w = w_in[l]
        n_att = 3 * ATT_WIDTH
        n_rkv = 3 * W
        n_w, n_a, n_g = rwkv_w2.shape[1], rwkv_a2.shape[1], rwkv_g2.shape[1]
        assert n_w + n_a == LANES and w.shape[1] == n_att + n_rkv + n_w + n_a + n_g
        n_lora = LANES + -(-n_g // LANES) * LANES
        pad_l = n_lora - (n_w + n_a + n_g)
        w_att = w[:, :n_att].astype(BF16)
        w_rkv = w[:, n_att:n_att + n_rkv].astype(BF16)
        w_lora = jnp.pad(w[:, n_att + n_rkv:], ((0, 0), (0, pad_l))).astype(BF16)
        qkv, p_rkv, p_lora = _in_proj(x, g_pre_mix[l], sc_m, sh_m, w_att, w_rkv, w_lora)
        o_att = _attention(qkv, attn_out_gain[l])
        mix = rwkv_shift_mix[l]
        row = lambda t: t.reshape(1, -1)
        o_rwkv = _rwkv(
            p_rkv, p_lora, row(mix[:n_rkv]), row(jnp.pad(mix[n_rkv:], (0, pad_l))),
            _pad_rows(rwkv_w2[l], 0, LANES), _pad_rows(rwkv_a2[l], n_w, LANES),
            _pad_rows(rwkv_g2[l], 0, n_lora - LANES),
            row(rwkv_w0[l]), row(rwkv_a0[l]), row(rwkv_k_k[l]), row(rwkv_k_a[l]), row(rwkv_r_k[l]),
            row(rwkv_ln_w[l]), row(rwkv_ln_b[l]))

        n_route = N_GROUPS + N_EXPERTS
        w_router = jnp.concatenate([router_group_w[l], router_expert_w[l]], axis=1)
        w_router = jnp.pad(w_router, ((0, 0), (0, LANES - n_route)))
        b_router = jnp.pad(jnp.concatenate([router_group_b[l], router_expert_b[l]]), (0, LANES - n_route))
        x1, h2w, meta, counts = _out_proj(o_att, o_rwkv, x, w_out[l].astype(BF16), g_post_mix[l], gt_m,
                                          g_pre_ffn[l], sc_f, sh_f, w_router, b_router.reshape(1, LANES))

        src_tok, slot_pos, tile_expert, n_valid = _dispatch(meta.reshape(bsz * seq, LANES), counts, tm_moe)
        xs = h2w.reshape(bsz * seq, d // 2)[src_tok]
        rows = _experts(xs, tile_expert, n_valid, expert_w_gate[l], expert_w_up[l], expert_w_down[l], tm_moe)
        y0 = rows[slot_pos[:, 0]].reshape(bsz, seq, d)
        y1 = rows[slot_pos[:, 1]].reshape(bsz, seq, d)
        x = _final(x1, y0, y1, meta, g_post_ffn[l], gt_f)
    return x
```

```python
import functools

import jax
import jax.numpy as jnp
from jax import lax
from jax.experimental import pallas as pl
from jax.experimental.pallas import tpu as pltpu

F32 = jnp.float32
BF16 = jnp.bfloat16
U32 = jnp.uint32

LANES = 128
VMEM_LIMIT_BYTES = 56 * 1024 * 1024

ATT_HEADS = 8
ATT_HEAD_DIM = 128
ATT_WIDTH = ATT_HEADS * ATT_HEAD_DIM
RWKV_HEAD_DIM = 64
RWKV_WIDTH = 1024
RWKV_CHUNK = 64
RWKV_LN_EPS = 64e-5
RMS_EPS = 1e-6
LOG2_E = 1.4426950408889634
EXP2_CLAMP = 126.0
N_GROUPS = 8
EXPERTS_PER_GROUP = 8
N_EXPERTS = 64
TOP_K = 2
N_MOD = 6
NEG_BIG = -1e30
META_E0, META_E1, META_RANK0, META_RANK1, META_GATE0, META_GATE1 = range(6)


def _dot(a, b):
    return jnp.dot(a, b, preferred_element_type=F32)


def _dot_nt(a, b):
    return lax.dot_general(a, b, (((1,), (1,)), ((), ())), preferred_element_type=F32)


def _dot_tn(a, b):
    return lax.dot_general(a, b, (((0,), (0,)), ((), ())), preferred_element_type=F32)


def _split(x):
    hi = x.astype(BF16)
    lo = (x - hi.astype(F32)).astype(BF16)
    return hi, lo


def _dot3(a, b):
    ah, al = _split(a)
    bh, bl = _split(b)
    return _dot(ah, bh) + _dot(ah, bl) + _dot(al, bh)


def _dot2(x, m2):
    hi, lo = _split(x)
    return _dot(jnp.concatenate([hi, lo], axis=1), m2)


def _params(*sem):
    return pltpu.CompilerParams(dimension_semantics=sem, vmem_limit_bytes=VMEM_LIMIT_BYTES)


def _adaln_kernel(c_ref, w_ref, b_ref, o_ref):
    c = c_ref[...]
    s = (c * jax.nn.sigmoid(c)).astype(BF16)
    o_ref[...] = _dot(s, w_ref[...].astype(BF16)) + b_ref[...]


def _adaln(c, w, b):
    bsz, d = c.shape
    n = w.shape[1]
    rows = 8
    tn = 1536
    cp = jnp.zeros((rows, d), F32).at[:bsz].set(c)
    out = pl.pallas_call(
        _adaln_kernel,
        grid=(n // tn,),
        in_specs=[
            pl.BlockSpec((rows, d), lambda j: (0, 0)),
            pl.BlockSpec((d, tn), lambda j: (0, j)),
            pl.BlockSpec((1, tn), lambda j: (0, j)),
        ],
        out_specs=pl.BlockSpec((rows, tn), lambda j: (0, j)),
        out_shape=jax.ShapeDtypeStruct((rows, n), F32),
        compiler_params=_params("arbitrary"),
        name="adaln",
    )(cp, w, b.reshape(1, n))
    return out[:bsz]


def _in_proj_kernel(x_ref, g_ref, sc_ref, sh_ref, wa_ref, wr_ref, wl_ref, qs_ref,
                    oa_ref, or_ref, ol_ref):
    x = x_ref[...]
    ms = jnp.mean(x * x, axis=-1, keepdims=True)
    y = x * lax.rsqrt(ms + RMS_EPS) * g_ref[...]
    h = (y * (1.0 + sc_ref[...]) + sh_ref[...]).astype(BF16)
    oa_ref[...] = (_dot(h, wa_ref[...]) * qs_ref[...]).astype(oa_ref.dtype)
    or_ref[...] = _dot(h, wr_ref[...])
    ol_ref[...] = _dot(h, wl_ref[...])


def _in_proj(x, g, sc, sh, w_att, w_rkv, w_lora, tm=256):
    bsz, seq, d = x.shape
    na, nr, nl = w_att.shape[1], w_rkv.shape[1], w_lora.shape[1]
    q_scale = jnp.where(jnp.arange(na) < ATT_WIDTH, LOG2_E * ATT_HEAD_DIM ** -0.5, 1.0).astype(F32).reshape(1, na)
    const = lambda b, i: (0, 0)
    row = lambda b, i: (b, i, 0)
    per_b = lambda b, i: (b, 0, 0)
    return pl.pallas_call(
        _in_proj_kernel,
        grid=(bsz, seq // tm),
        in_specs=[
            pl.BlockSpec((None, tm, d), row),
            pl.BlockSpec((1, d), const),
            pl.BlockSpec((None, 1, d), per_b),
            pl.BlockSpec((None, 1, d), per_b),
            pl.BlockSpec((d, na), const, pipeline_mode=pl.Buffered(1)),
            pl.BlockSpec((d, nr), const, pipeline_mode=pl.Buffered(1)),
            pl.BlockSpec((d, nl), const, pipeline_mode=pl.Buffered(1)),
            pl.BlockSpec((1, na), const),
        ],
        out_specs=[
            pl.BlockSpec((None, tm, na), row),
            pl.BlockSpec((None, tm, nr), row),
            pl.BlockSpec((None, tm, nl), row),
        ],
        out_shape=[
            jax.ShapeDtypeStruct((bsz, seq, na), BF16),
            jax.ShapeDtypeStruct((bsz, seq, nr), F32),
            jax.ShapeDtypeStruct((bsz, seq, nl), F32),
        ],
        compiler_params=_params("arbitrary", "arbitrary"),
        name="in_proj",
    )(x, g.reshape(1, d), sc, sh, w_att, w_rkv, w_lora, q_scale)


def _att_kernel(q_ref, k_ref, v_ref, gain_ref, o_ref, *, tq, tk):
    sub = LANES
    n_sub = tk // sub
    qi = pl.program_id(2)
    q = q_ref[...]
    r2 = lax.broadcasted_iota(jnp.int32, (2 * sub, 2 * sub), 0) & (sub - 1)
    c2 = lax.broadcasted_iota(jnp.int32, (2 * sub, 2 * sub), 1)
    tri_ones = jnp.where((c2 >= sub) | (r2 >= c2), 1.0, 0.0).astype(BF16)

    def tile_step(ks, carry, acc, masked):
        start = pl.multiple_of(ks * tk, tk)
        k = k_ref[pl.ds(start, tk), :]
        v = v_ref[pl.ds(start, tk), :]
        z = _dot_nt(q, k)
        sp = jnp.maximum(jnp.log2(1.0 + jnp.exp2(jnp.minimum(z, EXP2_CLAMP))), z)
        if masked:
            row = lax.broadcasted_iota(jnp.int32, (tq, tk), 0) + qi * tq
            col = lax.broadcasted_iota(jnp.int32, (tq, tk), 1) + ks * tk
            causal = col < row
            sp = jnp.where(causal, sp, 0.0)
        hi, lo = _split(sp)
        cs = [_dot(jnp.concatenate([hi[:, u * sub:(u + 1) * sub], lo[:, u * sub:(u + 1) * sub]], axis=1),
                   tri_ones) for u in range(n_sub)]
        carries = [None] * n_sub
        for u in reversed(range(n_sub)):
            carries[u] = carry
            carry = carry + cs[u][:, sub:]
        later = jnp.concatenate([cs[u][:, :sub] + carries[u] for u in range(n_sub)], axis=1)
        w = jnp.exp2(z - later)
        if masked:
            w = jnp.where(causal, w, 0.0)
        acc = acc + _dot(w.astype(BF16), v)
        return carry, acc

    zeros = jnp.zeros((tq, sub), F32)
    top = (qi * tq) // tk
    carry, acc = tile_step(top, zeros, zeros, True)

    def body(i, state):
        return tile_step(top - 1 - i, state[0], state[1], False)

    carry, acc = lax.fori_loop(0, top, body, (carry, acc))
    ms = jnp.mean(acc * acc, axis=-1, keepdims=True)
    o_ref[...] = (acc * lax.rsqrt(ms + RMS_EPS) * gain_ref[...]).astype(o_ref.dtype)


def _attention(qkv, gain, tq=512, tk=512):
    bsz, seq, _ = qkv.shape
    nh = ATT_HEADS
    assert tk % tq == 0 and seq % tk == 0
    kern = functools.partial(_att_kernel, tq=tq, tk=tk)
    return pl.pallas_call(
        kern,
        grid=(bsz, nh, seq // tq),
        in_specs=[
            pl.BlockSpec((None, tq, ATT_HEAD_DIM), lambda b, h, i: (b, i, h)),
            pl.BlockSpec((None, seq, ATT_HEAD_DIM), lambda b, h, i: (b, 0, nh + h)),
            pl.BlockSpec((None, seq, ATT_HEAD_DIM), lambda b, h, i: (b, 0, 2 * nh + h)),
            pl.BlockSpec((1, ATT_HEAD_DIM), lambda b, h, i: (0, h)),
        ],
        out_specs=pl.BlockSpec((None, tq, ATT_HEAD_DIM), lambda b, h, i: (b, i, h)),
        out_shape=jax.ShapeDtypeStruct((bsz, seq, ATT_WIDTH), BF16),
        compiler_params=_params("arbitrary", "arbitrary", "arbitrary"),
        name="attention",
    )(qkv, qkv, qkv, gain.reshape(1, ATT_WIDTH))


def _rwkv_kernel(pr_ref, pk_ref, pv_ref, pl_ref, mr_ref, mk_ref, mv_ref, ml_ref,
                 w2_ref, a2_ref, g2_ref, w0_ref, a0_ref, kk_ref, ka_ref, rk_ref, lnw_ref, lnb_ref,
                 o_ref, s_ref, prev_ref, prevl_ref, y_ref, *, ts):
    C = RWKV_CHUNK
    N = RWKV_HEAD_DIM
    n_chunks = ts // C

    @pl.when(pl.program_id(2) == 0)
    def _():
        s_ref[...] = jnp.zeros_like(s_ref)
        prev_ref[...] = jnp.zeros_like(prev_ref)
        prevl_ref[...] = jnp.zeros_like(prevl_ref)

    first_row = lax.broadcasted_iota(jnp.int32, (ts, 1), 0) == 0

    def shift(x, last_prev, mix):
        prev = jnp.where(first_row, last_prev, pltpu.roll(x, 1, 0))
        return x + (prev - x) * mix

    raw = [pr_ref[...], pk_ref[...], pv_ref[...]]
    raw_l = pl_ref[...]
    r_all = shift(raw[0], prev_ref[0:1, :], mr_ref[...])
    k_all = shift(raw[1], prev_ref[1:2, :], mk_ref[...])
    v_all = shift(raw[2], prev_ref[2:3, :], mv_ref[...])
    p_l = shift(raw_l, prevl_ref[...], ml_ref[...])
    for i in range(3):
        prev_ref[i:i + 1, :] = raw[i][ts - 1:ts, :]
    prevl_ref[...] = raw_l[ts - 1:ts, :]

    wd = o_ref.shape[-1]
    n_pairs = wd // LANES
    r128 = lax.broadcasted_iota(jnp.int32, (LANES, LANES), 0)
    c128 = lax.broadcasted_iota(jnp.int32, (LANES, LANES), 1)
    same_head = (r128 < N) == (c128 < N)
    rw = lax.broadcasted_iota(jnp.int32, (wd, wd), 0)
    cw = lax.broadcasted_iota(jnp.int32, (wd, wd), 1)
    head_ones = jnp.where((rw >> 6) == (cw >> 6), 1.0, 0.0).astype(BF16)
    head_ones2 = jnp.concatenate([head_ones, head_ones], axis=0)

    lw_pre = w0_ref[...] + _dot(jnp.tanh(p_l[:, :LANES]).astype(BF16), w2_ref[...])
    w_log = -(jnp.maximum(-lw_pre, 0.0) + jnp.log(1.0 + jnp.exp(-jnp.abs(lw_pre)))) - 0.5
    lw_all = -jnp.exp(w_log)
    rate = jax.nn.sigmoid(a0_ref[...] + _dot(p_l[:, :LANES].astype(BF16), a2_ref[...]))
    gate = _dot(jax.nn.sigmoid(p_l[:, LANES:]).astype(BF16), g2_ref[...])
    kk = k_all * kk_ref[...]
    norm = jnp.sqrt(_dot2(kk * kk, head_ones2))
    kk = kk / jnp.maximum(norm, 1e-12)
    k_all = k_all * (1.0 + (rate - 1.0) * ka_ref[...])
    a_all = -kk
    b_all = kk * rate

    row = lax.broadcasted_iota(jnp.int32, (C, LANES), 0)
    lane = lax.broadcasted_iota(jnp.int32, (C, LANES), 1)
    col = lane & (N - 1)
    head0 = lane < N
    strict = row > col
    incl = row >= col
    eye = jnp.where(row == col, 1.0, 0.0)
    blk16 = (row >> 4) == (col >> 4)
    blk32 = (row >> 5) == (col >> 5)
    tri2 = jnp.where(incl, 1.0, 0.0).astype(BF16)
    ones = jnp.ones((C, LANES), BF16)

    def bd(x):
        z = jnp.zeros_like(x)
        return jnp.concatenate([jnp.where(head0, x, z), jnp.where(head0, z, x)], axis=0)

    def mm3(x, y):
        xh, xl = _split(x)
        yh, yl = _split(y)
        byh, byl = bd(yh), bd(yl)
        rhs = jnp.concatenate([jnp.concatenate([byh, byl], axis=1),
                               jnp.concatenate([byh, jnp.zeros_like(byl)], axis=1)], axis=0)
        p = _dot(jnp.concatenate([xh, xl], axis=1), rhs)
        return p[:, :LANES] + p[:, LANES:]

    def mm1(x, y_bd):
        return _dot(x.astype(BF16), y_bd)

    cs = range(n_pairs * n_chunks)
    chunk = lambda x: [x[c * C:(c + 1) * C, j * LANES:(j + 1) * LANES]
                       for j in range(n_pairs) for c in range(n_chunks)]
    r, k, v, lw, a, b = (chunk(x) for x in (r_all, k_all, v_all, lw_all, a_all, b_all))
    lws = [_split(x) for x in lw]
    cum = [_dot(tri2, jnp.concatenate([h, l], axis=0)) for h, l in lws]
    wcol = [jnp.exp(_dot_tn(h, ones) + _dot_tn(l, ones)) for h, l in lws]
    e_neg = [jnp.exp(-x) for x in cum]
    e_rem = [jnp.exp(x[C - 1:C, :] - x) for x in cum]
    rt = [r[c] * jnp.exp(cum[c]) for c in cs]
    at = [a[c] * jnp.exp(cum[c] - lw[c]) for c in cs]
    kt = [k[c] * e_neg[c] for c in cs]
    bt = [b[c] * e_neg[c] for c in cs]
    kw = [k[c] * e_rem[c] for c in cs]
    bw = [b[c] * e_rem[c] for c in cs]

    bts = [_split(x) for x in bt]
    ats = [_split(x) for x in at]
    bbt_h = [bd(h) for h, _ in bts]
    bbt_l = [bd(l) for _, l in bts]
    rt_b = [x.astype(BF16) for x in rt]
    bkt = [bd(x.astype(BF16)) for x in kt]
    lhs = [jnp.concatenate([ats[c][0], ats[c][1], rt_b[c]], axis=0) for c in cs]
    rhs = [jnp.concatenate([bbt_h[c], bbt_l[c], bkt[c]], axis=0) for c in cs]
    aa = [_dot_nt(lhs[c], rhs[c]) for c in cs]
    L = LANES
    a_ab = [jnp.where(strict, x[:C, :L] + x[:C, L:2 * L] + x[C:2 * C, :L], 0.0) for x in aa]
    a_ak = [jnp.where(strict, x[:C, 2 * L:] + x[C:2 * C, 2 * L:], 0.0) for x in aa]
    a_rb = [jnp.where(incl, x[2 * C:, :L], 0.0) for x in aa]
    a_rk = [jnp.where(incl, x[2 * C:, 2 * L:], 0.0) for x in aa]

    d = [jnp.where(blk16, x, 0.0) for x in a_ab]
    x = [eye + dd for dd in d]
    y = [mm3(dd, dd) for dd in d]
    for _ in range(2):
        x = [x[c] + mm3(x[c], y[c]) for c in cs]
        y = [mm3(yy, yy) for yy in y]
    x = [x[c] + mm3(x[c], y[c]) for c in cs]
    off = [jnp.where(blk32 & jnp.logical_not(blk16), aa_, 0.0) for aa_ in a_ab]
    xo = [mm3(x[c], off[c]) for c in cs]
    x = [x[c] + mm3(xo[c], x[c]) for c in cs]
    off = [jnp.where(blk32, 0.0, aa_) for aa_ in a_ab]
    xo = [mm3(x[c], off[c]) for c in cs]
    t_inv = [x[c] + mm3(xo[c], x[c]) for c in cs]

    ap = [mm3(t_inv[c], at[c]).astype(BF16) for c in cs]
    ta = [mm3(t_inv[c], a_ak[c]) for c in cs]
    bv = [bd(x_.astype(BF16)) for x_ in v]
    u_loc = [mm1(ta[c], bv[c]).astype(BF16) for c in cs]
    y_loc = [mm1(a_rk[c], bv[c]) for c in cs]
    a_rb = [x_.astype(BF16) for x_ in a_rb]
    bw_b = [x_.astype(BF16) for x_ in bw]
    bkw = [jnp.concatenate([bw_b[c], kw[c].astype(BF16)], axis=0) for c in cs]
    p_mat = [jnp.where(same_head, _dot_tn(bw_b[c], ap[c]), 0.0).astype(BF16) for c in cs]
    g_mat = [jnp.where(same_head, _dot_tn(bkw[c], jnp.concatenate([u_loc[c], v[c].astype(BF16)], axis=0)),
                       0.0) for c in cs]
    q_mat = [(rt[c] + _dot(a_rb[c], bd(ap[c]))).astype(BF16) for c in cs]
    z_mat = [_dot(a_rb[c], bd(u_loc[c])) + y_loc[c] for c in cs]

    states = [s_ref[j] for j in range(n_pairs)]
    entry = [None] * len(cs)
    for c in range(n_chunks):
        for j in range(n_pairs):
            i = j * n_chunks + c
            s_b = states[j].astype(BF16)
            entry[i] = s_b
            states[j] = states[j] * wcol[i] + _dot(p_mat[i], s_b) + g_mat[i]
    for j in range(n_pairs):
        s_ref[j] = states[j]
        for c in range(n_chunks):
            i = j * n_chunks + c
            y_ref[c * C:(c + 1) * C, j * LANES:(j + 1) * LANES] = _dot(q_mat[i], entry[i]) + z_mat[i]

    inv_n = 1.0 / N
    y_all = y_ref[...]
    mu = _dot2(y_all, head_ones2) * inv_n
    dev = y_all - mu
    var = _dot2(dev * dev, head_ones2) * inv_n
    yn = dev * lax.rsqrt(var + RWKV_LN_EPS) * lnw_ref[...] + lnb_ref[...]
    bonus = _dot2(r_all * k_all * rk_ref[...], head_ones2) * v_all
    o_ref[...] = ((yn + bonus) * gate).astype(o_ref.dtype)


def _rwkv(p_rkv, p_lora, mix_rkv, mix_lora, w2p, a2p, g2p, w0, a0, k_k, k_a, r_k, ln_w, ln_b,
          ts=512, pairs_per_step=2):
    bsz, seq, _ = p_rkv.shape
    W = RWKV_WIDTH
    nl = p_lora.shape[-1]
    wd = pairs_per_step * LANES
    npair = W // wd
    col = lambda j: pl.BlockSpec((None, ts, wd), lambda bb, p, t, j=j: (bb, t, j * npair + p))
    vec_col = lambda j: pl.BlockSpec((1, wd), lambda bb, p, t, j=j: (0, j * npair + p))
    vec = pl.BlockSpec((1, wd), lambda bb, p, t: (0, p))
    mat = lambda rows: pl.BlockSpec((rows, wd), lambda bb, p, t: (0, p))
    kern = functools.partial(_rwkv_kernel, ts=ts)
    return pl.pallas_call(
        kern,
        grid=(bsz, npair, seq // ts),
        in_specs=[
            col(0), col(1), col(2),
            pl.BlockSpec((None, ts, nl), lambda bb, p, t: (bb, t, 0)),
            vec_col(0), vec_col(1), vec_col(2),
            pl.BlockSpec((1, nl), lambda bb, p, t: (0, 0)),
            mat(LANES), mat(LANES), mat(nl - LANES),
            vec, vec, vec, vec, vec, vec, vec,
        ],
        out_specs=pl.BlockSpec((None, ts, wd), lambda bb, p, t: (bb, t, p)),
        out_shape=jax.ShapeDtypeStruct((bsz, seq, W), BF16),
        scratch_shapes=[
            pltpu.VMEM((pairs_per_step, LANES, LANES), F32),
            pltpu.VMEM((8, wd), F32),
            pltpu.VMEM((1, nl), F32),
            pltpu.VMEM((ts, wd), F32),
        ],
        compiler_params=_params("arbitrary", "arbitrary", "arbitrary"),
        name="rwkv",
    )(p_rkv, p_rkv, p_rkv, p_lora, mix_rkv, mix_rkv, mix_rkv, mix_lora,
      w2p, a2p, g2p, w0, a0, k_k, k_a, r_k, ln_w, ln_b)


def _out_proj_kernel(oa_ref, or_ref, x_ref, w_ref, gpost_ref, gt_ref, gpre_ref, sc_ref, sh_ref,
                     wr_ref, br_ref, x1_ref, h2_ref, meta_ref, cnt_ref, run_ref):
    tm = x_ref.shape[0]
    half = oa_ref.shape[-1]

    @pl.when((pl.program_id(0) == 0) & (pl.program_id(1) == 0))
    def _():
        run_ref[...] = jnp.zeros_like(run_ref)

    y = _dot(oa_ref[...], w_ref[:half, :]) + _dot(or_ref[...], w_ref[half:, :])
    ms = jnp.mean(y * y, axis=-1, keepdims=True)
    x1 = x_ref[...] + gt_ref[...] * (y * lax.rsqrt(ms + RMS_EPS) * gpost_ref[...])
    x1_ref[...] = x1
    ms1 = jnp.mean(x1 * x1, axis=-1, keepdims=True)
    h2 = (x1 * lax.rsqrt(ms1 + RMS_EPS) * gpre_ref[...]) * (1.0 + sc_ref[...]) + sh_ref[...]
    d2 = h2.shape[-1] // 2
    lo_bits = lax.bitcast_convert_type(h2[:, :d2].astype(BF16).astype(F32), U32)
    hi_bits = lax.bitcast_convert_type(h2[:, d2:].astype(BF16).astype(F32), U32)
    h2_ref[...] = (lo_bits >> 16) | hi_bits

    lg = _dot3(h2, wr_ref[...]) + br_ref[...]
    lane_i = lax.broadcasted_iota(jnp.int32, lg.shape, 1)
    lane = lane_i.astype(F32)
    lane_group = ((lane_i - N_GROUPS) >> 3).astype(F32)
    first = lambda mask: jnp.min(jnp.where(mask, lane, 4.0 * LANES), axis=-1, keepdims=True)
    gl = jnp.where(lane_i < N_GROUPS, lg, NEG_BIG)
    gmax = jnp.max(gl, axis=-1, keepdims=True)
    g_sel = first(gl == gmax)
    p_group = 1.0 / jnp.sum(jnp.exp(gl - gmax), axis=-1, keepdims=True)
    in_group = (lane_i >= N_GROUPS) & (lane_group == g_sel)
    el = jnp.where(in_group, lg, NEG_BIG)
    m0 = jnp.max(el, axis=-1, keepdims=True)
    i0 = first(el == m0)
    el = jnp.where(lane == i0, NEG_BIG, el)
    m1 = jnp.max(el, axis=-1, keepdims=True)
    i1 = first(el == m1)
    t = jnp.exp(m1 - m0)
    gate0 = p_group / (1.0 + t)
    gate1 = p_group * t / (1.0 + t)
    e0 = i0 - N_GROUPS
    e1 = i1 - N_GROUPS
    hit = (lane == e0) | (lane == e1)
    onehot = jnp.where(hit, 1.0, 0.0).astype(BF16)
    rr = lax.broadcasted_iota(jnp.int32, (tm, tm), 0)
    cc = lax.broadcasted_iota(jnp.int32, (tm, tm), 1)
    before = jnp.where(cc < rr, 1.0, 0.0).astype(BF16)
    seen = _dot(before, onehot) + run_ref[...]
    rank0 = jnp.sum(jnp.where(lane == e0, seen, 0.0), axis=-1, keepdims=True)
    rank1 = jnp.sum(jnp.where(lane == e1, seen, 0.0), axis=-1, keepdims=True)
    total = seen[tm - 1:tm, :] + jnp.where(hit[tm - 1:tm, :], 1.0, 0.0)
    run_ref[...] = total
    cnt_ref[...] = total
    meta = jnp.zeros(lg.shape, F32)
    for idx, val in ((META_E0, e0.astype(F32)), (META_E1, e1.astype(F32)), (META_RANK0, rank0),
                     (META_RANK1, rank1), (META_GATE0, gate0), (META_GATE1, gate1)):
        meta = jnp.where(lane == idx, val, meta)
    meta_ref[...] = meta


def _out_proj(o_att, o_rwkv, x, w_out, g_post, gt, g_pre, sc, sh, w_router, b_router, tm=256):
    bsz, seq, d = x.shape
    half = o_att.shape[-1]
    const = lambda b, i: (0, 0)
    row = lambda b, i: (b, i, 0)
    per_b = lambda b, i: (b, 0, 0)
    vec = pl.BlockSpec((1, d), const)
    mod = pl.BlockSpec((None, 1, d), per_b)
    return pl.pallas_call(
        _out_proj_kernel,
        grid=(bsz, seq // tm),
        in_specs=[
            pl.BlockSpec((None, tm, half), row),
            pl.BlockSpec((None, tm, half), row),
            pl.BlockSpec((None, tm, d), row),
            pl.BlockSpec((d, d), const, pipeline_mode=pl.Buffered(1)),
            vec, mod, vec, mod, mod,
            pl.BlockSpec((d, LANES), const),
            pl.BlockSpec((1, LANES), const),
        ],
        out_specs=[
            pl.BlockSpec((None, tm, d), row),
            pl.BlockSpec((None, tm, d // 2), row),
            pl.BlockSpec((None, tm, LANES), row),
            pl.BlockSpec((1, LANES), const),
        ],
        out_shape=[
            jax.ShapeDtypeStruct((bsz, seq, d), F32),
            jax.ShapeDtypeStruct((bsz, seq, d // 2), U32),
            jax.ShapeDtypeStruct((bsz, seq, LANES), F32),
            jax.ShapeDtypeStruct((1, LANES), F32),
        ],
        scratch_shapes=[pltpu.VMEM((1, LANES), F32)],
        compiler_params=_params("arbitrary", "arbitrary"),
        name="out_proj",
    )(o_att, o_rwkv, x, w_out, g_post.reshape(1, d), gt, g_pre.reshape(1, d), sc, sh, w_router, b_router)


def _expert_kernel(te_ref, nv_ref, x_ref, wg_ref, wu_ref, wd_ref, o_ref, wg_b, wu_b, wd_b):
    i = pl.program_id(0)
    valid = i < nv_ref[0]
    prev = te_ref[jnp.maximum(i - 1, 0)]
    new_expert = (i == 0) | (te_ref[i] != prev)

    @pl.when(valid & new_expert)
    def _():
        wg_b[...] = wg_ref[...].astype(BF16)
        wu_b[...] = wu_ref[...].astype(BF16)
        wd_b[...] = wd_ref[...].astype(BF16)

    @pl.when(valid)
    def _():
        words = x_ref[...]
        d2 = words.shape[-1]
        x_lo = lax.bitcast_convert_type(words << 16, F32).astype(BF16)
        x_hi = lax.bitcast_convert_type(words & jnp.uint32(0xFFFF0000), F32).astype(BF16)
        g = _dot(x_lo, wg_b[:d2, :]) + _dot(x_hi, wg_b[d2:, :])
        u = _dot(x_lo, wu_b[:d2, :]) + _dot(x_hi, wu_b[d2:, :])
        hid = (g * jax.nn.sigmoid(g) * u).astype(BF16)
        o_ref[...] = _dot(hid, wd_b[...])

    @pl.when(jnp.logical_not(valid))
    def _():
        o_ref[...] = jnp.zeros_like(o_ref)


def _experts(xs, tile_expert, n_valid, w_gate, w_up, w_down, tm):
    m_pad, d2 = xs.shape
    d = 2 * d2
    de = w_gate.shape[-1]
    n_tiles = m_pad // tm
    grid_spec = pltpu.PrefetchScalarGridSpec(
        num_scalar_prefetch=2,
        grid=(n_tiles,),
        in_specs=[
            pl.BlockSpec((tm, d2), lambda i, te, nv: (i, 0)),
            pl.BlockSpec((None, d, de), lambda i, te, nv: (te[i], 0, 0)),
            pl.BlockSpec((None, d, de), lambda i, te, nv: (te[i], 0, 0)),
            pl.BlockSpec((None, de, d), lambda i, te, nv: (te[i], 0, 0)),
        ],
        out_specs=pl.BlockSpec((tm, d), lambda i, te, nv: (i, 0)),
        scratch_shapes=[
            pltpu.VMEM((d, de), BF16),
            pltpu.VMEM((d, de), BF16),
            pltpu.VMEM((de, d), BF16),
        ],
    )
    return pl.pallas_call(
        _expert_kernel,
        grid_spec=grid_spec,
        out_shape=jax.ShapeDtypeStruct((m_pad, d), F32),
        compiler_params=_params("arbitrary"),
        name="experts",
    )(tile_expert, n_valid, xs, w_gate, w_up, w_down)


def _final_kernel(x_ref, y0_ref, y1_ref, meta_ref, g_ref, gt_ref, o_ref):
    meta = meta_ref[...]
    y = y0_ref[...] * meta[:, META_GATE0:META_GATE0 + 1] + y1_ref[...] * meta[:, META_GATE1:META_GATE1 + 1]
    ms = jnp.mean(y * y, axis=-1, keepdims=True)
    o_ref[...] = x_ref[...] + gt_ref[...] * (y * lax.rsqrt(ms + RMS_EPS) * g_ref[...])


def _final(x1, y0, y1, meta, g, gt, tm=512):
    bsz, seq, d = x1.shape
    row = lambda b, i: (b, i, 0)
    blk = pl.BlockSpec((None, tm, d), row)
    return pl.pallas_call(
        _final_kernel,
        grid=(bsz, seq // tm),
        in_specs=[blk, blk, blk,
                  pl.BlockSpec((None, tm, LANES), row),
                  pl.BlockSpec((1, d), lambda b, i: (0, 0)),
                  pl.BlockSpec((None, 1, d), lambda b, i: (b, 0, 0))],
        out_specs=blk,
        out_shape=jax.ShapeDtypeStruct((bsz, seq, d), F32),
        compiler_params=_params("arbitrary", "arbitrary"),
        name="final",
    )(x1, y0, y1, meta, g.reshape(1, d), gt)


def _dispatch(meta, counts, tm):
    n_tok = meta.shape[0]
    n_rows = n_tok * TOP_K
    n_tiles = n_rows // tm + N_EXPERTS
    expert = meta[:, META_E0:META_E1 + 1].astype(jnp.int32)
    rank = meta[:, META_RANK0:META_RANK1 + 1].astype(jnp.int32)
    counts = counts[0, :N_EXPERTS].astype(jnp.int32)
    tiles_e = (counts + tm - 1) // tm
    tile_end = jnp.cumsum(tiles_e)
    pad_start = (tile_end - tiles_e) * tm
    slot_pos = pad_start[expert] + rank
    tok = jnp.broadcast_to(jnp.arange(n_tok, dtype=jnp.int32)[:, None], (n_tok, TOP_K))
    filler = jnp.arange(n_tiles * tm, dtype=jnp.int32) % n_tok
    src_tok = filler.at[slot_pos.reshape(-1)].set(tok.reshape(-1))
    n_valid = tile_end[-1]
    t_idx = jnp.arange(n_tiles, dtype=jnp.int32)
    tile_expert = jnp.searchsorted(tile_end, jnp.minimum(t_idx, n_valid - 1), side='right')
    tile_expert = jnp.minimum(tile_expert, N_EXPERTS - 1).astype(jnp.int32)
    return src_tok, slot_pos, tile_expert, n_valid.reshape(1).astype(jnp.int32)


def _pad_rows(w, rows_before, rows_total):
    return jnp.pad(w, ((rows_before, rows_total - rows_before - w.shape[0]), (0, 0))).astype(BF16)


def kernel(x, c, w_ada, b_ada, g_pre_mix, g_post_mix, g_pre_ffn, g_post_ffn, w_in, w_out,
           attn_out_gain, rwkv_shift_mix, rwkv_w0, rwkv_w2, rwkv_a0, rwkv_a2, rwkv_g2,
           rwkv_k_k, rwkv_k_a, rwkv_r_k, rwkv_ln_w, rwkv_ln_b, router_group_w, router_group_b,
           router_expert_w, router_expert_b, expert_w_gate, expert_w_up, expert_w_down):
    bsz, seq, d = x.shape
    depth = w_ada.shape[0]
    tm_moe = 256
    W = RWKV_WIDTH
    for l in range(depth):
        mod = _adaln(c, w_ada[l], b_ada[l])
        sh_m, sc_m, gt_m, sh_f, sc_f, gt_f = jnp.split(mod[:, None, :], N_MOD, axis=-1)

        w = w_in[l]
        n_att = 3 * ATT_WIDTH
        n_rkv = 3 * W
        n_w, n_a, n_g = rwkv_w2.shape[1], rwkv_a2.shape[1], rwkv_g2.shape[1]
        assert n_w + n_a == LANES and w.shape[1] == n_att + n_rkv + n_w + n_a + n_g
        n_lora = LANES + -(-n_g // LANES) * LANES
        pad_l = n_lora - (n_w + n_a + n_g)
        w_att = w[:, :n_att].astype(BF16)
        w_rkv = w[:, n_att:n_att + n_rkv].astype(BF16)
        w_lora = jnp.pad(w[:, n_att + n_rkv:], ((0, 0), (0, pad_l))).astype(BF16)
        qkv, p_rkv, p_lora = _in_proj(x, g_pre_mix[l], sc_m, sh_m, w_att, w_rkv, w_lora)
        o_att = _attention(qkv, attn_out_gain[l])
        mix = rwkv_shift_mix[l]
        row = lambda t: t.reshape(1, -1)
        o_rwkv = _rwkv(
            p_rkv, p_lora, row(mix[:n_rkv]), row(jnp.pad(mix[n_rkv:], (0, pad_l))),
            _pad_rows(rwkv_w2[l], 0, LANES), _pad_rows(rwkv_a2[l], n_w, LANES),
            _pad_rows(rwkv_g2[l], 0, n_lora - LANES),
            row(rwkv_w0[l]), row(rwkv_a0[l]), row(rwkv_k_k[l]), row(rwkv_k_a[l]), row(rwkv_r_k[l]),
            row(rwkv_ln_w[l]), row(rwkv_ln_b[l]))

        n_route = N_GROUPS + N_EXPERTS
        w_router = jnp.concatenate([router_group_w[l], router_expert_w[l]], axis=1)
        w_router = jnp.pad(w_router, ((0, 0), (0, LANES - n_route)))
        b_router = jnp.pad(jnp.concatenate([router_group_b[l], router_expert_b[l]]), (0, LANES - n_route))
        x1, h2w, meta, counts = _out_proj(o_att, o_rwkv, x, w_out[l].astype(BF16), g_post_mix[l], gt_m,
                                          g_pre_ffn[l], sc_f, sh_f, w_router, b_router.reshape(1, LANES))

        src_tok, slot_pos, tile_expert, n_valid = _dispatch(meta.reshape(bsz * seq, LANES), counts, tm_moe)
        xs = h2w.reshape(bsz * seq, d // 2)[src_tok]
        rows = _experts(xs, tile_expert, n_valid, expert_w_gate[l], expert_w_up[l], expert_w_down[l], tm_moe)
        y0 = rows[slot_pos[:, 0]].reshape(bsz, seq, d)
        y1 = rows[slot_pos[:, 1]].reshape(bsz, seq, d)
        x = _final(x1, y0, y1, meta, g_post_ffn[l], gt_f)
    return x
```

```python
import functools

import jax
import jax.numpy as jnp
from jax import lax
from jax.experimental import pallas as pl
from jax.experimental.pallas import tpu as pltpu

F32 = jnp.float32
BF16 = jnp.bfloat16
U32 = jnp.uint32

LANES = 128
VMEM_LIMIT_BYTES = 56 * 1024 * 1024

ATT_HEADS = 8
ATT_HEAD_DIM = 128
ATT_WIDTH = ATT_HEADS * ATT_HEAD_DIM
RWKV_HEAD_DIM = 64
RWKV_WIDTH = 1024
RWKV_CHUNK = 64
RWKV_LN_EPS = 64e-5
RMS_EPS = 1e-6
LOG2_E = 1.4426950408889634
EXP2_CLAMP = 126.0
N_GROUPS = 8
EXPERTS_PER_GROUP = 8
N_EXPERTS = 64
TOP_K = 2
N_MOD = 6
NEG_BIG = -1e30
META_E0, META_E1, META_RANK0, META_RANK1, META_GATE0, META_GATE1 = range(6)


def _dot(a, b):
    return jnp.dot(a, b, preferred_element_type=F32)


def _dot_nt(a, b):
    return lax.dot_general(a, b, (((1,), (1,)), ((), ())), preferred_element_type=F32)


def _dot_tn(a, b):
    return lax.dot_general(a, b, (((0,), (0,)), ((), ())), preferred_element_type=F32)


def _split(x):
    hi = x.astype(BF16)
    lo = (x - hi.astype(F32)).astype(BF16)
    return hi, lo


def _dot3(a, b):
    ah, al = _split(a)
    bh, bl = _split(b)
    return _dot(ah, bh) + _dot(ah, bl) + _dot(al, bh)


def _dot2(x, m2):
    hi, lo = _split(x)
    return _dot(jnp.concatenate([hi, lo], axis=1), m2)


def _params(*sem):
    return pltpu.CompilerParams(dimension_semantics=sem, vmem_limit_bytes=VMEM_LIMIT_BYTES)


def _adaln_kernel(c_ref, w_ref, b_ref, o_ref):
    c = c_ref[...]
    s = (c * jax.nn.sigmoid(c)).astype(BF16)
    o_ref[...] = _dot(s, w_ref[...].astype(BF16)) + b_ref[...]


def _adaln(c, w, b):
    bsz, d = c.shape
    n = w.shape[1]
    rows = 8
    tn = 1536
    cp = jnp.zeros((rows, d), F32).at[:bsz].set(c)
    out = pl.pallas_call(
        _adaln_kernel,
        grid=(n // tn,),
        in_specs=[
            pl.BlockSpec((rows, d), lambda j: (0, 0)),
            pl.BlockSpec((d, tn), lambda j: (0, j)),
            pl.BlockSpec((1, tn), lambda j: (0, j)),
        ],
        out_specs=pl.BlockSpec((rows, tn), lambda j: (0, j)),
        out_shape=jax.ShapeDtypeStruct((rows, n), F32),
        compiler_params=_params("arbitrary"),
        name="adaln",
    )(cp, w, b.reshape(1, n))
    return out[:bsz]


def _in_proj_kernel(x_ref, g_ref, sc_ref, sh_ref, wa_ref, wr_ref, wl_ref, qs_ref,
                    oa_ref, or_ref, ol_ref):
    x = x_ref[...]
    ms = jnp.mean(x * x, axis=-1, keepdims=True)
    y = x * lax.rsqrt(ms + RMS_EPS) * g_ref[...]
    h = (y * (1.0 + sc_ref[...]) + sh_ref[...]).astype(BF16)
    oa_ref[...] = (_dot(h, wa_ref[...]) * qs_ref[...]).astype(oa_ref.dtype)
    or_ref[...] = _dot(h, wr_ref[...])
    ol_ref[...] = _dot(h, wl_ref[...])


def _in_proj(x, g, sc, sh, w_att, w_rkv, w_lora, tm=256):
    bsz, seq, d = x.shape
    na, nr, nl = w_att.shape[1], w_rkv.shape[1], w_lora.shape[1]
    q_scale = jnp.where(jnp.arange(na) < ATT_WIDTH, LOG2_E * ATT_HEAD_DIM ** -0.5, 1.0).astype(F32).reshape(1, na)
    const = lambda b, i: (0, 0)
    row = lambda b, i: (b, i, 0)
    per_b = lambda b, i: (b, 0, 0)
    return pl.pallas_call(
        _in_proj_kernel,
        grid=(bsz, seq // tm),
        in_specs=[
            pl.BlockSpec((None, tm, d), row),
            pl.BlockSpec((1, d), const),
            pl.BlockSpec((None, 1, d), per_b),
            pl.BlockSpec((None, 1, d), per_b),
            pl.BlockSpec((d, na), const, pipeline_mode=pl.Buffered(1)),
            pl.BlockSpec((d, nr), const, pipeline_mode=pl.Buffered(1)),
            pl.BlockSpec((d, nl), const, pipeline_mode=pl.Buffered(1)),
            pl.BlockSpec((1, na), const),
        ],
        out_specs=[
            pl.BlockSpec((None, tm, na), row),
            pl.BlockSpec((None, tm, nr), row),
            pl.BlockSpec((None, tm, nl), row),
        ],
        out_shape=[
            jax.ShapeDtypeStruct((bsz, seq, na), BF16),
            jax.ShapeDtypeStruct((bsz, seq, nr), F32),
            jax.ShapeDtypeStruct((bsz, seq, nl), F32),
        ],
        compiler_params=_params("arbitrary", "arbitrary"),
        name="in_proj",
    )(x, g.reshape(1, d), sc, sh, w_att, w_rkv, w_lora, q_scale)


def _att_kernel(q_ref, k_ref, v_ref, gain_ref, o_ref, *, tq, tk):
    sub = LANES
    n_sub = tk // sub
    qi = pl.program_id(2)
    q = q_ref[...]
    r2 = lax.broadcasted_iota(jnp.int32, (2 * sub, 2 * sub), 0) & (sub - 1)
    c2 = lax.broadcasted_iota(jnp.int32, (2 * sub, 2 * sub), 1)
    tri_ones = jnp.where((c2 >= sub) | (r2 >= c2), 1.0, 0.0).astype(BF16)

    lower = (lax.broadcasted_iota(jnp.int32, (sub, sub), 1)
             < lax.broadcasted_iota(jnp.int32, (sub, sub), 0))

    def sweep(items):
        n = [it[1].shape[0] // sub for it in items]
        tiles = lambda x, m: [x[:, u * sub:(u + 1) * sub] for u in range(m)]
        z = [_dot_nt(it[0], it[1]) for it in items]
        sp = [jnp.maximum(jnp.log2(1.0 + jnp.exp2(jnp.minimum(x, EXP2_CLAMP))), x) for x in z]
        z = [tiles(x, m) for x, m in zip(z, n)]
        sp = [tiles(x, m) for x, m in zip(sp, n)]
        for i, it in enumerate(items):
            if it[5]:
                sp[i][-1] = jnp.where(lower, sp[i][-1], 0.0)
        cs = [[_dot(jnp.concatenate(_split(x), axis=1), tri_ones) for x in row] for row in sp]
        out = []
        for i, it in enumerate(items):
            carry, acc = (it[3], it[4]) if it[3] is not None else out[-1]
            w = [None] * n[i]
            for u in reversed(range(n[i])):
                w[u] = jnp.exp2(z[i][u] - (cs[i][u][:, :sub] + carry))
                carry = carry + cs[i][u][:, sub:]
            if it[5]:
                w[-1] = jnp.where(lower, w[-1], 0.0)
            acc = acc + _dot(jnp.concatenate(w, axis=1).astype(BF16), it[2])
            out.append((carry, acc))
        return out

    assert tq == tk
    start = pl.multiple_of(qi * tk, tk)
    zeros = jnp.zeros((sub, sub), F32)
    diag = sweep([(q[r * sub:(r + 1) * sub, :], k_ref[pl.ds(start, (r + 1) * sub), :],
                   v_ref[pl.ds(start, (r + 1) * sub), :], zeros, zeros, True) for r in range(n_sub)])
    carry = jnp.concatenate([c for c, _ in diag], axis=0)
    acc = jnp.concatenate([a for _, a in diag], axis=0)

    def keys(j):
        st = pl.multiple_of(j * tk, tk)
        return k_ref[pl.ds(st, tk), :], v_ref[pl.ds(st, tk), :]

    def two_tiles(i, state):
        j = qi - 1 - 2 * i
        return sweep([(q, *keys(j), state[0], state[1], False), (q, *keys(j - 1), None, None, False)])[-1]

    def one_tile(i, state):
        return sweep([(q, *keys(0), state[0], state[1], False)])[0]

    carry, acc = lax.fori_loop(0, lax.shift_right_logical(qi, 1), two_tiles, (carry, acc))
    carry, acc = lax.fori_loop(0, qi & 1, one_tile, (carry, acc))
    ms = jnp.mean(acc * acc, axis=-1, keepdims=True)
    o_ref[...] = (acc * lax.rsqrt(ms + RMS_EPS) * gain_ref[...]).astype(o_ref.dtype)


def _attention(qkv, gain, tq=512, tk=512):
    bsz, seq, _ = qkv.shape
    nh = ATT_HEADS
    assert tk % tq == 0 and seq % tk == 0
    kern = functools.partial(_att_kernel, tq=tq, tk=tk)
    return pl.pallas_call(
        kern,
        grid=(bsz, nh, seq // tq),
        in_specs=[
            pl.BlockSpec((None, tq, ATT_HEAD_DIM), lambda b, h, i: (b, i, h)),
            pl.BlockSpec((None, seq, ATT_HEAD_DIM), lambda b, h, i: (b, 0, nh + h)),
            pl.BlockSpec((None, seq, ATT_HEAD_DIM), lambda b, h, i: (b, 0, 2 * nh + h)),
            pl.BlockSpec((1, ATT_HEAD_DIM), lambda b, h, i: (0, h)),
        ],
        out_specs=pl.BlockSpec((None, tq, ATT_HEAD_DIM), lambda b, h, i: (b, i, h)),
        out_shape=jax.ShapeDtypeStruct((bsz, seq, ATT_WIDTH), BF16),
        compiler_params=_params("arbitrary", "arbitrary", "arbitrary"),
        name="attention",
    )(qkv, qkv, qkv, gain.reshape(1, ATT_WIDTH))


def _rwkv_kernel(pr_ref, pk_ref, pv_ref, pl_ref, mr_ref, mk_ref, mv_ref, ml_ref,
                 w2_ref, a2_ref, g2_ref, w0_ref, a0_ref, kk_ref, ka_ref, rk_ref, lnw_ref, lnb_ref,
                 o_ref, s_ref, prev_ref, prevl_ref, y_ref, *, ts):
    C = RWKV_CHUNK
    N = RWKV_HEAD_DIM
    n_chunks = ts // C

    @pl.when(pl.program_id(2) == 0)
    def _():
        s_ref[...] = jnp.zeros_like(s_ref)
        prev_ref[...] = jnp.zeros_like(prev_ref)
        prevl_ref[...] = jnp.zeros_like(prevl_ref)

    first_row = lax.broadcasted_iota(jnp.int32, (ts, 1), 0) == 0

    def shift(x, last_prev, mix):
        prev = jnp.where(first_row, last_prev, pltpu.roll(x, 1, 0))
        return x + (prev - x) * mix

    raw = [pr_ref[...], pk_ref[...], pv_ref[...]]
    raw_l = pl_ref[...]
    r_all = shift(raw[0], prev_ref[0:1, :], mr_ref[...])
    k_all = shift(raw[1], prev_ref[1:2, :], mk_ref[...])
    v_all = shift(raw[2], prev_ref[2:3, :], mv_ref[...])
    p_l = shift(raw_l, prevl_ref[...], ml_ref[...])
    for i in range(3):
        prev_ref[i:i + 1, :] = raw[i][ts - 1:ts, :]
    prevl_ref[...] = raw_l[ts - 1:ts, :]

    wd = o_ref.shape[-1]
    n_pairs = wd // LANES
    r128 = lax.broadcasted_iota(jnp.int32, (LANES, LANES), 0)
    c128 = lax.broadcasted_iota(jnp.int32, (LANES, LANES), 1)
    same_head = (r128 < N) == (c128 < N)
    rw = lax.broadcasted_iota(jnp.int32, (wd, wd), 0)
    cw = lax.broadcasted_iota(jnp.int32, (wd, wd), 1)
    head_ones = jnp.where((rw >> 6) == (cw >> 6), 1.0, 0.0).astype(BF16)
    head_ones2 = jnp.concatenate([head_ones, head_ones], axis=0)

    lw_pre = w0_ref[...] + _dot(jnp.tanh(p_l[:, :LANES]).astype(BF16), w2_ref[...])
    w_log = -(jnp.maximum(-lw_pre, 0.0) + jnp.log(1.0 + jnp.exp(-jnp.abs(lw_pre)))) - 0.5
    lw_all = -jnp.exp(w_log)
    rate = jax.nn.sigmoid(a0_ref[...] + _dot(p_l[:, :LANES].astype(BF16), a2_ref[...]))
    gate = _dot(jax.nn.sigmoid(p_l[:, LANES:]).astype(BF16), g2_ref[...])
    kk = k_all * kk_ref[...]
    norm = jnp.sqrt(_dot2(kk * kk, head_ones2))
    kk = kk / jnp.maximum(norm, 1e-12)
    k_all = k_all * (1.0 + (rate - 1.0) * ka_ref[...])
    a_all = -kk
    b_all = kk * rate

    row = lax.broadcasted_iota(jnp.int32, (C, LANES), 0)
    lane = lax.broadcasted_iota(jnp.int32, (C, LANES), 1)
    col = lane & (N - 1)
    head0 = lane < N
    strict = row > col
    incl = row >= col
    eye = jnp.where(row == col, 1.0, 0.0)
    blk16 = (row >> 4) == (col >> 4)
    blk32 = (row >> 5) == (col >> 5)
    tri2 = jnp.where(incl, 1.0, 0.0).astype(BF16)
    ones = jnp.ones((C, LANES), BF16)

    def bd(x):
        z = jnp.zeros_like(x)
        return jnp.concatenate([jnp.where(head0, x, z), jnp.where(head0, z, x)], axis=0)

    def mm3(x, y):
        xh, xl = _split(x)
        yh, yl = _split(y)
        byh, byl = bd(yh), bd(yl)
        rhs = jnp.concatenate([jnp.concatenate([byh, byl], axis=1),
                               jnp.concatenate([byh, jnp.zeros_like(byl)], axis=1)], axis=0)
        p = _dot(jnp.concatenate([xh, xl], axis=1), rhs)
        return p[:, :LANES] + p[:, LANES:]

    def mm1(x, y_bd):
        return _dot(x.astype(BF16), y_bd)

    cs = range(n_pairs * n_chunks)
    chunk = lambda x: [x[c * C:(c + 1) * C, j * LANES:(j + 1) * LANES]
                       for j in range(n_pairs) for c in range(n_chunks)]
    r, k, v, lw, a, b = (chunk(x) for x in (r_all, k_all, v_all, lw_all, a_all, b_all))
    lws = [_split(x) for x in lw]
    cum = [_dot(tri2, jnp.concatenate([h, l], axis=0)) for h, l in lws]
    wcol = [jnp.exp(_dot_tn(h, ones) + _dot_tn(l, ones)) for h, l in lws]
    e_neg = [jnp.exp(-x) for x in cum]
    e_rem = [jnp.exp(x[C - 1:C, :] - x) for x in cum]
    rt = [r[c] * jnp.exp(cum[c]) for c in cs]
    at = [a[c] * jnp.exp(cum[c] - lw[c]) for c in cs]
    kt = [k[c] * e_neg[c] for c in cs]
    bt = [b[c] * e_neg[c] for c in cs]
    kw = [k[c] * e_rem[c] for c in cs]
    bw = [b[c] * e_rem[c] for c in cs]

    bts = [_split(x) for x in bt]
    ats = [_split(x) for x in at]
    bbt_h = [bd(h) for h, _ in bts]
    bbt_l = [bd(l) for _, l in bts]
    rt_b = [x.astype(BF16) for x in rt]
    bkt = [bd(x.astype(BF16)) for x in kt]
    lhs = [jnp.concatenate([ats[c][0], ats[c][1], rt_b[c]], axis=0) for c in cs]
    rhs = [jnp.concatenate([bbt_h[c], bbt_l[c], bkt[c]], axis=0) for c in cs]
    aa = [_dot_nt(lhs[c], rhs[c]) for c in cs]
    L = LANES
    a_ab = [jnp.where(strict, x[:C, :L] + x[:C, L:2 * L] + x[C:2 * C, :L], 0.0) for x in aa]
    a_ak = [jnp.where(strict, x[:C, 2 * L:] + x[C:2 * C, 2 * L:], 0.0) for x in aa]
    a_rb = [jnp.where(incl, x[2 * C:, :L], 0.0) for x in aa]
    a_rk = [jnp.where(incl, x[2 * C:, 2 * L:], 0.0) for x in aa]

    d = [jnp.where(blk16, x, 0.0) for x in a_ab]
    x = [eye + dd for dd in d]
    y = [mm3(dd, dd) for dd in d]
    for _ in range(2):
        x = [x[c] + mm3(x[c], y[c]) for c in cs]
        y = [mm3(yy, yy) for yy in y]
    x = [x[c] + mm3(x[c], y[c]) for c in cs]
    off = [jnp.where(blk32 & jnp.logical_not(blk16), aa_, 0.0) for aa_ in a_ab]
    xo = [mm3(x[c], off[c]) for c in cs]
    x = [x[c] + mm3(xo[c], x[c]) for c in cs]
    off = [jnp.where(blk32, 0.0, aa_) for aa_ in a_ab]
    xo = [mm3(x[c], off[c]) for c in cs]
    t_inv = [x[c] + mm3(xo[c], x[c]) for c in cs]

    ap = [mm3(t_inv[c], at[c]).astype(BF16) for c in cs]
    ta = [mm3(t_inv[c], a_ak[c]) for c in cs]
    bv = [bd(x_.astype(BF16)) for x_ in v]
    u_loc = [mm1(ta[c], bv[c]).astype(BF16) for c in cs]
    y_loc = [mm1(a_rk[c], bv[c]) for c in cs]
    a_rb = [x_.astype(BF16) for x_ in a_rb]
    bw_b = [x_.astype(BF16) for x_ in bw]
    bkw = [jnp.concatenate([bw_b[c], kw[c].astype(BF16)], axis=0) for c in cs]
    p_mat = [jnp.where(same_head, _dot_tn(bw_b[c], ap[c]), 0.0).astype(BF16) for c in cs]
    g_mat = [jnp.where(same_head, _dot_tn(bkw[c], jnp.concatenate([u_loc[c], v[c].astype(BF16)], axis=0)),
                       0.0) for c in cs]
    q_mat = [(rt[c] + _dot(a_rb[c], bd(ap[c]))).astype(BF16) for c in cs]
    z_mat = [_dot(a_rb[c], bd(u_loc[c])) + y_loc[c] for c in cs]

    states = [s_ref[j] for j in range(n_pairs)]
    entry = [None] * len(cs)
    for c in range(n_chunks):
        for j in range(n_pairs):
            i = j * n_chunks + c
            s_b = states[j].astype(BF16)
            entry[i] = s_b
            states[j] = states[j] * wcol[i] + _dot(p_mat[i], s_b) + g_mat[i]
    for j in range(n_pairs):
        s_ref[j] = states[j]
        for c in range(n_chunks):
            i = j * n_chunks + c
            y_ref[c * C:(c + 1) * C, j * LANES:(j + 1) * LANES] = _dot(q_mat[i], entry[i]) + z_mat[i]

    inv_n = 1.0 / N
    y_all = y_ref[...]
    mu = _dot2(y_all, head_ones2) * inv_n
    dev = y_all - mu
    var = _dot2(dev * dev, head_ones2) * inv_n
    yn = dev * lax.rsqrt(var + RWKV_LN_EPS) * lnw_ref[...] + lnb_ref[...]
    bonus = _dot2(r_all * k_all * rk_ref[...], head_ones2) * v_all
    o_ref[...] = ((yn + bonus) * gate).astype(o_ref.dtype)


def _rwkv(p_rkv, p_lora, mix_rkv, mix_lora, w2p, a2p, g2p, w0, a0, k_k, k_a, r_k, ln_w, ln_b,
          ts=512, pairs_per_step=2):
    bsz, seq, _ = p_rkv.shape
    W = RWKV_WIDTH
    nl = p_lora.shape[-1]
    wd = pairs_per_step * LANES
    npair = W // wd
    col = lambda j: pl.BlockSpec((None, ts, wd), lambda bb, p, t, j=j: (bb, t, j * npair + p))
    vec_col = lambda j: pl.BlockSpec((1, wd), lambda bb, p, t, j=j: (0, j * npair + p))
    vec = pl.BlockSpec((1, wd), lambda bb, p, t: (0, p))
    mat = lambda rows: pl.BlockSpec((rows, wd), lambda bb, p, t: (0, p))
    kern = functools.partial(_rwkv_kernel, ts=ts)
    return pl.pallas_call(
        kern,
        grid=(bsz, npair, seq // ts),
        in_specs=[
            col(0), col(1), col(2),
            pl.BlockSpec((None, ts, nl), lambda bb, p, t: (bb, t, 0)),
            vec_col(0), vec_col(1), vec_col(2),
            pl.BlockSpec((1, nl), lambda bb, p, t: (0, 0)),
            mat(LANES), mat(LANES), mat(nl - LANES),
            vec, vec, vec, vec, vec, vec, vec,
        ],
        out_specs=pl.BlockSpec((None, ts, wd), lambda bb, p, t: (bb, t, p)),
        out_shape=jax.ShapeDtypeStruct((bsz, seq, W), BF16),
        scratch_shapes=[
            pltpu.VMEM((pairs_per_step, LANES, LANES), F32),
            pltpu.VMEM((8, wd), F32),
            pltpu.VMEM((1, nl), F32),
            pltpu.VMEM((ts, wd), F32),
        ],
        compiler_params=_params("arbitrary", "arbitrary", "arbitrary"),
        name="rwkv",
    )(p_rkv, p_rkv, p_rkv, p_lora, mix_rkv, mix_rkv, mix_rkv, mix_lora,
      w2p, a2p, g2p, w0, a0, k_k, k_a, r_k, ln_w, ln_b)


def _out_proj_kernel(oa_ref, or_ref, x_ref, w_ref, gpost_ref, gt_ref, gpre_ref, sc_ref, sh_ref,
                     wr_ref, br_ref, x1_ref, h2_ref, meta_ref, cnt_ref, run_ref):
    tm = x_ref.shape[0]
    half = oa_ref.shape[-1]

    @pl.when((pl.program_id(0) == 0) & (pl.program_id(1) == 0))
    def _():
        run_ref[...] = jnp.zeros_like(run_ref)

    y = _dot(oa_ref[...], w_ref[:half, :]) + _dot(or_ref[...], w_ref[half:, :])
    ms = jnp.mean(y * y, axis=-1, keepdims=True)
    x1 = x_ref[...] + gt_ref[...] * (y * lax.rsqrt(ms + RMS_EPS) * gpost_ref[...])
    x1_ref[...] = x1
    ms1 = jnp.mean(x1 * x1, axis=-1, keepdims=True)
    h2 = (x1 * lax.rsqrt(ms1 + RMS_EPS) * gpre_ref[...]) * (1.0 + sc_ref[...]) + sh_ref[...]
    h2_ref[...] = _pack_halves(h2)

    lg = _dot3(h2, wr_ref[...]) + br_ref[...]
    lane_i = lax.broadcasted_iota(jnp.int32, lg.shape, 1)
    lane = lane_i.astype(F32)
    lane_group = ((lane_i - N_GROUPS) >> 3).astype(F32)
    first = lambda mask: jnp.min(jnp.where(mask, lane, 4.0 * LANES), axis=-1, keepdims=True)
    gl = jnp.where(lane_i < N_GROUPS, lg, NEG_BIG)
    gmax = jnp.max(gl, axis=-1, keepdims=True)
    g_sel = first(gl == gmax)
    p_group = 1.0 / jnp.sum(jnp.exp(gl - gmax), axis=-1, keepdims=True)
    in_group = (lane_i >= N_GROUPS) & (lane_group == g_sel)
    el = jnp.where(in_group, lg, NEG_BIG)
    m0 = jnp.max(el, axis=-1, keepdims=True)
    i0 = first(el == m0)
    el = jnp.where(lane == i0, NEG_BIG, el)
    m1 = jnp.max(el, axis=-1, keepdims=True)
    i1 = first(el == m1)
    t = jnp.exp(m1 - m0)
    gate0 = p_group / (1.0 + t)
    gate1 = p_group * t / (1.0 + t)
    e0 = i0 - N_GROUPS
    e1 = i1 - N_GROUPS
    hit = (lane == e0) | (lane == e1)
    onehot = jnp.where(hit, 1.0, 0.0).astype(BF16)
    rr = lax.broadcasted_iota(jnp.int32, (tm, tm), 0)
    cc = lax.broadcasted_iota(jnp.int32, (tm, tm), 1)
    before = jnp.where(cc < rr, 1.0, 0.0).astype(BF16)
    seen = _dot(before, onehot) + run_ref[...]
    rank0 = jnp.sum(jnp.where(lane == e0, seen, 0.0), axis=-1, keepdims=True)
    rank1 = jnp.sum(jnp.where(lane == e1, seen, 0.0), axis=-1, keepdims=True)
    total = seen[tm - 1:tm, :] + jnp.where(hit[tm - 1:tm, :], 1.0, 0.0)
    run_ref[...] = total
    cnt_ref[...] = total
    meta = jnp.zeros(lg.shape, F32)
    for idx, val in ((META_E0, e0.astype(F32)), (META_E1, e1.astype(F32)), (META_RANK0, rank0),
                     (META_RANK1, rank1), (META_GATE0, gate0), (META_GATE1, gate1)):
        meta = jnp.where(lane == idx, val, meta)
    meta_ref[...] = meta


def _out_proj(o_att, o_rwkv, x, w_out, g_post, gt, g_pre, sc, sh, w_router, b_router, tm=512):
    bsz, seq, d = x.shape
    half = o_att.shape[-1]
    const = lambda b, i: (0, 0)
    row = lambda b, i: (b, i, 0)
    per_b = lambda b, i: (b, 0, 0)
    vec = pl.BlockSpec((1, d), const)
    mod = pl.BlockSpec((None, 1, d), per_b)
    return pl.pallas_call(
        _out_proj_kernel,
        grid=(bsz, seq // tm),
        in_specs=[
            pl.BlockSpec((None, tm, half), row),
            pl.BlockSpec((None, tm, half), row),
            pl.BlockSpec((None, tm, d), row),
            pl.BlockSpec((d, d), const, pipeline_mode=pl.Buffered(1)),
            vec, mod, vec, mod, mod,
            pl.BlockSpec((d, LANES), const),
            pl.BlockSpec((1, LANES), const),
        ],
        out_specs=[
            pl.BlockSpec((None, tm, d), row),
            pl.BlockSpec((None, tm, d // 2), row),
            pl.BlockSpec((None, tm, LANES), row),
            pl.BlockSpec((1, LANES), const),
        ],
        out_shape=[
            jax.ShapeDtypeStruct((bsz, seq, d), F32),
            jax.ShapeDtypeStruct((bsz, seq, d // 2), U32),
            jax.ShapeDtypeStruct((bsz, seq, LANES), F32),
            jax.ShapeDtypeStruct((1, LANES), F32),
        ],
        scratch_shapes=[pltpu.VMEM((1, LANES), F32)],
        compiler_params=_params("arbitrary", "arbitrary"),
        name="out_proj",
    )(o_att, o_rwkv, x, w_out, g_post.reshape(1, d), gt, g_pre.reshape(1, d), sc, sh, w_router, b_router)


def _pack_halves(x):
    d2 = x.shape[-1] // 2
    lo_bits = lax.bitcast_convert_type(x[:, :d2].astype(BF16).astype(F32), U32)
    hi_bits = lax.bitcast_convert_type(x[:, d2:].astype(BF16).astype(F32), U32)
    return (lo_bits >> 16) | hi_bits


def _unpack_halves(words):
    lo = lax.bitcast_convert_type(words << 16, F32)
    hi = lax.bitcast_convert_type(words & jnp.uint32(0xFFFF0000), F32)
    return lo, hi


def _expert_kernel(te_ref, nv_ref, nx_ref, sl_ref, x_ref, wg_hbm, wu_hbm, wd_hbm, o_ref,
                   wg_f, wu_f, wd_f, wg_b, wu_b, wd_b, sem):
    i = pl.program_id(0)
    valid = i < nv_ref[0]
    prev = te_ref[jnp.maximum(i - 1, 0)]
    new_expert = (i == 0) | (te_ref[i] != prev)

    def weight_copies(e, slot):
        return (pltpu.make_async_copy(wg_hbm.at[e], wg_f.at[slot], sem.at[slot, 0]),
                pltpu.make_async_copy(wu_hbm.at[e], wu_f.at[slot], sem.at[slot, 1]),
                pltpu.make_async_copy(wd_hbm.at[e], wd_f.at[slot], sem.at[slot, 2]))

    @pl.when(valid & (i == 0))
    def _():
        for cp in weight_copies(te_ref[0], sl_ref[0]):
            cp.start()

    @pl.when(valid & new_expert)
    def _():
        slot = sl_ref[i]
        for cp in weight_copies(te_ref[i], slot):
            cp.wait()

        @pl.when(nx_ref[i] >= 0)
        def _():
            for cp in weight_copies(nx_ref[i], 1 - slot):
                cp.start()

        wg_b[...] = wg_f[slot].astype(BF16)
        wu_b[...] = wu_f[slot].astype(BF16)
        wd_b[...] = wd_f[slot].astype(BF16)

    @pl.when(valid)
    def _():
        x_lo, x_hi = _unpack_halves(x_ref[...])
        d2 = x_lo.shape[-1]
        x_lo, x_hi = x_lo.astype(BF16), x_hi.astype(BF16)
        g = _dot(x_lo, wg_b[:d2, :]) + _dot(x_hi, wg_b[d2:, :])
        u = _dot(x_lo, wu_b[:d2, :]) + _dot(x_hi, wu_b[d2:, :])
        hid = (g * jax.nn.sigmoid(g) * u).astype(BF16)
        o_ref[...] = _pack_halves(_dot(hid, wd_b[...]))

    @pl.when(jnp.logical_not(valid))
    def _():
        o_ref[...] = jnp.zeros_like(o_ref)


def _experts(xs, tile_expert, n_valid, next_expert, slot, w_gate, w_up, w_down, tm):
    m_pad, d2 = xs.shape
    d = 2 * d2
    de = w_gate.shape[-1]
    n_tiles = m_pad // tm
    grid_spec = pltpu.PrefetchScalarGridSpec(
        num_scalar_prefetch=4,
        grid=(n_tiles,),
        in_specs=[
            pl.BlockSpec((tm, d2), lambda i, *_: (i, 0)),
            pl.BlockSpec(memory_space=pl.ANY),
            pl.BlockSpec(memory_space=pl.ANY),
            pl.BlockSpec(memory_space=pl.ANY),
        ],
        out_specs=pl.BlockSpec((tm, d2), lambda i, *_: (i, 0)),
        scratch_shapes=[
            pltpu.VMEM((2, d, de), F32),
            pltpu.VMEM((2, d, de), F32),
            pltpu.VMEM((2, de, d), F32),
            pltpu.VMEM((d, de), BF16),
            pltpu.VMEM((d, de), BF16),
            pltpu.VMEM((de, d), BF16),
            pltpu.SemaphoreType.DMA((2, 3)),
        ],
    )
    return pl.pallas_call(
        _expert_kernel,
        grid_spec=grid_spec,
        out_shape=jax.ShapeDtypeStruct((m_pad, d2), U32),
        compiler_params=_params("arbitrary"),
        name="experts",
    )(tile_expert, n_valid, next_expert, slot, xs, w_gate, w_up, w_down)


def _final_kernel(x_ref, y0_ref, y1_ref, meta_ref, g_ref, gt_ref, o_ref):
    meta = meta_ref[...]
    gate0 = meta[:, META_GATE0:META_GATE0 + 1]
    gate1 = meta[:, META_GATE1:META_GATE1 + 1]
    d2 = y0_ref.shape[-1]
    halves = [a * gate0 + b * gate1 for a, b in zip(_unpack_halves(y0_ref[...]), _unpack_halves(y1_ref[...]))]
    ssq = sum(jnp.sum(h * h, axis=-1, keepdims=True) for h in halves)
    inv = lax.rsqrt(ssq * (1.0 / (2 * d2)) + RMS_EPS)
    for j, h in enumerate(halves):
        cols = slice(j * d2, (j + 1) * d2)
        o_ref[:, cols] = x_ref[:, cols] + gt_ref[:, cols] * (h * inv * g_ref[:, cols])


def _final(x1, y0, y1, meta, g, gt, tm=512):
    bsz, seq, d = x1.shape
    row = lambda b, i: (b, i, 0)
    blk = pl.BlockSpec((None, tm, d), row)
    packed = pl.BlockSpec((None, tm, d // 2), row)
    return pl.pallas_call(
        _final_kernel,
        grid=(bsz, seq // tm),
        in_specs=[blk, packed, packed,
                  pl.BlockSpec((None, tm, LANES), row),
                  pl.BlockSpec((1, d), lambda b, i: (0, 0)),
                  pl.BlockSpec((None, 1, d), lambda b, i: (b, 0, 0))],
        out_specs=blk,
        out_shape=jax.ShapeDtypeStruct((bsz, seq, d), F32),
        compiler_params=_params("arbitrary", "arbitrary"),
        name="final",
    )(x1, y0, y1, meta, g.reshape(1, d), gt)


def _dispatch(meta, counts, tm):
    n_tok = meta.shape[0]
    n_rows = n_tok * TOP_K
    n_tiles = n_rows // tm + N_EXPERTS
    expert = meta[:, META_E0:META_E1 + 1].astype(jnp.int32)
    rank = meta[:, META_RANK0:META_RANK1 + 1].astype(jnp.int32)
    counts = counts[0, :N_EXPERTS].astype(jnp.int32)
    tiles_e = (counts + tm - 1) // tm
    tile_end = jnp.cumsum(tiles_e)
    pad_start = (tile_end - tiles_e) * tm
    slot_pos = pad_start[expert] + rank
    tok = jnp.broadcast_to(jnp.arange(n_tok, dtype=jnp.int32)[:, None], (n_tok, TOP_K))
    filler = jnp.arange(n_tiles * tm, dtype=jnp.int32) % n_tok
    src_tok = filler.at[slot_pos.reshape(-1)].set(tok.reshape(-1))
    n_valid = tile_end[-1]
    t_idx = jnp.arange(n_tiles, dtype=jnp.int32)
    tile_expert = jnp.searchsorted(tile_end, jnp.minimum(t_idx, n_valid - 1), side='right')
    tile_expert = jnp.minimum(tile_expert, N_EXPERTS - 1).astype(jnp.int32)
    ids = jnp.arange(N_EXPERTS, dtype=jnp.int32)
    used = tiles_e > 0
    first_used_from = lax.cummin(jnp.where(used, ids, N_EXPERTS), reverse=True)
    next_used = jnp.concatenate([first_used_from[1:], jnp.full((1,), N_EXPERTS, jnp.int32)])
    next_used = jnp.where(next_used >= N_EXPERTS, -1, next_used)
    run_parity = (jnp.cumsum(used.astype(jnp.int32)) - 1) & 1
    return (src_tok, slot_pos, tile_expert, n_valid.reshape(1).astype(jnp.int32),
            next_used[tile_expert].astype(jnp.int32), run_parity[tile_expert].astype(jnp.int32))


def _pad_rows(w, rows_before, rows_total):
    return jnp.pad(w, ((rows_before, rows_total - rows_before - w.shape[0]), (0, 0))).astype(BF16)


def kernel(x, c, w_ada, b_ada, g_pre_mix, g_post_mix, g_pre_ffn, g_post_ffn, w_in, w_out,
           attn_out_gain, rwkv_shift_mix, rwkv_w0, rwkv_w2, rwkv_a0, rwkv_a2, rwkv_g2,
           rwkv_k_k, rwkv_k_a, rwkv_r_k, rwkv_ln_w, rwkv_ln_b, router_group_w, router_group_b,
           router_expert_w, router_expert_b, expert_w_gate, expert_w_up, expert_w_down):
    bsz, seq, d = x.shape
    depth = w_ada.shape[0]
    tm_moe = 256
    W = RWKV_WIDTH
    for l in range(depth):
        mod = _adaln(c, w_ada[l], b_ada[l])
        sh_m, sc_m, gt_m, sh_f, sc_f, gt_f = jnp.split(mod[:, None, :], N_MOD, axis=-1)

        w = w_in[l]
        n_att = 3 * ATT_WIDTH
        n_rkv = 3 * W
        n_w, n_a, n_g = rwkv_w2.shape[1], rwkv_a2.shape[1], rwkv_g2.shape[1]
        assert n_w + n_a == LANES and w.shape[1] == n_att + n_rkv + n_w + n_a + n_g
        n_lora = LANES + -(-n_g // LANES) * LANES
        pad_l = n_lora - (n_w + n_a + n_g)
        w_att = w[:, :n_att].astype(BF16)
        w_rkv = w[:, n_att:n_att + n_rkv].astype(BF16)
        w_lora = jnp.pad(w[:, n_att + n_rkv:], ((0, 0), (0, pad_l))).astype(BF16)
        qkv, p_rkv, p_lora = _in_proj(x, g_pre_mix[l], sc_m, sh_m, w_att, w_rkv, w_lora)
        o_att = _attention(qkv, attn_out_gain[l])
        mix = rwkv_shift_mix[l]
        row = lambda t: t.reshape(1, -1)
        o_rwkv = _rwkv(
            p_rkv, p_lora, row(mix[:n_rkv]), row(jnp.pad(mix[n_rkv:], (0, pad_l))),
            _pad_rows(rwkv_w2[l], 0, LANES), _pad_rows(rwkv_a2[l], n_w, LANES),
            _pad_rows(rwkv_g2[l], 0, n_lora - LANES),
            row(rwkv_w0[l]), row(rwkv_a0[l]), row(rwkv_k_k[l]), row(rwkv_k_a[l]), row(rwkv_r_k[l]),
            row(rwkv_ln_w[l]), row(rwkv_ln_b[l]))

        n_route = N_GROUPS + N_EXPERTS
        w_router = jnp.concatenate([router_group_w[l], router_expert_w[l]], axis=1)
        w_router = jnp.pad(w_router, ((0, 0), (0, LANES - n_route)))
        b_router = jnp.pad(jnp.concatenate([router_group_b[l], router_expert_b[l]]), (0, LANES - n_route))
        x1, h2w, meta, counts = _out_proj(o_att, o_rwkv, x, w_out[l].astype(BF16), g_post_mix[l], gt_m,
                                          g_pre_ffn[l], sc_f, sh_f, w_router, b_router.reshape(1, LANES))

        src_tok, slot_pos, tile_expert, n_valid, next_expert, slot = _dispatch(
            meta.reshape(bsz * seq, LANES), counts, tm_moe)
        xs = h2w.reshape(bsz * seq, d // 2)[src_tok]
        rows = _experts(xs, tile_expert, n_valid, next_expert, slot,
                        expert_w_gate[l], expert_w_up[l], expert_w_down[l], tm_moe)
        y0 = rows[slot_pos[:, 0]].reshape(bsz, seq, d // 2)
        y1 = rows[slot_pos[:, 1]].reshape(bsz, seq, d // 2)
        x = _final(x1, y0, y1, meta, g_post_ffn[l], gt_f)
    return x
```

```python
import functools

import jax
import jax.numpy as jnp
from jax import lax
from jax.experimental import pallas as pl
from jax.experimental.pallas import tpu as pltpu

F32 = jnp.float32
BF16 = jnp.bfloat16
U32 = jnp.uint32

LANES = 128
VMEM_LIMIT_BYTES = 56 * 1024 * 1024

ATT_HEADS = 8
ATT_HEAD_DIM = 128
ATT_WIDTH = ATT_HEADS * ATT_HEAD_DIM
RWKV_HEAD_DIM = 64
RWKV_WIDTH = 1024
RWKV_CHUNK = 64
RWKV_LN_EPS = 64e-5
RMS_EPS = 1e-6
LOG2_E = 1.4426950408889634
EXP2_CLAMP = 126.0
N_GROUPS = 8
EXPERTS_PER_GROUP = 8
N_EXPERTS = 64
TOP_K = 2
N_MOD = 6
NEG_BIG = -1e30
META_E0, META_E1, META_RANK0, META_RANK1, META_GATE0, META_GATE1 = range(6)


def _dot(a, b):
    return jnp.dot(a, b, preferred_element_type=F32)


def _dot_nt(a, b):
    return lax.dot_general(a, b, (((1,), (1,)), ((), ())), preferred_element_type=F32)


def _dot_tn(a, b):
    return lax.dot_general(a, b, (((0,), (0,)), ((), ())), preferred_element_type=F32)


def _split(x):
    hi = x.astype(BF16)
    lo = (x - hi.astype(F32)).astype(BF16)
    return hi, lo


def _dot3(a, b):
    ah, al = _split(a)
    bh, bl = _split(b)
    return _dot(ah, bh) + _dot(ah, bl) + _dot(al, bh)


def _dot2(x, m2):
    hi, lo = _split(x)
    return _dot(jnp.concatenate([hi, lo], axis=1), m2)


def _params(*sem):
    return pltpu.CompilerParams(dimension_semantics=sem, vmem_limit_bytes=VMEM_LIMIT_BYTES)


def _adaln_kernel(c_ref, w_ref, b_ref, o_ref):
    c = c_ref[...]
    s = (c * jax.nn.sigmoid(c)).astype(BF16)
    o_ref[...] = _dot(s, w_ref[...].astype(BF16)) + b_ref[...]


def _adaln(c, w, b):
    bsz, d = c.shape
    n = w.shape[1]
    rows = 8
    tn = 1536
    cp = jnp.zeros((rows, d), F32).at[:bsz].set(c)
    out = pl.pallas_call(
        _adaln_kernel,
        grid=(n // tn,),
        in_specs=[
            pl.BlockSpec((rows, d), lambda j: (0, 0)),
            pl.BlockSpec((d, tn), lambda j: (0, j)),
            pl.BlockSpec((1, tn), lambda j: (0, j)),
        ],
        out_specs=pl.BlockSpec((rows, tn), lambda j: (0, j)),
        out_shape=jax.ShapeDtypeStruct((rows, n), F32),
        compiler_params=_params("arbitrary"),
        name="adaln",
    )(cp, w, b.reshape(1, n))
    return out[:bsz]


def _in_proj_kernel(x_ref, g_ref, sc_ref, sh_ref, wa_ref, wr_ref, wl_ref, qs_ref,
                    oa_ref, or_ref, ol_ref):
    x = x_ref[...]
    ms = jnp.mean(x * x, axis=-1, keepdims=True)
    y = x * lax.rsqrt(ms + RMS_EPS) * g_ref[...]
    h = (y * (1.0 + sc_ref[...]) + sh_ref[...]).astype(BF16)
    oa_ref[...] = (_dot(h, wa_ref[...]) * qs_ref[...]).astype(oa_ref.dtype)
    or_ref[...] = _dot(h, wr_ref[...])
    ol_ref[...] = _dot(h, wl_ref[...])


def _in_proj(x, g, sc, sh, w_att, w_rkv, w_lora, tm=256):
    bsz, seq, d = x.shape
    na, nr, nl = w_att.shape[1], w_rkv.shape[1], w_lora.shape[1]
    q_scale = jnp.where(jnp.arange(na) < ATT_WIDTH, LOG2_E * ATT_HEAD_DIM ** -0.5, 1.0).astype(F32).reshape(1, na)
    const = lambda b, i: (0, 0)
    row = lambda b, i: (b, i, 0)
    per_b = lambda b, i: (b, 0, 0)
    return pl.pallas_call(
        _in_proj_kernel,
        grid=(bsz, seq // tm),
        in_specs=[
            pl.BlockSpec((None, tm, d), row),
            pl.BlockSpec((1, d), const),
            pl.BlockSpec((None, 1, d), per_b),
            pl.BlockSpec((None, 1, d), per_b),
            pl.BlockSpec((d, na), const, pipeline_mode=pl.Buffered(1)),
            pl.BlockSpec((d, nr), const, pipeline_mode=pl.Buffered(1)),
            pl.BlockSpec((d, nl), const, pipeline_mode=pl.Buffered(1)),
            pl.BlockSpec((1, na), const),
        ],
        out_specs=[
            pl.BlockSpec((None, tm, na), row),
            pl.BlockSpec((None, tm, nr), row),
            pl.BlockSpec((None, tm, nl), row),
        ],
        out_shape=[
            jax.ShapeDtypeStruct((bsz, seq, na), BF16),
            jax.ShapeDtypeStruct((bsz, seq, nr), F32),
            jax.ShapeDtypeStruct((bsz, seq, nl), F32),
        ],
        compiler_params=_params("arbitrary", "arbitrary"),
        name="in_proj",
    )(x, g.reshape(1, d), sc, sh, w_att, w_rkv, w_lora, q_scale)


def _att_kernel(q_ref, k_ref, v_ref, gain_ref, o_ref, *, tq, tk):
    sub = LANES
    n_sub = tk // sub
    qi = pl.program_id(2)
    q = q_ref[...]
    r2 = lax.broadcasted_iota(jnp.int32, (2 * sub, 2 * sub), 0) & (sub - 1)
    c2 = lax.broadcasted_iota(jnp.int32, (2 * sub, 2 * sub), 1)
    tri_ones = jnp.where((c2 >= sub) | (r2 >= c2), 1.0, 0.0).astype(BF16)

    lower = (lax.broadcasted_iota(jnp.int32, (sub, sub), 1)
             < lax.broadcasted_iota(jnp.int32, (sub, sub), 0))

    def sweep(items):
        n = [it[1].shape[0] // sub for it in items]
        tiles = lambda x, m: [x[:, u * sub:(u + 1) * sub] for u in range(m)]
        z = [_dot_nt(it[0], it[1]) for it in items]
        sp = [jnp.maximum(jnp.log2(1.0 + jnp.exp2(jnp.minimum(x, EXP2_CLAMP))), x) for x in z]
        z = [tiles(x, m) for x, m in zip(z, n)]
        sp = [tiles(x, m) for x, m in zip(sp, n)]
        for i, it in enumerate(items):
            if it[5]:
                sp[i][-1] = jnp.where(lower, sp[i][-1], 0.0)
        cs = [[_dot(jnp.concatenate(_split(x), axis=1), tri_ones) for x in row] for row in sp]
        out = []
        for i, it in enumerate(items):
            carry, acc = (it[3], it[4]) if it[3] is not None else out[-1]
            w = [None] * n[i]
            for u in reversed(range(n[i])):
                w[u] = jnp.exp2(z[i][u] - (cs[i][u][:, :sub] + carry))
                carry = carry + cs[i][u][:, sub:]
            if it[5]:
                w[-1] = jnp.where(lower, w[-1], 0.0)
            acc = acc + _dot(jnp.concatenate(w, axis=1).astype(BF16), it[2])
            out.append((carry, acc))
        return out

    assert tq == tk
    start = pl.multiple_of(qi * tk, tk)
    zeros = jnp.zeros((sub, sub), F32)
    diag = sweep([(q[r * sub:(r + 1) * sub, :], k_ref[pl.ds(start, (r + 1) * sub), :],
                   v_ref[pl.ds(start, (r + 1) * sub), :], zeros, zeros, True) for r in range(n_sub)])
    carry = jnp.concatenate([c for c, _ in diag], axis=0)
    acc = jnp.concatenate([a for _, a in diag], axis=0)

    def keys(j):
        st = pl.multiple_of(j * tk, tk)
        return k_ref[pl.ds(st, tk), :], v_ref[pl.ds(st, tk), :]

    def two_tiles(i, state):
        j = qi - 1 - 2 * i
        return sweep([(q, *keys(j), state[0], state[1], False), (q, *keys(j - 1), None, None, False)])[-1]

    def one_tile(i, state):
        return sweep([(q, *keys(0), state[0], state[1], False)])[0]

    carry, acc = lax.fori_loop(0, lax.shift_right_logical(qi, 1), two_tiles, (carry, acc))
    carry, acc = lax.fori_loop(0, qi & 1, one_tile, (carry, acc))
    ms = jnp.mean(acc * acc, axis=-1, keepdims=True)
    o_ref[...] = (acc * lax.rsqrt(ms + RMS_EPS) * gain_ref[...]).astype(o_ref.dtype)


def _attention(qkv, gain, tq=512, tk=512):
    bsz, seq, _ = qkv.shape
    nh = ATT_HEADS
    assert tk % tq == 0 and seq % tk == 0
    kern = functools.partial(_att_kernel, tq=tq, tk=tk)
    return pl.pallas_call(
        kern,
        grid=(bsz, nh, seq // tq),
        in_specs=[
            pl.BlockSpec((None, tq, ATT_HEAD_DIM), lambda b, h, i: (b, i, h)),
            pl.BlockSpec((None, seq, ATT_HEAD_DIM), lambda b, h, i: (b, 0, nh + h)),
            pl.BlockSpec((None, seq, ATT_HEAD_DIM), lambda b, h, i: (b, 0, 2 * nh + h)),
            pl.BlockSpec((1, ATT_HEAD_DIM), lambda b, h, i: (0, h)),
        ],
        out_specs=pl.BlockSpec((None, tq, ATT_HEAD_DIM), lambda b, h, i: (b, i, h)),
        out_shape=jax.ShapeDtypeStruct((bsz, seq, ATT_WIDTH), BF16),
        compiler_params=_params("arbitrary", "arbitrary", "arbitrary"),
        name="attention",
    )(qkv, qkv, qkv, gain.reshape(1, ATT_WIDTH))


def _rwkv_kernel(pr_ref, pk_ref, pv_ref, pl_ref, mr_ref, mk_ref, mv_ref, ml_ref,
                 w2_ref, a2_ref, g2_ref, w0_ref, a0_ref, kk_ref, ka_ref, rk_ref, lnw_ref, lnb_ref,
                 o_ref, s_ref, prev_ref, prevl_ref, y_ref, *, ts):
    C = RWKV_CHUNK
    N = RWKV_HEAD_DIM
    n_chunks = ts // C

    @pl.when(pl.program_id(2) == 0)
    def _():
        s_ref[...] = jnp.zeros_like(s_ref)
        prev_ref[...] = jnp.zeros_like(prev_ref)
        prevl_ref[...] = jnp.zeros_like(prevl_ref)

    first_row = lax.broadcasted_iota(jnp.int32, (ts, 1), 0) == 0

    def shift(x, last_prev, mix):
        prev = jnp.where(first_row, last_prev, pltpu.roll(x, 1, 0))
        return x + (prev - x) * mix

    raw = [pr_ref[...], pk_ref[...], pv_ref[...]]
    raw_l = pl_ref[...]
    r_all = shift(raw[0], prev_ref[0:1, :], mr_ref[...])
    k_all = shift(raw[1], prev_ref[1:2, :], mk_ref[...])
    v_all = shift(raw[2], prev_ref[2:3, :], mv_ref[...])
    p_l = shift(raw_l, prevl_ref[...], ml_ref[...])
    for i in range(3):
        prev_ref[i:i + 1, :] = raw[i][ts - 1:ts, :]
    prevl_ref[...] = raw_l[ts - 1:ts, :]

    wd = o_ref.shape[-1]
    n_pairs = wd // LANES
    r128 = lax.broadcasted_iota(jnp.int32, (LANES, LANES), 0)
    c128 = lax.broadcasted_iota(jnp.int32, (LANES, LANES), 1)
    same_head = (r128 < N) == (c128 < N)
    rw = lax.broadcasted_iota(jnp.int32, (wd, wd), 0)
    cw = lax.broadcasted_iota(jnp.int32, (wd, wd), 1)
    head_ones = jnp.where((rw >> 6) == (cw >> 6), 1.0, 0.0).astype(BF16)
    head_ones2 = jnp.concatenate([head_ones, head_ones], axis=0)

    lw_pre = w0_ref[...] + _dot(jnp.tanh(p_l[:, :LANES]).astype(BF16), w2_ref[...])
    w_log = -(jnp.maximum(-lw_pre, 0.0) + jnp.log(1.0 + jnp.exp(-jnp.abs(lw_pre)))) - 0.5
    lw_all = -jnp.exp(w_log)
    rate = jax.nn.sigmoid(a0_ref[...] + _dot(p_l[:, :LANES].astype(BF16), a2_ref[...]))
    gate = _dot(jax.nn.sigmoid(p_l[:, LANES:]).astype(BF16), g2_ref[...])
    kk = k_all * kk_ref[...]
    norm = jnp.sqrt(_dot2(kk * kk, head_ones2))
    kk = kk / jnp.maximum(norm, 1e-12)
    k_all = k_all * (1.0 + (rate - 1.0) * ka_ref[...])
    a_all = -kk
    b_all = kk * rate

    row = lax.broadcasted_iota(jnp.int32, (C, LANES), 0)
    lane = lax.broadcasted_iota(jnp.int32, (C, LANES), 1)
    col = lane & (N - 1)
    head0 = lane < N
    strict = row > col
    incl = row >= col
    eye = jnp.where(row == col, 1.0, 0.0)
    blk16 = (row >> 4) == (col >> 4)
    blk32 = (row >> 5) == (col >> 5)
    tri2 = jnp.where(incl, 1.0, 0.0).astype(BF16)
    ones = jnp.ones((C, LANES), BF16)

    def bd(x):
        z = jnp.zeros_like(x)
        return jnp.concatenate([jnp.where(head0, x, z), jnp.where(head0, z, x)], axis=0)

    def mm3(x, y):
        xh, xl = _split(x)
        yh, yl = _split(y)
        byh, byl = bd(yh), bd(yl)
        rhs = jnp.concatenate([jnp.concatenate([byh, byl], axis=1),
                               jnp.concatenate([byh, jnp.zeros_like(byl)], axis=1)], axis=0)
        p = _dot(jnp.concatenate([xh, xl], axis=1), rhs)
        return p[:, :LANES] + p[:, LANES:]

    def mm1(x, y_bd):
        return _dot(x.astype(BF16), y_bd)

    cs = range(n_pairs * n_chunks)
    chunk = lambda x: [x[c * C:(c + 1) * C, j * LANES:(j + 1) * LANES]
                       for j in range(n_pairs) for c in range(n_chunks)]
    r, k, v, lw, a, b = (chunk(x) for x in (r_all, k_all, v_all, lw_all, a_all, b_all))
    lws = [_split(x) for x in lw]
    cum = [_dot(tri2, jnp.concatenate([h, l], axis=0)) for h, l in lws]
    wcol = [jnp.exp(_dot_tn(h, ones) + _dot_tn(l, ones)) for h, l in lws]
    e_neg = [jnp.exp(-x) for x in cum]
    e_rem = [jnp.exp(x[C - 1:C, :] - x) for x in cum]
    rt = [r[c] * jnp.exp(cum[c]) for c in cs]
    at = [a[c] * jnp.exp(cum[c] - lw[c]) for c in cs]
    kt = [k[c] * e_neg[c] for c in cs]
    bt = [b[c] * e_neg[c] for c in cs]
    kw = [k[c] * e_rem[c] for c in cs]
    bw = [b[c] * e_rem[c] for c in cs]

    at_b = [x.astype(BF16) for x in at]
    rt_b = [x.astype(BF16) for x in rt]
    lhs = [jnp.concatenate([at_b[c], rt_b[c]], axis=0) for c in cs]
    rhs = [jnp.concatenate([bd(bt[c].astype(BF16)), bd(kt[c].astype(BF16))], axis=0) for c in cs]
    aa = [_dot_nt(lhs[c], rhs[c]) for c in cs]
    L = LANES
    a_ab = [jnp.where(strict, x[:C, :L], 0.0) for x in aa]
    a_ak = [jnp.where(strict, x[:C, L:], 0.0) for x in aa]
    a_rb = [jnp.where(incl, x[C:, :L], 0.0) for x in aa]
    a_rk = [jnp.where(incl, x[C:, L:], 0.0) for x in aa]

    d = [jnp.where(blk16, x, 0.0) for x in a_ab]
    x = [eye + dd for dd in d]
    y = [mm3(dd, dd) for dd in d]
    for _ in range(2):
        x = [x[c] + mm3(x[c], y[c]) for c in cs]
        y = [mm3(yy, yy) for yy in y]
    x = [x[c] + mm3(x[c], y[c]) for c in cs]
    for in_block in (blk32, None):
        if in_block is blk32:
            off = [jnp.where(blk32 & jnp.logical_not(blk16), aa_, 0.0) for aa_ in a_ab]
        else:
            off = [jnp.where(blk32, 0.0, aa_) for aa_ in a_ab]
        xo = [mm1(x[c], bd(off[c].astype(BF16))) for c in cs]
        x = [x[c] + mm1(xo[c], bd(x[c].astype(BF16))) for c in cs]
    t_inv = x

    rhs = [jnp.concatenate([bd(at_b[c]), bd(a_ak[c].astype(BF16))], axis=1) for c in cs]
    tt = [mm1(t_inv[c], rhs[c]) for c in cs]
    ap = [x_[:, :L].astype(BF16) for x_ in tt]
    ta = [x_[:, L:] for x_ in tt]
    bv = [bd(x_.astype(BF16)) for x_ in v]
    u_loc = [mm1(ta[c], bv[c]).astype(BF16) for c in cs]
    y_loc = [mm1(a_rk[c], bv[c]) for c in cs]
    a_rb = [x_.astype(BF16) for x_ in a_rb]
    bw_b = [x_.astype(BF16) for x_ in bw]
    bkw = [jnp.concatenate([bw_b[c], kw[c].astype(BF16)], axis=0) for c in cs]
    p_mat = [jnp.where(same_head, _dot_tn(bw_b[c], ap[c]), 0.0).astype(BF16) for c in cs]
    g_mat = [jnp.where(same_head, _dot_tn(bkw[c], jnp.concatenate([u_loc[c], v[c].astype(BF16)], axis=0)),
                       0.0) for c in cs]
    q_mat = [(rt[c] + _dot(a_rb[c], bd(ap[c]))).astype(BF16) for c in cs]
    z_mat = [_dot(a_rb[c], bd(u_loc[c])) + y_loc[c] for c in cs]

    states = [s_ref[j] for j in range(n_pairs)]
    entry = [None] * len(cs)
    for c in range(n_chunks):
        for j in range(n_pairs):
            i = j * n_chunks + c
            s_b = states[j].astype(BF16)
            entry[i] = s_b
            states[j] = states[j] * wcol[i] + _dot(p_mat[i], s_b) + g_mat[i]
    for j in range(n_pairs):
        s_ref[j] = states[j]
        for c in range(n_chunks):
            i = j * n_chunks + c
            y_ref[c * C:(c + 1) * C, j * LANES:(j + 1) * LANES] = _dot(q_mat[i], entry[i]) + z_mat[i]

    inv_n = 1.0 / N
    y_all = y_ref[...]
    mu = _dot2(y_all, head_ones2) * inv_n
    dev = y_all - mu
    var = _dot2(dev * dev, head_ones2) * inv_n
    yn = dev * lax.rsqrt(var + RWKV_LN_EPS) * lnw_ref[...] + lnb_ref[...]
    bonus = _dot2(r_all * k_all * rk_ref[...], head_ones2) * v_all
    o_ref[...] = ((yn + bonus) * gate).astype(o_ref.dtype)


def _rwkv(p_rkv, p_lora, mix_rkv, mix_lora, w2p, a2p, g2p, w0, a0, k_k, k_a, r_k, ln_w, ln_b,
          ts=512, pairs_per_step=2):
    bsz, seq, _ = p_rkv.shape
    W = RWKV_WIDTH
    nl = p_lora.shape[-1]
    wd = pairs_per_step * LANES
    npair = W // wd
    col = lambda j: pl.BlockSpec((None, ts, wd), lambda bb, p, t, j=j: (bb, t, j * npair + p))
    vec_col = lambda j: pl.BlockSpec((1, wd), lambda bb, p, t, j=j: (0, j * npair + p))
    vec = pl.BlockSpec((1, wd), lambda bb, p, t: (0, p))
    mat = lambda rows: pl.BlockSpec((rows, wd), lambda bb, p, t: (0, p))
    kern = functools.partial(_rwkv_kernel, ts=ts)
    return pl.pallas_call(
        kern,
        grid=(bsz, npair, seq // ts),
        in_specs=[
            col(0), col(1), col(2),
            pl.BlockSpec((None, ts, nl), lambda bb, p, t: (bb, t, 0)),
            vec_col(0), vec_col(1), vec_col(2),
            pl.BlockSpec((1, nl), lambda bb, p, t: (0, 0)),
            mat(LANES), mat(LANES), mat(nl - LANES),
            vec, vec, vec, vec, vec, vec, vec,
        ],
        out_specs=pl.BlockSpec((None, ts, wd), lambda bb, p, t: (bb, t, p)),
        out_shape=jax.ShapeDtypeStruct((bsz, seq, W), BF16),
        scratch_shapes=[
            pltpu.VMEM((pairs_per_step, LANES, LANES), F32),
            pltpu.VMEM((8, wd), F32),
            pltpu.VMEM((1, nl), F32),
            pltpu.VMEM((ts, wd), F32),
        ],
        compiler_params=_params("arbitrary", "arbitrary", "arbitrary"),
        name="rwkv",
    )(p_rkv, p_rkv, p_rkv, p_lora, mix_rkv, mix_rkv, mix_rkv, mix_lora,
      w2p, a2p, g2p, w0, a0, k_k, k_a, r_k, ln_w, ln_b)


def _out_proj_kernel(oa_ref, or_ref, x_ref, w_ref, gpost_ref, gt_ref, gpre_ref, sc_ref, sh_ref,
                     wr_ref, br_ref, x1_ref, h2_ref, meta_ref, cnt_ref, run_ref):
    tm = x_ref.shape[0]
    half = oa_ref.shape[-1]

    @pl.when((pl.program_id(0) == 0) & (pl.program_id(1) == 0))
    def _():
        run_ref[...] = jnp.zeros_like(run_ref)

    y = _dot(oa_ref[...], w_ref[:half, :]) + _dot(or_ref[...], w_ref[half:, :])
    ms = jnp.mean(y * y, axis=-1, keepdims=True)
    x1 = x_ref[...] + gt_ref[...] * (y * lax.rsqrt(ms + RMS_EPS) * gpost_ref[...])
    x1_ref[...] = x1
    ms1 = jnp.mean(x1 * x1, axis=-1, keepdims=True)
    h2 = (x1 * lax.rsqrt(ms1 + RMS_EPS) * gpre_ref[...]) * (1.0 + sc_ref[...]) + sh_ref[...]
    h2_ref[...] = _pack_halves(h2)

    lg = _dot3(h2, wr_ref[...]) + br_ref[...]
    lane_i = lax.broadcasted_iota(jnp.int32, lg.shape, 1)
    lane = lane_i.astype(F32)
    lane_group = ((lane_i - N_GROUPS) >> 3).astype(F32)
    first = lambda mask: jnp.min(jnp.where(mask, lane, 4.0 * LANES), axis=-1, keepdims=True)
    gl = jnp.where(lane_i < N_GROUPS, lg, NEG_BIG)
    gmax = jnp.max(gl, axis=-1, keepdims=True)
    g_sel = first(gl == gmax)
    p_group = 1.0 / jnp.sum(jnp.exp(gl - gmax), axis=-1, keepdims=True)
    in_group = (lane_i >= N_GROUPS) & (lane_group == g_sel)
    el = jnp.where(in_group, lg, NEG_BIG)
    m0 = jnp.max(el, axis=-1, keepdims=True)
    i0 = first(el == m0)
    el = jnp.where(lane == i0, NEG_BIG, el)
    m1 = jnp.max(el, axis=-1, keepdims=True)
    i1 = first(el == m1)
    t = jnp.exp(m1 - m0)
    gate0 = p_group / (1.0 + t)
    gate1 = p_group * t / (1.0 + t)
    e0 = i0 - N_GROUPS
    e1 = i1 - N_GROUPS
    hit = (lane == e0) | (lane == e1)
    onehot = jnp.where(hit, 1.0, 0.0).astype(BF16)
    rr = lax.broadcasted_iota(jnp.int32, (tm, tm), 0)
    cc = lax.broadcasted_iota(jnp.int32, (tm, tm), 1)
    before = jnp.where(cc < rr, 1.0, 0.0).astype(BF16)
    seen = _dot(before, onehot) + run_ref[...]
    rank0 = jnp.sum(jnp.where(lane == e0, seen, 0.0), axis=-1, keepdims=True)
    rank1 = jnp.sum(jnp.where(lane == e1, seen, 0.0), axis=-1, keepdims=True)
    total = seen[tm - 1:tm, :] + jnp.where(hit[tm - 1:tm, :], 1.0, 0.0)
    run_ref[...] = total
    cnt_ref[...] = total
    meta = jnp.zeros(lg.shape, F32)
    for idx, val in ((META_E0, e0.astype(F32)), (META_E1, e1.astype(F32)), (META_RANK0, rank0),
                     (META_RANK1, rank1), (META_GATE0, gate0), (META_GATE1, gate1)):
        meta = jnp.where(lane == idx, val, meta)
    meta_ref[...] = meta


def _out_proj(o_att, o_rwkv, x, w_out, g_post, gt, g_pre, sc, sh, w_router, b_router, tm=512):
    bsz, seq, d = x.shape
    half = o_att.shape[-1]
    const = lambda b, i: (0, 0)
    row = lambda b, i: (b, i, 0)
    per_b = lambda b, i: (b, 0, 0)
    vec = pl.BlockSpec((1, d), const)
    mod = pl.BlockSpec((None, 1, d), per_b)
    return pl.pallas_call(
        _out_proj_kernel,
        grid=(bsz, seq // tm),
        in_specs=[
            pl.BlockSpec((None, tm, half), row),
            pl.BlockSpec((None, tm, half), row),
            pl.BlockSpec((None, tm, d), row),
            pl.BlockSpec((d, d), const, pipeline_mode=pl.Buffered(1)),
            vec, mod, vec, mod, mod,
            pl.BlockSpec((d, LANES), const),
            pl.BlockSpec((1, LANES), const),
        ],
        out_specs=[
            pl.BlockSpec((None, tm, d), row),
            pl.BlockSpec((None, tm, d // 2), row),
            pl.BlockSpec((None, tm, LANES), row),
            pl.BlockSpec((1, LANES), const),
        ],
        out_shape=[
            jax.ShapeDtypeStruct((bsz, seq, d), F32),
            jax.ShapeDtypeStruct((bsz, seq, d // 2), U32),
            jax.ShapeDtypeStruct((bsz, seq, LANES), F32),
            jax.ShapeDtypeStruct((1, LANES), F32),
        ],
        scratch_shapes=[pltpu.VMEM((1, LANES), F32)],
        compiler_params=_params("arbitrary", "arbitrary"),
        name="out_proj",
    )(o_att, o_rwkv, x, w_out, g_post.reshape(1, d), gt, g_pre.reshape(1, d), sc, sh, w_router, b_router)


def _pack_halves(x):
    d2 = x.shape[-1] // 2
    lo_bits = lax.bitcast_convert_type(x[:, :d2].astype(BF16).astype(F32), U32)
    hi_bits = lax.bitcast_convert_type(x[:, d2:].astype(BF16).astype(F32), U32)
    return (lo_bits >> 16) | hi_bits


def _unpack_halves(words):
    lo = lax.bitcast_convert_type(words << 16, F32)
    hi = lax.bitcast_convert_type(words & jnp.uint32(0xFFFF0000), F32)
    return lo, hi


def _expert_kernel(te_ref, nv_ref, nx_ref, sl_ref, x_ref, wg_hbm, wu_hbm, wd_hbm, o_ref,
                   wg_f, wu_f, wd_f, wg_b, wu_b, wd_b, sem):
    i = pl.program_id(0)
    valid = i < nv_ref[0]
    prev = te_ref[jnp.maximum(i - 1, 0)]
    new_expert = (i == 0) | (te_ref[i] != prev)

    def weight_copies(e, slot):
        return (pltpu.make_async_copy(wg_hbm.at[e], wg_f.at[slot], sem.at[slot, 0]),
                pltpu.make_async_copy(wu_hbm.at[e], wu_f.at[slot], sem.at[slot, 1]),
                pltpu.make_async_copy(wd_hbm.at[e], wd_f.at[slot], sem.at[slot, 2]))

    @pl.when(valid & (i == 0))
    def _():
        for cp in weight_copies(te_ref[0], sl_ref[0]):
            cp.start()

    @pl.when(valid & new_expert)
    def _():
        slot = sl_ref[i]
        for cp in weight_copies(te_ref[i], slot):
            cp.wait()

        @pl.when(nx_ref[i] >= 0)
        def _():
            for cp in weight_copies(nx_ref[i], 1 - slot):
                cp.start()

        wg_b[...] = wg_f[slot].astype(BF16)
        wu_b[...] = wu_f[slot].astype(BF16)
        wd_b[...] = wd_f[slot].astype(BF16)

    @pl.when(valid)
    def _():
        x_lo, x_hi = _unpack_halves(x_ref[...])
        d2 = x_lo.shape[-1]
        x_lo, x_hi = x_lo.astype(BF16), x_hi.astype(BF16)
        g = _dot(x_lo, wg_b[:d2, :]) + _dot(x_hi, wg_b[d2:, :])
        u = _dot(x_lo, wu_b[:d2, :]) + _dot(x_hi, wu_b[d2:, :])
        hid = (g * jax.nn.sigmoid(g) * u).astype(BF16)
        o_ref[...] = _pack_halves(_dot(hid, wd_b[...]))

    @pl.when(jnp.logical_not(valid))
    def _():
        o_ref[...] = jnp.zeros_like(o_ref)


def _experts(xs, tile_expert, n_valid, next_expert, slot, w_gate, w_up, w_down, tm):
    m_pad, d2 = xs.shape
    d = 2 * d2
    de = w_gate.shape[-1]
    n_tiles = m_pad // tm
    grid_spec = pltpu.PrefetchScalarGridSpec(
        num_scalar_prefetch=4,
        grid=(n_tiles,),
        in_specs=[
            pl.BlockSpec((tm, d2), lambda i, *_: (i, 0)),
            pl.BlockSpec(memory_space=pl.ANY),
            pl.BlockSpec(memory_space=pl.ANY),
            pl.BlockSpec(memory_space=pl.ANY),
        ],
        out_specs=pl.BlockSpec((tm, d2), lambda i, *_: (i, 0)),
        scratch_shapes=[
            pltpu.VMEM((2, d, de), F32),
            pltpu.VMEM((2, d, de), F32),
            pltpu.VMEM((2, de, d), F32),
            pltpu.VMEM((d, de), BF16),
            pltpu.VMEM((d, de), BF16),
            pltpu.VMEM((de, d), BF16),
            pltpu.SemaphoreType.DMA((2, 3)),
        ],
    )
    return pl.pallas_call(
        _expert_kernel,
        grid_spec=grid_spec,
        out_shape=jax.ShapeDtypeStruct((m_pad, d2), U32),
        compiler_params=_params("arbitrary"),
        name="experts",
    )(tile_expert, n_valid, next_expert, slot, xs, w_gate, w_up, w_down)


def _final_kernel(x_ref, y_ref, meta_ref, g_ref, gt_ref, o_ref):
    meta = meta_ref[...]
    gate0 = meta[:, META_GATE0:META_GATE0 + 1]
    gate1 = meta[:, META_GATE1:META_GATE1 + 1]
    d2 = y_ref.shape[-1] // TOP_K
    halves = [a * gate0 + b * gate1
              for a, b in zip(_unpack_halves(y_ref[:, :d2]), _unpack_halves(y_ref[:, d2:]))]
    ssq = sum(jnp.sum(h * h, axis=-1, keepdims=True) for h in halves)
    inv = lax.rsqrt(ssq * (1.0 / (2 * d2)) + RMS_EPS)
    for j, h in enumerate(halves):
        cols = slice(j * d2, (j + 1) * d2)
        o_ref[:, cols] = x_ref[:, cols] + gt_ref[:, cols] * (h * inv * g_ref[:, cols])


def _final(x1, y, meta, g, gt, tm=512):
    bsz, seq, d = x1.shape
    row = lambda b, i: (b, i, 0)
    blk = pl.BlockSpec((None, tm, d), row)
    return pl.pallas_call(
        _final_kernel,
        grid=(bsz, seq // tm),
        in_specs=[blk, pl.BlockSpec((None, tm, y.shape[-1]), row),
                  pl.BlockSpec((None, tm, LANES), row),
                  pl.BlockSpec((1, d), lambda b, i: (0, 0)),
                  pl.BlockSpec((None, 1, d), lambda b, i: (b, 0, 0))],
        out_specs=blk,
        out_shape=jax.ShapeDtypeStruct((bsz, seq, d), F32),
        compiler_params=_params("arbitrary", "arbitrary"),
        name="final",
    )(x1, y, meta, g.reshape(1, d), gt)


def _dispatch(meta, counts, tm):
    n_tok = meta.shape[0]
    n_rows = n_tok * TOP_K
    n_tiles = n_rows // tm + N_EXPERTS
    expert = meta[:, META_E0:META_E1 + 1].astype(jnp.int32)
    rank = meta[:, META_RANK0:META_RANK1 + 1].astype(jnp.int32)
    counts = counts[0, :N_EXPERTS].astype(jnp.int32)
    tiles_e = (counts + tm - 1) // tm
    tile_end = jnp.cumsum(tiles_e)
    pad_start = (tile_end - tiles_e) * tm
    slot_pos = pad_start[expert] + rank
    n_valid = tile_end[-1]
    t_idx = jnp.arange(n_tiles, dtype=jnp.int32)
    tile_expert = jnp.searchsorted(tile_end, jnp.minimum(t_idx, n_valid - 1), side='right')
    tile_expert = jnp.minimum(tile_expert, N_EXPERTS - 1).astype(jnp.int32)
    order = jnp.argsort(expert.reshape(-1), stable=True).astype(jnp.int32)
    row_start = jnp.cumsum(counts) - counts
    pos = jnp.arange(n_tiles * tm, dtype=jnp.int32)
    e_pos = jnp.repeat(tile_expert, tm)
    r_pos = pos - pad_start[e_pos]
    real = (r_pos < counts[e_pos]) & (pos < n_valid * tm)
    src_tok = jnp.where(real, order[jnp.clip(row_start[e_pos] + r_pos, 0, n_rows - 1)] // TOP_K, pos % n_tok)
    ids = jnp.arange(N_EXPERTS, dtype=jnp.int32)
    used = tiles_e > 0
    first_used_from = lax.cummin(jnp.where(used, ids, N_EXPERTS), reverse=True)
    next_used = jnp.concatenate([first_used_from[1:], jnp.full((1,), N_EXPERTS, jnp.int32)])
    next_used = jnp.where(next_used >= N_EXPERTS, -1, next_used)
    run_parity = (jnp.cumsum(used.astype(jnp.int32)) - 1) & 1
    return (src_tok, slot_pos, tile_expert, n_valid.reshape(1).astype(jnp.int32),
            next_used[tile_expert].astype(jnp.int32), run_parity[tile_expert].astype(jnp.int32))


def _pad_rows(w, rows_before, rows_total):
    return jnp.pad(w, ((rows_before, rows_total - rows_before - w.shape[0]), (0, 0))).astype(BF16)


def kernel(x, c, w_ada, b_ada, g_pre_mix, g_post_mix, g_pre_ffn, g_post_ffn, w_in, w_out,
           attn_out_gain, rwkv_shift_mix, rwkv_w0, rwkv_w2, rwkv_a0, rwkv_a2, rwkv_g2,
           rwkv_k_k, rwkv_k_a, rwkv_r_k, rwkv_ln_w, rwkv_ln_b, router_group_w, router_group_b,
           router_expert_w, router_expert_b, expert_w_gate, expert_w_up, expert_w_down):
    bsz, seq, d = x.shape
    depth = w_ada.shape[0]
    tm_moe = 256
    W = RWKV_WIDTH
    for l in range(depth):
        mod = _adaln(c, w_ada[l], b_ada[l])
        sh_m, sc_m, gt_m, sh_f, sc_f, gt_f = jnp.split(mod[:, None, :], N_MOD, axis=-1)

        w = w_in[l]
        n_att = 3 * ATT_WIDTH
        n_rkv = 3 * W
        n_w, n_a, n_g = rwkv_w2.shape[1], rwkv_a2.shape[1], rwkv_g2.shape[1]
        assert n_w + n_a == LANES and w.shape[1] == n_att + n_rkv + n_w + n_a + n_g
        n_lora = LANES + -(-n_g // LANES) * LANES
        pad_l = n_lora - (n_w + n_a + n_g)
        w_att = w[:, :n_att].astype(BF16)
        w_rkv = w[:, n_att:n_att + n_rkv].astype(BF16)
        w_lora = jnp.pad(w[:, n_att + n_rkv:], ((0, 0), (0, pad_l))).astype(BF16)
        qkv, p_rkv, p_lora = _in_proj(x, g_pre_mix[l], sc_m, sh_m, w_att, w_rkv, w_lora)
        o_att = _attention(qkv, attn_out_gain[l])
        mix = rwkv_shift_mix[l]
        row = lambda t: t.reshape(1, -1)
        o_rwkv = _rwkv(
            p_rkv, p_lora, row(mix[:n_rkv]), row(jnp.pad(mix[n_rkv:], (0, pad_l))),
            _pad_rows(rwkv_w2[l], 0, LANES), _pad_rows(rwkv_a2[l], n_w, LANES),
            _pad_rows(rwkv_g2[l], 0, n_lora - LANES),
            row(rwkv_w0[l]), row(rwkv_a0[l]), row(rwkv_k_k[l]), row(rwkv_k_a[l]), row(rwkv_r_k[l]),
            row(rwkv_ln_w[l]), row(rwkv_ln_b[l]))

        n_route = N_GROUPS + N_EXPERTS
        w_router = jnp.concatenate([router_group_w[l], router_expert_w[l]], axis=1)
        w_router = jnp.pad(w_router, ((0, 0), (0, LANES - n_route)))
        b_router = jnp.pad(jnp.concatenate([router_group_b[l], router_expert_b[l]]), (0, LANES - n_route))
        x1, h2w, meta, counts = _out_proj(o_att, o_rwkv, x, w_out[l].astype(BF16), g_post_mix[l], gt_m,
                                          g_pre_ffn[l], sc_f, sh_f, w_router, b_router.reshape(1, LANES))

        src_tok, slot_pos, tile_expert, n_valid, next_expert, slot = _dispatch(
            meta.reshape(bsz * seq, LANES), counts, tm_moe)
        xs = h2w.reshape(bsz * seq, d // 2)[src_tok]
        rows = _experts(xs, tile_expert, n_valid, next_expert, slot,
                        expert_w_gate[l], expert_w_up[l], expert_w_down[l], tm_moe)
        y = rows[slot_pos].reshape(bsz, seq, TOP_K * (d // 2))
        x = _final(x1, y, meta, g_post_ffn[l], gt_f)
    return x
```

```python
import functools

import jax
import jax.numpy as jnp
from jax import lax
from jax.experimental import pallas as pl
from jax.experimental.pallas import tpu as pltpu

F32 = jnp.float32
BF16 = jnp.bfloat16
U32 = jnp.uint32

LANES = 128
VMEM_LIMIT_BYTES = 56 * 1024 * 1024

ATT_HEADS = 8
ATT_HEAD_DIM = 128
ATT_WIDTH = ATT_HEADS * ATT_HEAD_DIM
RWKV_HEAD_DIM = 64
RWKV_WIDTH = 1024
RWKV_CHUNK = 64
RWKV_LN_EPS = 64e-5
RMS_EPS = 1e-6
LOG2_E = 1.4426950408889634
EXP2_CLAMP = 126.0
N_GROUPS = 8
EXPERTS_PER_GROUP = 8
N_EXPERTS = 64
TOP_K = 2
N_MOD = 6
NEG_BIG = -1e30
WEIGHT_SLOTS = 3
META_E0, META_E1, META_RANK0, META_RANK1, META_GATE0, META_GATE1 = range(6)


def _dot(a, b):
    return jnp.dot(a, b, preferred_element_type=F32)


def _dot_nt(a, b):
    return lax.dot_general(a, b, (((1,), (1,)), ((), ())), preferred_element_type=F32)


def _dot_tn(a, b):
    return lax.dot_general(a, b, (((0,), (0,)), ((), ())), preferred_element_type=F32)


def _split(x):
    hi = x.astype(BF16)
    lo = (x - hi.astype(F32)).astype(BF16)
    return hi, lo


def _dot3(a, b):
    ah, al = _split(a)
    bh, bl = _split(b)
    return _dot(ah, bh) + _dot(ah, bl) + _dot(al, bh)


def _dot2(x, m2):
    hi, lo = _split(x)
    return _dot(jnp.concatenate([hi, lo], axis=1), m2)


def _params(*sem):
    return pltpu.CompilerParams(dimension_semantics=sem, vmem_limit_bytes=VMEM_LIMIT_BYTES)


def _adaln_kernel(c_ref, w_ref, b_ref, o_ref):
    c = c_ref[...]
    s = (c * jax.nn.sigmoid(c)).astype(BF16)
    o_ref[...] = _dot(s, w_ref[...].astype(BF16)) + b_ref[...]


def _adaln(c, w, b):
    bsz, d = c.shape
    n = w.shape[1]
    rows = 8
    tn = 1536
    cp = jnp.zeros((rows, d), F32).at[:bsz].set(c)
    out = pl.pallas_call(
        _adaln_kernel,
        grid=(n // tn,),
        in_specs=[
            pl.BlockSpec((rows, d), lambda j: (0, 0)),
            pl.BlockSpec((d, tn), lambda j: (0, j)),
            pl.BlockSpec((1, tn), lambda j: (0, j)),
        ],
        out_specs=pl.BlockSpec((rows, tn), lambda j: (0, j)),
        out_shape=jax.ShapeDtypeStruct((rows, n), F32),
        compiler_params=_params("arbitrary"),
        name="adaln",
    )(cp, w, b.reshape(1, n))
    return out[:bsz]


def _in_proj_kernel(x_ref, g_ref, sc_ref, sh_ref, wa_ref, wr_ref, wl_ref, qs_ref,
                    oa_ref, or_ref, ol_ref):
    x = x_ref[...]
    ms = jnp.mean(x * x, axis=-1, keepdims=True)
    y = x * lax.rsqrt(ms + RMS_EPS) * g_ref[...]
    h = (y * (1.0 + sc_ref[...]) + sh_ref[...]).astype(BF16)
    oa_ref[...] = (_dot(h, wa_ref[...]) * qs_ref[...]).astype(oa_ref.dtype)
    or_ref[...] = _dot(h, wr_ref[...])
    ol_ref[...] = _dot(h, wl_ref[...])


def _in_proj(x, g, sc, sh, w_att, w_rkv, w_lora, tm=256):
    bsz, seq, d = x.shape
    na, nr, nl = w_att.shape[1], w_rkv.shape[1], w_lora.shape[1]
    q_scale = jnp.where(jnp.arange(na) < ATT_WIDTH, LOG2_E * ATT_HEAD_DIM ** -0.5, 1.0).astype(F32).reshape(1, na)
    const = lambda b, i: (0, 0)
    row = lambda b, i: (b, i, 0)
    per_b = lambda b, i: (b, 0, 0)
    return pl.pallas_call(
        _in_proj_kernel,
        grid=(bsz, seq // tm),
        in_specs=[
            pl.BlockSpec((None, tm, d), row),
            pl.BlockSpec((1, d), const),
            pl.BlockSpec((None, 1, d), per_b),
            pl.BlockSpec((None, 1, d), per_b),
            pl.BlockSpec((d, na), const, pipeline_mode=pl.Buffered(1)),
            pl.BlockSpec((d, nr), const, pipeline_mode=pl.Buffered(1)),
            pl.BlockSpec((d, nl), const, pipeline_mode=pl.Buffered(1)),
            pl.BlockSpec((1, na), const),
        ],
        out_specs=[
            pl.BlockSpec((None, tm, na), row),
            pl.BlockSpec((None, tm, nr), row),
            pl.BlockSpec((None, tm, nl), row),
        ],
        out_shape=[
            jax.ShapeDtypeStruct((bsz, seq, na), BF16),
            jax.ShapeDtypeStruct((bsz, seq, nr), F32),
            jax.ShapeDtypeStruct((bsz, seq, nl), F32),
        ],
        compiler_params=_params("arbitrary", "arbitrary"),
        name="in_proj",
    )(x, g.reshape(1, d), sc, sh, w_att, w_rkv, w_lora, q_scale)


def _att_kernel(q_ref, k_ref, v_ref, gain_ref, o_ref, *, tq, tk):
    sub = LANES
    n_sub = tk // sub
    qi = pl.program_id(2)
    q = q_ref[...]
    r2 = lax.broadcasted_iota(jnp.int32, (2 * sub, 2 * sub), 0) & (sub - 1)
    c2 = lax.broadcasted_iota(jnp.int32, (2 * sub, 2 * sub), 1)
    tri_ones = jnp.where((c2 >= sub) | (r2 >= c2), 1.0, 0.0).astype(BF16)

    lower = (lax.broadcasted_iota(jnp.int32, (sub, sub), 1)
             < lax.broadcasted_iota(jnp.int32, (sub, sub), 0))

    def sweep(items):
        n = [it[1].shape[0] // sub for it in items]
        tiles = lambda x, m: [x[:, u * sub:(u + 1) * sub] for u in range(m)]
        z = [_dot_nt(it[0], it[1]) for it in items]
        sp = [jnp.maximum(jnp.log2(1.0 + jnp.exp2(jnp.minimum(x, EXP2_CLAMP))), x) for x in z]
        z = [tiles(x, m) for x, m in zip(z, n)]
        sp = [tiles(x, m) for x, m in zip(sp, n)]
        for i, it in enumerate(items):
            if it[5]:
                sp[i][-1] = jnp.where(lower, sp[i][-1], 0.0)
        cs = [[_dot(jnp.concatenate(_split(x), axis=1), tri_ones) for x in row] for row in sp]
        out = []
        for i, it in enumerate(items):
            carry, acc = (it[3], it[4]) if it[3] is not None else out[-1]
            w = [None] * n[i]
            for u in reversed(range(n[i])):
                w[u] = jnp.exp2(z[i][u] - (cs[i][u][:, :sub] + carry))
                carry = carry + cs[i][u][:, sub:]
            if it[5]:
                w[-1] = jnp.where(lower, w[-1], 0.0)
            acc = acc + _dot(jnp.concatenate(w, axis=1).astype(BF16), it[2])
            out.append((carry, acc))
        return out

    assert tq == tk
    start = pl.multiple_of(qi * tk, tk)
    zeros = jnp.zeros((sub, sub), F32)
    diag = sweep([(q[r * sub:(r + 1) * sub, :], k_ref[pl.ds(start, (r + 1) * sub), :],
                   v_ref[pl.ds(start, (r + 1) * sub), :], zeros, zeros, True) for r in range(n_sub)])
    carry = jnp.concatenate([c for c, _ in diag], axis=0)
    acc = jnp.concatenate([a for _, a in diag], axis=0)

    def keys(j):
        st = pl.multiple_of(j * tk, tk)
        return k_ref[pl.ds(st, tk), :], v_ref[pl.ds(st, tk), :]

    def two_tiles(i, state):
        j = qi - 1 - 2 * i
        return sweep([(q, *keys(j), state[0], state[1], False), (q, *keys(j - 1), None, None, False)])[-1]

    def one_tile(i, state):
        return sweep([(q, *keys(0), state[0], state[1], False)])[0]

    carry, acc = lax.fori_loop(0, lax.shift_right_logical(qi, 1), two_tiles, (carry, acc))
    carry, acc = lax.fori_loop(0, qi & 1, one_tile, (carry, acc))
    ms = jnp.mean(acc * acc, axis=-1, keepdims=True)
    o_ref[...] = (acc * lax.rsqrt(ms + RMS_EPS) * gain_ref[...]).astype(o_ref.dtype)


def _attention(qkv, gain, tq=512, tk=512):
    bsz, seq, _ = qkv.shape
    nh = ATT_HEADS
    assert tk % tq == 0 and seq % tk == 0
    kern = functools.partial(_att_kernel, tq=tq, tk=tk)
    return pl.pallas_call(
        kern,
        grid=(bsz, nh, seq // tq),
        in_specs=[
            pl.BlockSpec((None, tq, ATT_HEAD_DIM), lambda b, h, i: (b, i, h)),
            pl.BlockSpec((None, seq, ATT_HEAD_DIM), lambda b, h, i: (b, 0, nh + h)),
            pl.BlockSpec((None, seq, ATT_HEAD_DIM), lambda b, h, i: (b, 0, 2 * nh + h)),
            pl.BlockSpec((1, ATT_HEAD_DIM), lambda b, h, i: (0, h)),
        ],
        out_specs=pl.BlockSpec((None, tq, ATT_HEAD_DIM), lambda b, h, i: (b, i, h)),
        out_shape=jax.ShapeDtypeStruct((bsz, seq, ATT_WIDTH), BF16),
        compiler_params=_params("arbitrary", "arbitrary", "arbitrary"),
        name="attention",
    )(qkv, qkv, qkv, gain.reshape(1, ATT_WIDTH))


def _rwkv_kernel(pr_ref, pk_ref, pv_ref, pl_ref, mr_ref, mk_ref, mv_ref, ml_ref,
                 w2_ref, a2_ref, g2_ref, w0_ref, a0_ref, kk_ref, ka_ref, rk_ref, lnw_ref, lnb_ref,
                 o_ref, s_ref, prev_ref, prevl_ref, y_ref, *, ts):
    C = RWKV_CHUNK
    N = RWKV_HEAD_DIM
    n_chunks = ts // C

    @pl.when(pl.program_id(2) == 0)
    def _():
        s_ref[...] = jnp.zeros_like(s_ref)
        prev_ref[...] = jnp.zeros_like(prev_ref)
        prevl_ref[...] = jnp.zeros_like(prevl_ref)

    first_row = lax.broadcasted_iota(jnp.int32, (ts, 1), 0) == 0

    def shift(x, last_prev, mix):
        prev = jnp.where(first_row, last_prev, pltpu.roll(x, 1, 0))
        return x + (prev - x) * mix

    raw = [pr_ref[...], pk_ref[...], pv_ref[...]]
    raw_l = pl_ref[...]
    r_all = shift(raw[0], prev_ref[0:1, :], mr_ref[...])
    k_all = shift(raw[1], prev_ref[1:2, :], mk_ref[...])
    v_all = shift(raw[2], prev_ref[2:3, :], mv_ref[...])
    p_l = shift(raw_l, prevl_ref[...], ml_ref[...])
    for i in range(3):
        prev_ref[i:i + 1, :] = raw[i][ts - 1:ts, :]
    prevl_ref[...] = raw_l[ts - 1:ts, :]

    wd = o_ref.shape[-1]
    n_pairs = wd // LANES
    r128 = lax.broadcasted_iota(jnp.int32, (LANES, LANES), 0)
    c128 = lax.broadcasted_iota(jnp.int32, (LANES, LANES), 1)
    same_head = (r128 < N) == (c128 < N)
    rw = lax.broadcasted_iota(jnp.int32, (wd, wd), 0)
    cw = lax.broadcasted_iota(jnp.int32, (wd, wd), 1)
    head_ones = jnp.where((rw >> 6) == (cw >> 6), 1.0, 0.0).astype(BF16)
    head_ones2 = jnp.concatenate([head_ones, head_ones], axis=0)

    lw_pre = w0_ref[...] + _dot(jnp.tanh(p_l[:, :LANES]).astype(BF16), w2_ref[...])
    w_log = -(jnp.maximum(-lw_pre, 0.0) + jnp.log(1.0 + jnp.exp(-jnp.abs(lw_pre)))) - 0.5
    lw_all = -jnp.exp(w_log)
    rate = jax.nn.sigmoid(a0_ref[...] + _dot(p_l[:, :LANES].astype(BF16), a2_ref[...]))
    gate = _dot(jax.nn.sigmoid(p_l[:, LANES:]).astype(BF16), g2_ref[...])
    kk = k_all * kk_ref[...]
    norm = jnp.sqrt(_dot2(kk * kk, head_ones2))
    kk = kk / jnp.maximum(norm, 1e-12)
    k_all = k_all * (1.0 + (rate - 1.0) * ka_ref[...])
    a_all = -kk
    b_all = kk * rate

    row = lax.broadcasted_iota(jnp.int32, (C, LANES), 0)
    lane = lax.broadcasted_iota(jnp.int32, (C, LANES), 1)
    col = lane & (N - 1)
    head0 = lane < N
    strict = row > col
    incl = row >= col
    eye = jnp.where(row == col, 1.0, 0.0)
    blk16 = (row >> 4) == (col >> 4)
    blk32 = (row >> 5) == (col >> 5)
    tri2 = jnp.where(incl, 1.0, 0.0).astype(BF16)
    ones = jnp.ones((C, LANES), BF16)

    def bd(x):
        z = jnp.zeros_like(x)
        return jnp.concatenate([jnp.where(head0, x, z), jnp.where(head0, z, x)], axis=0)

    def mm3(x, y):
        xh, xl = _split(x)
        yh, yl = _split(y)
        byh, byl = bd(yh), bd(yl)
        rhs = jnp.concatenate([jnp.concatenate([byh, byl], axis=1),
                               jnp.concatenate([byh, jnp.zeros_like(byl)], axis=1)], axis=0)
        p = _dot(jnp.concatenate([xh, xl], axis=1), rhs)
        return p[:, :LANES] + p[:, LANES:]

    def mm1(x, y_bd):
        return _dot(x.astype(BF16), y_bd)

    cs = range(n_pairs * n_chunks)
    chunk = lambda x: [x[c * C:(c + 1) * C, j * LANES:(j + 1) * LANES]
                       for j in range(n_pairs) for c in range(n_chunks)]
    r, k, v, lw, a, b = (chunk(x) for x in (r_all, k_all, v_all, lw_all, a_all, b_all))
    lws = [_split(x) for x in lw]
    cum = [_dot(tri2, jnp.concatenate([h, l], axis=0)) for h, l in lws]
    wcol = [jnp.exp(_dot_tn(h, ones) + _dot_tn(l, ones)) for h, l in lws]
    e_neg = [jnp.exp(-x) for x in cum]
    e_rem = [jnp.exp(x[C - 1:C, :] - x) for x in cum]
    rt = [r[c] * jnp.exp(cum[c]) for c in cs]
    at = [a[c] * jnp.exp(cum[c] - lw[c]) for c in cs]
    kt = [k[c] * e_neg[c] for c in cs]
    bt = [b[c] * e_neg[c] for c in cs]
    kw = [k[c] * e_rem[c] for c in cs]
    bw = [b[c] * e_rem[c] for c in cs]

    at_b = [x.astype(BF16) for x in at]
    rt_b = [x.astype(BF16) for x in rt]
    lhs = [jnp.concatenate([at_b[c], rt_b[c]], axis=0) for c in cs]
    rhs = [jnp.concatenate([bd(bt[c].astype(BF16)), bd(kt[c].astype(BF16))], axis=0) for c in cs]
    aa = [_dot_nt(lhs[c], rhs[c]) for c in cs]
    L = LANES
    a_ab = [jnp.where(strict, x[:C, :L], 0.0) for x in aa]
    a_ak = [jnp.where(strict, x[:C, L:], 0.0) for x in aa]
    a_rb = [jnp.where(incl, x[C:, :L], 0.0) for x in aa]
    a_rk = [jnp.where(incl, x[C:, L:], 0.0) for x in aa]

    d = [jnp.where(blk16, x, 0.0) for x in a_ab]
    x = [eye + dd for dd in d]
    y = [mm3(dd, dd) for dd in d]
    for _ in range(2):
        x = [x[c] + mm3(x[c], y[c]) for c in cs]
        y = [mm3(yy, yy) for yy in y]
    x = [x[c] + mm3(x[c], y[c]) for c in cs]
    for in_block in (blk32, None):
        if in_block is blk32:
            off = [jnp.where(blk32 & jnp.logical_not(blk16), aa_, 0.0) for aa_ in a_ab]
        else:
            off = [jnp.where(blk32, 0.0, aa_) for aa_ in a_ab]
        xo = [mm1(x[c], bd(off[c].astype(BF16))) for c in cs]
        x = [x[c] + mm1(xo[c], bd(x[c].astype(BF16))) for c in cs]
    t_inv = x

    rhs = [jnp.concatenate([bd(at_b[c]), bd(a_ak[c].astype(BF16))], axis=1) for c in cs]
    tt = [mm1(t_inv[c], rhs[c]) for c in cs]
    ap = [x_[:, :L].astype(BF16) for x_ in tt]
    ta = [x_[:, L:] for x_ in tt]
    bv = [bd(x_.astype(BF16)) for x_ in v]
    u_loc = [mm1(ta[c], bv[c]).astype(BF16) for c in cs]
    y_loc = [mm1(a_rk[c], bv[c]) for c in cs]
    a_rb = [x_.astype(BF16) for x_ in a_rb]
    bw_b = [x_.astype(BF16) for x_ in bw]
    bkw = [jnp.concatenate([bw_b[c], kw[c].astype(BF16)], axis=0) for c in cs]
    p_mat = [jnp.where(same_head, _dot_tn(bw_b[c], ap[c]), 0.0).astype(BF16) for c in cs]
    g_mat = [jnp.where(same_head, _dot_tn(bkw[c], jnp.concatenate([u_loc[c], v[c].astype(BF16)], axis=0)),
                       0.0) for c in cs]
    q_mat = [(rt[c] + _dot(a_rb[c], bd(ap[c]))).astype(BF16) for c in cs]
    z_mat = [_dot(a_rb[c], bd(u_loc[c])) + y_loc[c] for c in cs]

    states = [s_ref[j] for j in range(n_pairs)]
    entry = [None] * len(cs)
    for c in range(n_chunks):
        for j in range(n_pairs):
            i = j * n_chunks + c
            s_b = states[j].astype(BF16)
            entry[i] = s_b
            states[j] = states[j] * wcol[i] + _dot(p_mat[i], s_b) + g_mat[i]
    for j in range(n_pairs):
        s_ref[j] = states[j]
        for c in range(n_chunks):
            i = j * n_chunks + c
            y_ref[c * C:(c + 1) * C, j * LANES:(j + 1) * LANES] = _dot(q_mat[i], entry[i]) + z_mat[i]

    inv_n = 1.0 / N
    y_all = y_ref[...]
    mu = _dot2(y_all, head_ones2) * inv_n
    dev = y_all - mu
    var = _dot2(dev * dev, head_ones2) * inv_n
    yn = dev * lax.rsqrt(var + RWKV_LN_EPS) * lnw_ref[...] + lnb_ref[...]
    bonus = _dot2(r_all * k_all * rk_ref[...], head_ones2) * v_all
    o_ref[...] = ((yn + bonus) * gate).astype(o_ref.dtype)


def _rwkv(p_rkv, p_lora, mix_rkv, mix_lora, w2p, a2p, g2p, w0, a0, k_k, k_a, r_k, ln_w, ln_b,
          ts=512, pairs_per_step=2):
    bsz, seq, _ = p_rkv.shape
    W = RWKV_WIDTH
    nl = p_lora.shape[-1]
    wd = pairs_per_step * LANES
    npair = W // wd
    col = lambda j: pl.BlockSpec((None, ts, wd), lambda bb, p, t, j=j: (bb, t, j * npair + p))
    vec_col = lambda j: pl.BlockSpec((1, wd), lambda bb, p, t, j=j: (0, j * npair + p))
    vec = pl.BlockSpec((1, wd), lambda bb, p, t: (0, p))
    mat = lambda rows: pl.BlockSpec((rows, wd), lambda bb, p, t: (0, p))
    kern = functools.partial(_rwkv_kernel, ts=ts)
    return pl.pallas_call(
        kern,
        grid=(bsz, npair, seq // ts),
        in_specs=[
            col(0), col(1), col(2),
            pl.BlockSpec((None, ts, nl), lambda bb, p, t: (bb, t, 0)),
            vec_col(0), vec_col(1), vec_col(2),
            pl.BlockSpec((1, nl), lambda bb, p, t: (0, 0)),
            mat(LANES), mat(LANES), mat(nl - LANES),
            vec, vec, vec, vec, vec, vec, vec,
        ],
        out_specs=pl.BlockSpec((None, ts, wd), lambda bb, p, t: (bb, t, p)),
        out_shape=jax.ShapeDtypeStruct((bsz, seq, W), BF16),
        scratch_shapes=[
            pltpu.VMEM((pairs_per_step, LANES, LANES), F32),
            pltpu.VMEM((8, wd), F32),
            pltpu.VMEM((1, nl), F32),
            pltpu.VMEM((ts, wd), F32),
        ],
        compiler_params=_params("arbitrary", "arbitrary", "arbitrary"),
        name="rwkv",
    )(p_rkv, p_rkv, p_rkv, p_lora, mix_rkv, mix_rkv, mix_rkv, mix_lora,
      w2p, a2p, g2p, w0, a0, k_k, k_a, r_k, ln_w, ln_b)


def _out_proj_kernel(oa_ref, or_ref, x_ref, w_ref, gpost_ref, gt_ref, gpre_ref, sc_ref, sh_ref,
                     wr_ref, br_ref, x1_ref, h2_ref, meta_ref, cnt_ref, run_ref):
    tm = x_ref.shape[0]
    half = oa_ref.shape[-1]

    @pl.when((pl.program_id(0) == 0) & (pl.program_id(1) == 0))
    def _():
        run_ref[...] = jnp.zeros_like(run_ref)

    y = _dot(oa_ref[...], w_ref[:half, :]) + _dot(or_ref[...], w_ref[half:, :])
    ms = jnp.mean(y * y, axis=-1, keepdims=True)
    x1 = x_ref[...] + gt_ref[...] * (y * lax.rsqrt(ms + RMS_EPS) * gpost_ref[...])
    x1_ref[...] = x1
    ms1 = jnp.mean(x1 * x1, axis=-1, keepdims=True)
    h2 = (x1 * lax.rsqrt(ms1 + RMS_EPS) * gpre_ref[...]) * (1.0 + sc_ref[...]) + sh_ref[...]
    h2_ref[...] = _pack_halves(h2)

    lg = _dot3(h2, wr_ref[...]) + br_ref[...]
    lane_i = lax.broadcasted_iota(jnp.int32, lg.shape, 1)
    lane = lane_i.astype(F32)
    lane_group = ((lane_i - N_GROUPS) >> 3).astype(F32)
    first = lambda mask: jnp.min(jnp.where(mask, lane, 4.0 * LANES), axis=-1, keepdims=True)
    gl = jnp.where(lane_i < N_GROUPS, lg, NEG_BIG)
    gmax = jnp.max(gl, axis=-1, keepdims=True)
    g_sel = first(gl == gmax)
    p_group = 1.0 / jnp.sum(jnp.exp(gl - gmax), axis=-1, keepdims=True)
    in_group = (lane_i >= N_GROUPS) & (lane_group == g_sel)
    el = jnp.where(in_group, lg, NEG_BIG)
    m0 = jnp.max(el, axis=-1, keepdims=True)
    i0 = first(el == m0)
    el = jnp.where(lane == i0, NEG_BIG, el)
    m1 = jnp.max(el, axis=-1, keepdims=True)
    i1 = first(el == m1)
    t = jnp.exp(m1 - m0)
    gate0 = p_group / (1.0 + t)
    gate1 = p_group * t / (1.0 + t)
    e0 = i0 - N_GROUPS
    e1 = i1 - N_GROUPS
    hit = (lane == e0) | (lane == e1)
    onehot = jnp.where(hit, 1.0, 0.0).astype(BF16)
    rr = lax.broadcasted_iota(jnp.int32, (tm, tm), 0)
    cc = lax.broadcasted_iota(jnp.int32, (tm, tm), 1)
    before = jnp.where(cc < rr, 1.0, 0.0).astype(BF16)
    seen = _dot(before, onehot) + run_ref[...]
    rank0 = jnp.sum(jnp.where(lane == e0, seen, 0.0), axis=-1, keepdims=True)
    rank1 = jnp.sum(jnp.where(lane == e1, seen, 0.0), axis=-1, keepdims=True)
    total = seen[tm - 1:tm, :] + jnp.where(hit[tm - 1:tm, :], 1.0, 0.0)
    run_ref[...] = total
    cnt_ref[...] = total
    meta = jnp.zeros(lg.shape, F32)
    for idx, val in ((META_E0, e0.astype(F32)), (META_E1, e1.astype(F32)), (META_RANK0, rank0),
                     (META_RANK1, rank1), (META_GATE0, gate0), (META_GATE1, gate1)):
        meta = jnp.where(lane == idx, val, meta)
    meta_ref[...] = meta


def _out_proj(o_att, o_rwkv, x, w_out, g_post, gt, g_pre, sc, sh, w_router, b_router, tm=512):
    bsz, seq, d = x.shape
    half = o_att.shape[-1]
    const = lambda b, i: (0, 0)
    row = lambda b, i: (b, i, 0)
    per_b = lambda b, i: (b, 0, 0)
    vec = pl.BlockSpec((1, d), const)
    mod = pl.BlockSpec((None, 1, d), per_b)
    return pl.pallas_call(
        _out_proj_kernel,
        grid=(bsz, seq // tm),
        in_specs=[
            pl.BlockSpec((None, tm, half), row),
            pl.BlockSpec((None, tm, half), row),
            pl.BlockSpec((None, tm, d), row),
            pl.BlockSpec((d, d), const, pipeline_mode=pl.Buffered(1)),
            vec, mod, vec, mod, mod,
            pl.BlockSpec((d, LANES), const),
            pl.BlockSpec((1, LANES), const),
        ],
        out_specs=[
            pl.BlockSpec((None, tm, d), row),
            pl.BlockSpec((None, tm, d // 2), row),
            pl.BlockSpec((None, tm, LANES), row),
            pl.BlockSpec((1, LANES), const),
        ],
        out_shape=[
            jax.ShapeDtypeStruct((bsz, seq, d), F32),
            jax.ShapeDtypeStruct((bsz, seq, d // 2), U32),
            jax.ShapeDtypeStruct((bsz, seq, LANES), F32),
            jax.ShapeDtypeStruct((1, LANES), F32),
        ],
        scratch_shapes=[pltpu.VMEM((1, LANES), F32)],
        compiler_params=_params("arbitrary", "arbitrary"),
        name="out_proj",
    )(o_att, o_rwkv, x, w_out, g_post.reshape(1, d), gt, g_pre.reshape(1, d), sc, sh, w_router, b_router)


def _pack_halves(x):
    d2 = x.shape[-1] // 2
    lo_bits = lax.bitcast_convert_type(x[:, :d2].astype(BF16).astype(F32), U32)
    hi_bits = lax.bitcast_convert_type(x[:, d2:].astype(BF16).astype(F32), U32)
    return (lo_bits >> 16) | hi_bits


def _unpack_halves(words):
    lo = lax.bitcast_convert_type(words << 16, F32)
    hi = lax.bitcast_convert_type(words & jnp.uint32(0xFFFF0000), F32)
    return lo, hi


def _expert_kernel(te_ref, nv_ref, n1_ref, n2_ref, sl_ref, x_ref, wg_hbm, wu_hbm, wd_hbm, o_ref,
                   wg_f, wu_f, wd_f, wg_b, wu_b, wd_b, sem):
    i = pl.program_id(0)
    valid = i < nv_ref[0]
    prev = te_ref[jnp.maximum(i - 1, 0)]
    new_expert = (i == 0) | (te_ref[i] != prev)

    def weight_copies(e, slot):
        return (pltpu.make_async_copy(wg_hbm.at[e], wg_f.at[slot], sem.at[slot, 0]),
                pltpu.make_async_copy(wu_hbm.at[e], wu_f.at[slot], sem.at[slot, 1]),
                pltpu.make_async_copy(wd_hbm.at[e], wd_f.at[slot], sem.at[slot, 2]))

    def next_slot(slot, k):
        s = slot + k
        return jnp.where(s >= WEIGHT_SLOTS, s - WEIGHT_SLOTS, s)

    @pl.when(valid & (i == 0))
    def _():
        for cp in weight_copies(te_ref[0], sl_ref[0]):
            cp.start()

        @pl.when(n1_ref[0] >= 0)
        def _():
            for cp in weight_copies(n1_ref[0], next_slot(sl_ref[0], 1)):
                cp.start()

    @pl.when(valid & new_expert)
    def _():
        slot = sl_ref[i]
        for cp in weight_copies(te_ref[i], slot):
            cp.wait()

        @pl.when(n2_ref[i] >= 0)
        def _():
            for cp in weight_copies(n2_ref[i], next_slot(slot, 2)):
                cp.start()

        wg_b[...] = wg_f[slot].astype(BF16)
        wu_b[...] = wu_f[slot].astype(BF16)
        wd_b[...] = wd_f[slot].astype(BF16)

    @pl.when(valid)
    def _():
        x_lo, x_hi = _unpack_halves(x_ref[...])
        d2 = x_lo.shape[-1]
        x_lo, x_hi = x_lo.astype(BF16), x_hi.astype(BF16)
        g = _dot(x_lo, wg_b[:d2, :]) + _dot(x_hi, wg_b[d2:, :])
        u = _dot(x_lo, wu_b[:d2, :]) + _dot(x_hi, wu_b[d2:, :])
        hid = (g * jax.nn.sigmoid(g) * u).astype(BF16)
        o_ref[...] = _pack_halves(_dot(hid, wd_b[...]))

    @pl.when(jnp.logical_not(valid))
    def _():
        o_ref[...] = jnp.zeros_like(o_ref)


def _experts(xs, tile_expert, n_valid, next1, next2, slot, w_gate, w_up, w_down, tm):
    m_pad, d2 = xs.shape
    d = 2 * d2
    de = w_gate.shape[-1]
    n_tiles = m_pad // tm
    grid_spec = pltpu.PrefetchScalarGridSpec(
        num_scalar_prefetch=5,
        grid=(n_tiles,),
        in_specs=[
            pl.BlockSpec((tm, d2), lambda i, *_: (i, 0)),
            pl.BlockSpec(memory_space=pl.ANY),
            pl.BlockSpec(memory_space=pl.ANY),
            pl.BlockSpec(memory_space=pl.ANY),
        ],
        out_specs=pl.BlockSpec((tm, d2), lambda i, *_: (i, 0)),
        scratch_shapes=[
            pltpu.VMEM((WEIGHT_SLOTS, d, de), F32),
            pltpu.VMEM((WEIGHT_SLOTS, d, de), F32),
            pltpu.VMEM((WEIGHT_SLOTS, de, d), F32),
            pltpu.VMEM((d, de), BF16),
            pltpu.VMEM((d, de), BF16),
            pltpu.VMEM((de, d), BF16),
            pltpu.SemaphoreType.DMA((WEIGHT_SLOTS, 3)),
        ],
    )
    return pl.pallas_call(
        _expert_kernel,
        grid_spec=grid_spec,
        out_shape=jax.ShapeDtypeStruct((m_pad, d2), U32),
        compiler_params=_params("arbitrary"),
        name="experts",
    )(tile_expert, n_valid, next1, next2, slot, xs, w_gate, w_up, w_down)


def _final_kernel(x_ref, y0_ref, y1_ref, meta_ref, g_ref, gt_ref, o_ref):
    meta = meta_ref[...]
    gate0 = meta[:, META_GATE0:META_GATE0 + 1]
    gate1 = meta[:, META_GATE1:META_GATE1 + 1]
    d2 = y0_ref.shape[-1]
    halves = [a * gate0 + b * gate1 for a, b in zip(_unpack_halves(y0_ref[...]), _unpack_halves(y1_ref[...]))]
    ssq = sum(jnp.sum(h * h, axis=-1, keepdims=True) for h in halves)
    inv = lax.rsqrt(ssq * (1.0 / (2 * d2)) + RMS_EPS)
    for j, h in enumerate(halves):
        cols = slice(j * d2, (j + 1) * d2)
        o_ref[:, cols] = x_ref[:, cols] + gt_ref[:, cols] * (h * inv * g_ref[:, cols])


def _final(x1, y0, y1, meta, g, gt, tm=512):
    bsz, seq, d = x1.shape
    row = lambda b, i: (b, i, 0)
    blk = pl.BlockSpec((None, tm, d), row)
    packed = pl.BlockSpec((None, tm, d // 2), row)
    return pl.pallas_call(
        _final_kernel,
        grid=(bsz, seq // tm),
        in_specs=[blk, packed, packed,
                  pl.BlockSpec((None, tm, LANES), row),
                  pl.BlockSpec((1, d), lambda b, i: (0, 0)),
                  pl.BlockSpec((None, 1, d), lambda b, i: (b, 0, 0))],
        out_specs=blk,
        out_shape=jax.ShapeDtypeStruct((bsz, seq, d), F32),
        compiler_params=_params("arbitrary", "arbitrary"),
        name="final",
    )(x1, y0, y1, meta, g.reshape(1, d), gt)


def _dispatch(meta, counts, tm):
    n_tok = meta.shape[0]
    n_rows = n_tok * TOP_K
    n_tiles = n_rows // tm + N_EXPERTS
    expert = meta[:, META_E0:META_E1 + 1].astype(jnp.int32)
    rank = meta[:, META_RANK0:META_RANK1 + 1].astype(jnp.int32)
    counts = counts[0, :N_EXPERTS].astype(jnp.int32)
    tiles_e = (counts + tm - 1) // tm
    tile_end = jnp.cumsum(tiles_e)
    pad_start = (tile_end - tiles_e) * tm
    slot_pos = pad_start[expert] + rank
    n_valid = tile_end[-1]
    t_idx = jnp.arange(n_tiles, dtype=jnp.int32)
    tile_expert = jnp.searchsorted(tile_end, jnp.minimum(t_idx, n_valid - 1), side='right')
    tile_expert = jnp.minimum(tile_expert, N_EXPERTS - 1).astype(jnp.int32)
    tok = jnp.broadcast_to(jnp.arange(n_tok, dtype=jnp.int32)[:, None], (n_tok, TOP_K))
    filler = jnp.arange(n_tiles * tm, dtype=jnp.int32) % n_tok
    src_tok = filler.at[slot_pos.reshape(-1)].set(tok.reshape(-1))
    ids = jnp.arange(N_EXPERTS, dtype=jnp.int32)
    used = tiles_e > 0
    first_used_from = lax.cummin(jnp.where(used, ids, N_EXPERTS), reverse=True)
    next_used = jnp.concatenate([first_used_from[1:], jnp.full((1,), N_EXPERTS, jnp.int32)])
    next_used = jnp.where(next_used >= N_EXPERTS, -1, next_used)
    next2_used = jnp.where(next_used >= 0, next_used[jnp.maximum(next_used, 0)], -1)
    run_slot = (jnp.cumsum(used.astype(jnp.int32)) - 1) % WEIGHT_SLOTS
    per_tile = lambda table: table[tile_expert].astype(jnp.int32)
    return (src_tok, slot_pos, tile_expert, n_valid.reshape(1).astype(jnp.int32),
            per_tile(next_used), per_tile(next2_used), per_tile(run_slot))


def _pad_rows(w, rows_before, rows_total):
    return jnp.pad(w, ((rows_before, rows_total - rows_before - w.shape[0]), (0, 0))).astype(BF16)


def kernel(x, c, w_ada, b_ada, g_pre_mix, g_post_mix, g_pre_ffn, g_post_ffn, w_in, w_out,
           attn_out_gain, rwkv_shift_mix, rwkv_w0, rwkv_w2, rwkv_a0, rwkv_a2, rwkv_g2,
           rwkv_k_k, rwkv_k_a, rwkv_r_k, rwkv_ln_w, rwkv_ln_b, router_group_w, router_group_b,
           router_expert_w, router_expert_b, expert_w_gate, expert_w_up, expert_w_down):
    bsz, seq, d = x.shape
    depth = w_ada.shape[0]
    tm_moe = 256
    W = RWKV_WIDTH
    for l in range(depth):
        mod = _adaln(c, w_ada[l], b_ada[l])
        sh_m, sc_m, gt_m, sh_f, sc_f, gt_f = jnp.split(mod[:, None, :], N_MOD, axis=-1)

        w = w_in[l]
        n_att = 3 * ATT_WIDTH
        n_rkv = 3 * W
        n_w, n_a, n_g = rwkv_w2.shape[1], rwkv_a2.shape[1], rwkv_g2.shape[1]
        assert n_w + n_a == LANES and w.shape[1] == n_att + n_rkv + n_w + n_a + n_g
        n_lora = LANES + -(-n_g // LANES) * LANES
        pad_l = n_lora - (n_w + n_a + n_g)
        w_att = w[:, :n_att].astype(BF16)
        w_rkv = w[:, n_att:n_att + n_rkv].astype(BF16)
        w_lora = jnp.pad(w[:, n_att + n_rkv:], ((0, 0), (0, pad_l))).astype(BF16)
        qkv, p_rkv, p_lora = _in_proj(x, g_pre_mix[l], sc_m, sh_m, w_att, w_rkv, w_lora)
        o_att = _attention(qkv, attn_out_gain[l])
        mix = rwkv_shift_mix[l]
        row = lambda t: t.reshape(1, -1)
        o_rwkv = _rwkv(
            p_rkv, p_lora, row(mix[:n_rkv]), row(jnp.pad(mix[n_rkv:], (0, pad_l))),
            _pad_rows(rwkv_w2[l], 0, LANES), _pad_rows(rwkv_a2[l], n_w, LANES),
            _pad_rows(rwkv_g2[l], 0, n_lora - LANES),
            row(rwkv_w0[l]), row(rwkv_a0[l]), row(rwkv_k_k[l]), row(rwkv_k_a[l]), row(rwkv_r_k[l]),
            row(rwkv_ln_w[l]), row(rwkv_ln_b[l]))

        n_route = N_GROUPS + N_EXPERTS
        w_router = jnp.concatenate([router_group_w[l], router_expert_w[l]], axis=1)
        w_router = jnp.pad(w_router, ((0, 0), (0, LANES - n_route)))
        b_router = jnp.pad(jnp.concatenate([router_group_b[l], router_expert_b[l]]), (0, LANES - n_route))
        x1, h2w, meta, counts = _out_proj(o_att, o_rwkv, x, w_out[l].astype(BF16), g_post_mix[l], gt_m,
                                          g_pre_ffn[l], sc_f, sh_f, w_router, b_router.reshape(1, LANES))

        src_tok, slot_pos, tile_expert, n_valid, next1, next2, slot = _dispatch(
            meta.reshape(bsz * seq, LANES), counts, tm_moe)
        xs = h2w.reshape(bsz * seq, d // 2)[src_tok]
        rows = _experts(xs, tile_expert, n_valid, next1, next2, slot,
                        expert_w_gate[l], expert_w_up[l], expert_w_down[l], tm_moe)
        y0 = rows[slot_pos[:, 0]].reshape(bsz, seq, d // 2)
        y1 = rows[slot_pos[:, 1]].reshape(bsz, seq, d // 2)
        x = _final(x1, y0, y1, meta, g_post_ffn[l], gt_f)
    return x
```

```python
import functools

import jax
import jax.numpy as jnp
from jax import lax
from jax.experimental import pallas as pl
from jax.experimental.pallas import tpu as pltpu

F32 = jnp.float32
BF16 = jnp.bfloat16
U32 = jnp.uint32

LANES = 128
VMEM_LIMIT_BYTES = 56 * 1024 * 1024

ATT_HEADS = 8
ATT_HEAD_DIM = 128
ATT_WIDTH = ATT_HEADS * ATT_HEAD_DIM
RWKV_HEAD_DIM = 64
RWKV_WIDTH = 1024
RWKV_CHUNK = 64
RWKV_LN_EPS = 64e-5
RMS_EPS = 1e-6
LOG2_E = 1.4426950408889634
EXP2_CLAMP = 126.0
N_GROUPS = 8
EXPERTS_PER_GROUP = 8
N_EXPERTS = 64
TOP_K = 2
N_MOD = 6
NEG_BIG = -1e30
WEIGHT_SLOTS = 3
META_E0, META_E1, META_RANK0, META_RANK1, META_GATE0, META_GATE1 = range(6)


def _dot(a, b):
    return jnp.dot(a, b, preferred_element_type=F32)


def _dot_nt(a, b):
    return lax.dot_general(a, b, (((1,), (1,)), ((), ())), preferred_element_type=F32)


def _dot_tn(a, b):
    return lax.dot_general(a, b, (((0,), (0,)), ((), ())), preferred_element_type=F32)


def _split(x):
    hi = x.astype(BF16)
    lo = (x - hi.astype(F32)).astype(BF16)
    return hi, lo


def _dot3(a, b):
    ah, al = _split(a)
    bh, bl = _split(b)
    return _dot(ah, bh) + _dot(ah, bl) + _dot(al, bh)


def _dot2(x, m2):
    hi, lo = _split(x)
    return _dot(jnp.concatenate([hi, lo], axis=1), m2)


def _params(*sem):
    return pltpu.CompilerParams(dimension_semantics=sem, vmem_limit_bytes=VMEM_LIMIT_BYTES)


def _adaln_kernel(c_ref, w_ref, b_ref, o_ref):
    c = c_ref[...]
    s = (c * jax.nn.sigmoid(c)).astype(BF16)
    o_ref[...] = _dot(s, w_ref[...].astype(BF16)) + b_ref[...]


def _adaln(c, w, b):
    bsz, d = c.shape
    n = w.shape[1]
    rows = 8
    tn = 1536
    cp = jnp.zeros((rows, d), F32).at[:bsz].set(c)
    out = pl.pallas_call(
        _adaln_kernel,
        grid=(n // tn,),
        in_specs=[
            pl.BlockSpec((rows, d), lambda j: (0, 0)),
            pl.BlockSpec((d, tn), lambda j: (0, j)),
            pl.BlockSpec((1, tn), lambda j: (0, j)),
        ],
        out_specs=pl.BlockSpec((rows, tn), lambda j: (0, j)),
        out_shape=jax.ShapeDtypeStruct((rows, n), F32),
        compiler_params=_params("arbitrary"),
        name="adaln",
    )(cp, w, b.reshape(1, n))
    return out[:bsz]


def _in_proj_kernel(x_ref, g_ref, sc_ref, sh_ref, w_ref, qs_ref, oa_ref, or_ref, ol_ref):
    x = x_ref[...]
    ms = jnp.mean(x * x, axis=-1, keepdims=True)
    y = x * lax.rsqrt(ms + RMS_EPS) * g_ref[...]
    h = (y * (1.0 + sc_ref[...]) + sh_ref[...]).astype(BF16)
    na, nr = oa_ref.shape[-1], or_ref.shape[-1]
    oa_ref[...] = (_dot(h, w_ref[:, :na]) * qs_ref[...]).astype(oa_ref.dtype)
    or_ref[...] = _dot(h, w_ref[:, na:na + nr])
    ol_ref[...] = _dot(h, w_ref[:, na + nr:])


def _in_proj(x, g, sc, sh, w, na, nr, tm=256):
    bsz, seq, d = x.shape
    nl = w.shape[1] - na - nr
    q_scale = jnp.where(jnp.arange(na) < ATT_WIDTH, LOG2_E * ATT_HEAD_DIM ** -0.5, 1.0).astype(F32).reshape(1, na)
    const = lambda b, i: (0, 0)
    row = lambda b, i: (b, i, 0)
    per_b = lambda b, i: (b, 0, 0)
    return pl.pallas_call(
        _in_proj_kernel,
        grid=(bsz, seq // tm),
        in_specs=[
            pl.BlockSpec((None, tm, d), row),
            pl.BlockSpec((1, d), const),
            pl.BlockSpec((None, 1, d), per_b),
            pl.BlockSpec((None, 1, d), per_b),
            pl.BlockSpec((d, na + nr + nl), const, pipeline_mode=pl.Buffered(1)),
            pl.BlockSpec((1, na), const),
        ],
        out_specs=[
            pl.BlockSpec((None, tm, na), row),
            pl.BlockSpec((None, tm, nr), row),
            pl.BlockSpec((None, tm, nl), row),
        ],
        out_shape=[
            jax.ShapeDtypeStruct((bsz, seq, na), BF16),
            jax.ShapeDtypeStruct((bsz, seq, nr), F32),
            jax.ShapeDtypeStruct((bsz, seq, nl), F32),
        ],
        compiler_params=_params("arbitrary", "arbitrary"),
        name="in_proj",
    )(x, g.reshape(1, d), sc, sh, w, q_scale)


def _att_kernel(q_ref, k_ref, v_ref, gain_ref, o_ref, *, tq, tk):
    sub = LANES
    n_sub = tk // sub
    qi = pl.program_id(2)
    q = q_ref[...]
    r2 = lax.broadcasted_iota(jnp.int32, (2 * sub, 2 * sub), 0) & (sub - 1)
    c2 = lax.broadcasted_iota(jnp.int32, (2 * sub, 2 * sub), 1)
    tri_ones = jnp.where((c2 >= sub) | (r2 >= c2), 1.0, 0.0).astype(BF16)

    lower = (lax.broadcasted_iota(jnp.int32, (sub, sub), 1)
             < lax.broadcasted_iota(jnp.int32, (sub, sub), 0))

    def sweep(items):
        n = [it[1].shape[0] // sub for it in items]
        tiles = lambda x, m: [x[:, u * sub:(u + 1) * sub] for u in range(m)]
        z = [_dot_nt(it[0], it[1]) for it in items]
        sp = [jnp.maximum(jnp.log2(1.0 + jnp.exp2(jnp.minimum(x, EXP2_CLAMP))), x) for x in z]
        z = [tiles(x, m) for x, m in zip(z, n)]
        sp = [tiles(x, m) for x, m in zip(sp, n)]
        for i, it in enumerate(items):
            if it[5]:
                sp[i][-1] = jnp.where(lower, sp[i][-1], 0.0)
        cs = [[_dot(jnp.concatenate(_split(x), axis=1), tri_ones) for x in row] for row in sp]
        out = []
        for i, it in enumerate(items):
            carry, acc = (it[3], it[4]) if it[3] is not None else out[-1]
            w = [None] * n[i]
            for u in reversed(range(n[i])):
                w[u] = jnp.exp2(z[i][u] - (cs[i][u][:, :sub] + carry))
                carry = carry + cs[i][u][:, sub:]
            if it[5]:
                w[-1] = jnp.where(lower, w[-1], 0.0)
            acc = acc + _dot(jnp.concatenate(w, axis=1).astype(BF16), it[2])
            out.append((carry, acc))
        return out

    assert tq == tk
    start = pl.multiple_of(qi * tk, tk)
    zeros = jnp.zeros((sub, sub), F32)
    diag = sweep([(q[r * sub:(r + 1) * sub, :], k_ref[pl.ds(start, (r + 1) * sub), :],
                   v_ref[pl.ds(start, (r + 1) * sub), :], zeros, zeros, True) for r in range(n_sub)])
    carry = jnp.concatenate([c for c, _ in diag], axis=0)
    acc = jnp.concatenate([a for _, a in diag], axis=0)

    def keys(j):
        st = pl.multiple_of(j * tk, tk)
        return k_ref[pl.ds(st, tk), :], v_ref[pl.ds(st, tk), :]

    def two_tiles(i, state):
        j = qi - 1 - 2 * i
        return sweep([(q, *keys(j), state[0], state[1], False), (q, *keys(j - 1), None, None, False)])[-1]

    def one_tile(i, state):
        return sweep([(q, *keys(0), state[0], state[1], False)])[0]

    carry, acc = lax.fori_loop(0, lax.shift_right_logical(qi, 1), two_tiles, (carry, acc))
    carry, acc = lax.fori_loop(0, qi & 1, one_tile, (carry, acc))
    ms = jnp.mean(acc * acc, axis=-1, keepdims=True)
    o_ref[...] = (acc * lax.rsqrt(ms + RMS_EPS) * gain_ref[...]).astype(o_ref.dtype)


def _attention(qkv, gain, tq=512, tk=512):
    bsz, seq, _ = qkv.shape
    nh = ATT_HEADS
    assert tk % tq == 0 and seq % tk == 0
    kern = functools.partial(_att_kernel, tq=tq, tk=tk)
    return pl.pallas_call(
        kern,
        grid=(bsz, nh, seq // tq),
        in_specs=[
            pl.BlockSpec((None, tq, ATT_HEAD_DIM), lambda b, h, i: (b, i, h)),
            pl.BlockSpec((None, seq, ATT_HEAD_DIM), lambda b, h, i: (b, 0, nh + h)),
            pl.BlockSpec((None, seq, ATT_HEAD_DIM), lambda b, h, i: (b, 0, 2 * nh + h)),
            pl.BlockSpec((1, ATT_HEAD_DIM), lambda b, h, i: (0, h)),
        ],
        out_specs=pl.BlockSpec((None, tq, ATT_HEAD_DIM), lambda b, h, i: (b, i, h)),
        out_shape=jax.ShapeDtypeStruct((bsz, seq, ATT_WIDTH), BF16),
        compiler_params=_params("arbitrary", "arbitrary", "arbitrary"),
        name="attention",
    )(qkv, qkv, qkv, gain.reshape(1, ATT_WIDTH))


def _rwkv_kernel(pr_ref, pk_ref, pv_ref, pl_ref, mr_ref, mk_ref, mv_ref, ml_ref,
                 w2_ref, a2_ref, g2_ref, w0_ref, a0_ref, kk_ref, ka_ref, rk_ref, lnw_ref, lnb_ref,
                 o_ref, s_ref, prev_ref, prevl_ref, y_ref, *, ts):
    C = RWKV_CHUNK
    N = RWKV_HEAD_DIM
    n_chunks = ts // C

    @pl.when(pl.program_id(2) == 0)
    def _():
        s_ref[...] = jnp.zeros_like(s_ref)
        prev_ref[...] = jnp.zeros_like(prev_ref)
        prevl_ref[...] = jnp.zeros_like(prevl_ref)

    first_row = lax.broadcasted_iota(jnp.int32, (ts, 1), 0) == 0

    def shift(x, last_prev, mix):
        prev = jnp.where(first_row, last_prev, pltpu.roll(x, 1, 0))
        return x + (prev - x) * mix

    raw = [pr_ref[...], pk_ref[...], pv_ref[...]]
    raw_l = pl_ref[...]
    r_all = shift(raw[0], prev_ref[0:1, :], mr_ref[...])
    k_all = shift(raw[1], prev_ref[1:2, :], mk_ref[...])
    v_all = shift(raw[2], prev_ref[2:3, :], mv_ref[...])
    p_l = shift(raw_l, prevl_ref[...], ml_ref[...])
    for i in range(3):
        prev_ref[i:i + 1, :] = raw[i][ts - 1:ts, :]
    prevl_ref[...] = raw_l[ts - 1:ts, :]

    wd = o_ref.shape[-1]
    n_pairs = wd // LANES
    r128 = lax.broadcasted_iota(jnp.int32, (LANES, LANES), 0)
    c128 = lax.broadcasted_iota(jnp.int32, (LANES, LANES), 1)
    same_head = (r128 < N) == (c128 < N)
    rw = lax.broadcasted_iota(jnp.int32, (wd, wd), 0)
    cw = lax.broadcasted_iota(jnp.int32, (wd, wd), 1)
    head_ones = jnp.where((rw >> 6) == (cw >> 6), 1.0, 0.0).astype(BF16)
    head_ones2 = jnp.concatenate([head_ones, head_ones], axis=0)

    lw_pre = w0_ref[...] + _dot(jnp.tanh(p_l[:, :LANES]).astype(BF16), w2_ref[...])
    w_log = -(jnp.maximum(-lw_pre, 0.0) + jnp.log(1.0 + jnp.exp(-jnp.abs(lw_pre)))) - 0.5
    lw_all = -jnp.exp(w_log)
    rate = jax.nn.sigmoid(a0_ref[...] + _dot(p_l[:, :LANES].astype(BF16), a2_ref[...]))
    gate = _dot(jax.nn.sigmoid(p_l[:, LANES:]).astype(BF16), g2_ref[...])
    kk = k_all * kk_ref[...]
    norm = jnp.sqrt(_dot2(kk * kk, head_ones2))
    kk = kk / jnp.maximum(norm, 1e-12)
    k_all = k_all * (1.0 + (rate - 1.0) * ka_ref[...])
    a_all = -kk
    b_all = kk * rate

    row = lax.broadcasted_iota(jnp.int32, (C, LANES), 0)
    lane = lax.broadcasted_iota(jnp.int32, (C, LANES), 1)
    col = lane & (N - 1)
    head0 = lane < N
    strict = row > col
    incl = row >= col
    eye = jnp.where(row == col, 1.0, 0.0)
    blk16 = (row >> 4) == (col >> 4)
    blk32 = (row >> 5) == (col >> 5)
    tri2 = jnp.where(incl, 1.0, 0.0).astype(BF16)
    ones = jnp.ones((C, LANES), BF16)

    def bd(x):
        z = jnp.zeros_like(x)
        return jnp.concatenate([jnp.where(head0, x, z), jnp.where(head0, z, x)], axis=0)

    def mm3(x, y):
        xh, xl = _split(x)
        yh, yl = _split(y)
        byh, byl = bd(yh), bd(yl)
        rhs = jnp.concatenate([jnp.concatenate([byh, byl], axis=1),
                               jnp.concatenate([byh, jnp.zeros_like(byl)], axis=1)], axis=0)
        p = _dot(jnp.concatenate([xh, xl], axis=1), rhs)
        return p[:, :LANES] + p[:, LANES:]

    def mm1(x, y_bd):
        return _dot(x.astype(BF16), y_bd)

    cs = range(n_pairs * n_chunks)
    chunk = lambda x: [x[c * C:(c + 1) * C, j * LANES:(j + 1) * LANES]
                       for j in range(n_pairs) for c in range(n_chunks)]
    r, k, v, lw, a, b = (chunk(x) for x in (r_all, k_all, v_all, lw_all, a_all, b_all))
    lws = [_split(x) for x in lw]
    cum = [_dot(tri2, jnp.concatenate([h, l], axis=0)) for h, l in lws]
    wcol = [jnp.exp(_dot_tn(h, ones) + _dot_tn(l, ones)) for h, l in lws]
    e_neg = [jnp.exp(-x) for x in cum]
    e_rem = [jnp.exp(x[C - 1:C, :] - x) for x in cum]
    rt = [r[c] * jnp.exp(cum[c]) for c in cs]
    at = [a[c] * jnp.exp(cum[c] - lw[c]) for c in cs]
    kt = [k[c] * e_neg[c] for c in cs]
    bt = [b[c] * e_neg[c] for c in cs]
    kw = [k[c] * e_rem[c] for c in cs]
    bw = [b[c] * e_rem[c] for c in cs]

    at_b = [x.astype(BF16) for x in at]
    rt_b = [x.astype(BF16) for x in rt]
    lhs = [jnp.concatenate([at_b[c], rt_b[c]], axis=0) for c in cs]
    rhs = [jnp.concatenate([bd(bt[c].astype(BF16)), bd(kt[c].astype(BF16))], axis=0) for c in cs]
    aa = [_dot_nt(lhs[c], rhs[c]) for c in cs]
    L = LANES
    a_ab = [jnp.where(strict, x[:C, :L], 0.0) for x in aa]
    a_ak = [jnp.where(strict, x[:C, L:], 0.0) for x in aa]
    a_rb = [jnp.where(incl, x[C:, :L], 0.0) for x in aa]
    a_rk = [jnp.where(incl, x[C:, L:], 0.0) for x in aa]

    d = [jnp.where(blk16, x, 0.0) for x in a_ab]
    x = [eye + dd for dd in d]
    y = [mm3(dd, dd) for dd in d]
    for _ in range(2):
        x = [x[c] + mm3(x[c], y[c]) for c in cs]
        y = [mm3(yy, yy) for yy in y]
    x = [x[c] + mm3(x[c], y[c]) for c in cs]
    for in_block in (blk32, None):
        if in_block is blk32:
            off = [jnp.where(blk32 & jnp.logical_not(blk16), aa_, 0.0) for aa_ in a_ab]
        else:
            off = [jnp.where(blk32, 0.0, aa_) for aa_ in a_ab]
        xo = [mm1(x[c], bd(off[c].astype(BF16))) for c in cs]
        x = [x[c] + mm1(xo[c], bd(x[c].astype(BF16))) for c in cs]
    t_inv = x

    rhs = [jnp.concatenate([bd(at_b[c]), bd(a_ak[c].astype(BF16))], axis=1) for c in cs]
    tt = [mm1(t_inv[c], rhs[c]) for c in cs]
    ap = [x_[:, :L].astype(BF16) for x_ in tt]
    ta = [x_[:, L:] for x_ in tt]
    bv = [bd(x_.astype(BF16)) for x_ in v]
    u_loc = [mm1(ta[c], bv[c]).astype(BF16) for c in cs]
    y_loc = [mm1(a_rk[c], bv[c]) for c in cs]
    a_rb = [x_.astype(BF16) for x_ in a_rb]
    bw_b = [x_.astype(BF16) for x_ in bw]
    bkw = [jnp.concatenate([bw_b[c], kw[c].astype(BF16)], axis=0) for c in cs]
    p_f32 = [jnp.where(same_head, _dot_tn(bw_b[c], ap[c]), 0.0) for c in cs]
    p_mat = [x_.astype(BF16) for x_ in p_f32]
    g_mat = [jnp.where(same_head, _dot_tn(bkw[c], jnp.concatenate([u_loc[c], v[c].astype(BF16)], axis=0)),
                       0.0) for c in cs]
    q_mat = [(rt[c] + _dot(a_rb[c], bd(ap[c]))).astype(BF16) for c in cs]
    z_mat = [_dot(a_rb[c], bd(u_loc[c])) + y_loc[c] for c in cs]
    assert n_chunks % 2 == 0
    evens = [j * n_chunks + c for j in range(n_pairs) for c in range(0, n_chunks, 2)]
    pg = {i: _dot(p_mat[i + 1], jnp.concatenate([p_mat[i], g_mat[i].astype(BF16)], axis=1)) for i in evens}
    w_row = {i: jnp.exp(cum[i][C - 1:C, :]) for i in evens}
    p2 = {i: (wcol[i + 1] * p_f32[i] + p_f32[i + 1] * w_row[i] + pg[i][:, :L]).astype(BF16) for i in evens}
    g2 = {i: wcol[i + 1] * g_mat[i] + pg[i][:, L:] + g_mat[i + 1] for i in evens}
    w2 = {i: wcol[i] * wcol[i + 1] for i in evens}

    states = [s_ref[j] for j in range(n_pairs)]
    entry = [None] * len(cs)
    odd_entry = [None] * len(cs)
    for c in range(0, n_chunks, 2):
        for j in range(n_pairs):
            i = j * n_chunks + c
            s_b = states[j].astype(BF16)
            entry[i] = s_b
            odd_entry[i + 1] = (states[j], s_b)
            states[j] = states[j] * w2[i] + _dot(p2[i], s_b) + g2[i]
    for j in range(n_pairs):
        s_ref[j] = states[j]
    for i in evens:
        s0, s0_b = odd_entry[i + 1]
        entry[i + 1] = (s0 * wcol[i] + _dot(p_mat[i], s0_b) + g_mat[i]).astype(BF16)
    for j in range(n_pairs):
        for c in range(n_chunks):
            i = j * n_chunks + c
            y_ref[c * C:(c + 1) * C, j * LANES:(j + 1) * LANES] = _dot(q_mat[i], entry[i]) + z_mat[i]

    inv_n = 1.0 / N
    y_all = y_ref[...]
    mu = _dot2(y_all, head_ones2) * inv_n
    dev = y_all - mu
    var = _dot2(dev * dev, head_ones2) * inv_n
    yn = dev * lax.rsqrt(var + RWKV_LN_EPS) * lnw_ref[...] + lnb_ref[...]
    bonus = _dot2(r_all * k_all * rk_ref[...], head_ones2) * v_all
    o_ref[...] = ((yn + bonus) * gate).astype(o_ref.dtype)


def _rwkv(p_rkv, p_lora, mix_rkv, mix_lora, w2p, a2p, g2p, w0, a0, k_k, k_a, r_k, ln_w, ln_b,
          ts=512, pairs_per_step=2):
    bsz, seq, _ = p_rkv.shape
    W = RWKV_WIDTH
    nl = p_lora.shape[-1]
    wd = pairs_per_step * LANES
    npair = W // wd
    col = lambda j: pl.BlockSpec((None, ts, wd), lambda bb, p, t, j=j: (bb, t, j * npair + p))
    vec_col = lambda j: pl.BlockSpec((1, wd), lambda bb, p, t, j=j: (0, j * npair + p))
    vec = pl.BlockSpec((1, wd), lambda bb, p, t: (0, p))
    mat = lambda rows: pl.BlockSpec((rows, wd), lambda bb, p, t: (0, p))
    kern = functools.partial(_rwkv_kernel, ts=ts)
    return pl.pallas_call(
        kern,
        grid=(bsz, npair, seq // ts),
        in_specs=[
            col(0), col(1), col(2),
            pl.BlockSpec((None, ts, nl), lambda bb, p, t: (bb, t, 0)),
            vec_col(0), vec_col(1), vec_col(2),
            pl.BlockSpec((1, nl), lambda bb, p, t: (0, 0)),
            mat(LANES), mat(LANES), mat(nl - LANES),
            vec, vec, vec, vec, vec, vec, vec,
        ],
        out_specs=pl.BlockSpec((None, ts, wd), lambda bb, p, t: (bb, t, p)),
        out_shape=jax.ShapeDtypeStruct((bsz, seq, W), BF16),
        scratch_shapes=[
            pltpu.VMEM((pairs_per_step, LANES, LANES), F32),
            pltpu.VMEM((8, wd), F32),
            pltpu.VMEM((1, nl), F32),
            pltpu.VMEM((ts, wd), F32),
        ],
        compiler_params=_params("arbitrary", "arbitrary", "arbitrary"),
        name="rwkv",
    )(p_rkv, p_rkv, p_rkv, p_lora, mix_rkv, mix_rkv, mix_rkv, mix_lora,
      w2p, a2p, g2p, w0, a0, k_k, k_a, r_k, ln_w, ln_b)


def _out_proj_kernel(oa_ref, or_ref, x_ref, w_ref, gpost_ref, gt_ref, gpre_ref, sc_ref, sh_ref,
                     wr_ref, br_ref, x1_ref, h2_ref, meta_ref, cnt_ref, run_ref):
    tm = x_ref.shape[0]
    half = oa_ref.shape[-1]

    @pl.when((pl.program_id(0) == 0) & (pl.program_id(1) == 0))
    def _():
        run_ref[...] = jnp.zeros_like(run_ref)

    y = _dot(oa_ref[...], w_ref[:half, :]) + _dot(or_ref[...], w_ref[half:, :])
    ms = jnp.mean(y * y, axis=-1, keepdims=True)
    x1 = x_ref[...] + gt_ref[...] * (y * lax.rsqrt(ms + RMS_EPS) * gpost_ref[...])
    x1_ref[...] = x1
    ms1 = jnp.mean(x1 * x1, axis=-1, keepdims=True)
    h2 = (x1 * lax.rsqrt(ms1 + RMS_EPS) * gpre_ref[...]) * (1.0 + sc_ref[...]) + sh_ref[...]
    h2_ref[...] = _pack_halves(h2)

    lg = _dot3(h2, wr_ref[...]) + br_ref[...]
    lane_i = lax.broadcasted_iota(jnp.int32, lg.shape, 1)
    lane = lane_i.astype(F32)
    lane_group = ((lane_i - N_GROUPS) >> 3).astype(F32)
    first = lambda mask: jnp.min(jnp.where(mask, lane, 4.0 * LANES), axis=-1, keepdims=True)
    gl = jnp.where(lane_i < N_GROUPS, lg, NEG_BIG)
    gmax = jnp.max(gl, axis=-1, keepdims=True)
    g_sel = first(gl == gmax)
    p_group = 1.0 / jnp.sum(jnp.exp(gl - gmax), axis=-1, keepdims=True)
    in_group = (lane_i >= N_GROUPS) & (lane_group == g_sel)
    el = jnp.where(in_group, lg, NEG_BIG)
    m0 = jnp.max(el, axis=-1, keepdims=True)
    i0 = first(el == m0)
    el = jnp.where(lane == i0, NEG_BIG, el)
    m1 = jnp.max(el, axis=-1, keepdims=True)
    i1 = first(el == m1)
    t = jnp.exp(m1 - m0)
    gate0 = p_group / (1.0 + t)
    gate1 = p_group * t / (1.0 + t)
    e0 = i0 - N_GROUPS
    e1 = i1 - N_GROUPS
    hit = (lane == e0) | (lane == e1)
    onehot = jnp.where(hit, 1.0, 0.0).astype(BF16)
    rr = lax.broadcasted_iota(jnp.int32, (tm, tm), 0)
    cc = lax.broadcasted_iota(jnp.int32, (tm, tm), 1)
    before = jnp.where(cc < rr, 1.0, 0.0).astype(BF16)
    seen = _dot(before, onehot) + run_ref[...]
    rank0 = jnp.sum(jnp.where(lane == e0, seen, 0.0), axis=-1, keepdims=True)
    rank1 = jnp.sum(jnp.where(lane == e1, seen, 0.0), axis=-1, keepdims=True)
    total = seen[tm - 1:tm, :] + jnp.where(hit[tm - 1:tm, :], 1.0, 0.0)
    run_ref[...] = total
    cnt_ref[...] = total
    meta = jnp.zeros(lg.shape, F32)
    for idx, val in ((META_E0, e0.astype(F32)), (META_E1, e1.astype(F32)), (META_RANK0, rank0),
                     (META_RANK1, rank1), (META_GATE0, gate0), (META_GATE1, gate1)):
        meta = jnp.where(lane == idx, val, meta)
    meta_ref[...] = meta


def _out_proj(o_att, o_rwkv, x, w_out, g_post, gt, g_pre, sc, sh, w_router, b_router, tm=512):
    bsz, seq, d = x.shape
    half = o_att.shape[-1]
    const = lambda b, i: (0, 0)
    row = lambda b, i: (b, i, 0)
    per_b = lambda b, i: (b, 0, 0)
    vec = pl.BlockSpec((1, d), const)
    mod = pl.BlockSpec((None, 1, d), per_b)
    return pl.pallas_call(
        _out_proj_kernel,
        grid=(bsz, seq // tm),
        in_specs=[
            pl.BlockSpec((None, tm, half), row),
            pl.BlockSpec((None, tm, half), row),
            pl.BlockSpec((None, tm, d), row),
            pl.BlockSpec((d, d), const, pipeline_mode=pl.Buffered(1)),
            vec, mod, vec, mod, mod,
            pl.BlockSpec((d, LANES), const),
            pl.BlockSpec((1, LANES), const),
        ],
        out_specs=[
            pl.BlockSpec((None, tm, d), row),
            pl.BlockSpec((None, tm, d // 2), row),
            pl.BlockSpec((None, tm, LANES), row),
            pl.BlockSpec((1, LANES), const),
        ],
        out_shape=[
            jax.ShapeDtypeStruct((bsz, seq, d), F32),
            jax.ShapeDtypeStruct((bsz, seq, d // 2), U32),
            jax.ShapeDtypeStruct((bsz, seq, LANES), F32),
            jax.ShapeDtypeStruct((1, LANES), F32),
        ],
        scratch_shapes=[pltpu.VMEM((1, LANES), F32)],
        compiler_params=_params("arbitrary", "arbitrary"),
        name="out_proj",
    )(o_att, o_rwkv, x, w_out, g_post.reshape(1, d), gt, g_pre.reshape(1, d), sc, sh, w_router, b_router)


def _pack_halves(x):
    d2 = x.shape[-1] // 2
    lo_bits = lax.bitcast_convert_type(x[:, :d2].astype(BF16).astype(F32), U32)
    hi_bits = lax.bitcast_convert_type(x[:, d2:].astype(BF16).astype(F32), U32)
    return (lo_bits >> 16) | hi_bits


def _unpack_halves(words):
    lo = lax.bitcast_convert_type(words << 16, F32)
    hi = lax.bitcast_convert_type(words & jnp.uint32(0xFFFF0000), F32)
    return lo, hi


def _expert_kernel(te_ref, nv_ref, n1_ref, n2_ref, sl_ref, x_ref, wg_hbm, wu_hbm, wd_hbm, o_ref,
                   wg_f, wu_f, wd_f, wg_b, wu_b, wd_b, sem):
    i = pl.program_id(0)
    valid = i < nv_ref[0]
    prev = te_ref[jnp.maximum(i - 1, 0)]
    new_expert = (i == 0) | (te_ref[i] != prev)

    def weight_copies(e, slot):
        return (pltpu.make_async_copy(wg_hbm.at[e], wg_f.at[slot], sem.at[slot, 0]),
                pltpu.make_async_copy(wu_hbm.at[e], wu_f.at[slot], sem.at[slot, 1]),
                pltpu.make_async_copy(wd_hbm.at[e], wd_f.at[slot], sem.at[slot, 2]))

    def next_slot(slot, k):
        s = slot + k
        return jnp.where(s >= WEIGHT_SLOTS, s - WEIGHT_SLOTS, s)

    @pl.when(valid & (i == 0))
    def _():
        for cp in weight_copies(te_ref[0], sl_ref[0]):
            cp.start()

        @pl.when(n1_ref[0] >= 0)
        def _():
            for cp in weight_copies(n1_ref[0], next_slot(sl_ref[0], 1)):
                cp.start()

    @pl.when(valid & new_expert)
    def _():
        slot = sl_ref[i]
        for cp in weight_copies(te_ref[i], slot):
            cp.wait()

        @pl.when(n2_ref[i] >= 0)
        def _():
            for cp in weight_copies(n2_ref[i], next_slot(slot, 2)):
                cp.start()

        wg_b[...] = wg_f[slot].astype(BF16)
        wu_b[...] = wu_f[slot].astype(BF16)
        wd_b[...] = wd_f[slot].astype(BF16)

    @pl.when(valid)
    def _():
        x_lo, x_hi = _unpack_halves(x_ref[...])
        d2 = x_lo.shape[-1]
        x_lo, x_hi = x_lo.astype(BF16), x_hi.astype(BF16)
        g = _dot(x_lo, wg_b[:d2, :]) + _dot(x_hi, wg_b[d2:, :])
        u = _dot(x_lo, wu_b[:d2, :]) + _dot(x_hi, wu_b[d2:, :])
        hid = (g * jax.nn.sigmoid(g) * u).astype(BF16)
        o_ref[...] = _pack_halves(_dot(hid, wd_b[...]))

    @pl.when(jnp.logical_not(valid))
    def _():
        o_ref[...] = jnp.zeros_like(o_ref)


def _experts(xs, tile_expert, n_valid, next1, next2, slot, w_gate, w_up, w_down, tm):
    m_pad, d2 = xs.shape
    d = 2 * d2
    de = w_gate.shape[-1]
    n_tiles = m_pad // tm
    grid_spec = pltpu.PrefetchScalarGridSpec(
        num_scalar_prefetch=5,
        grid=(n_tiles,),
        in_specs=[
            pl.BlockSpec((tm, d2), lambda i, *_: (i, 0)),
            pl.BlockSpec(memory_space=pl.ANY),
            pl.BlockSpec(memory_space=pl.ANY),
            pl.BlockSpec(memory_space=pl.ANY),
        ],
        out_specs=pl.BlockSpec((tm, d2), lambda i, *_: (i, 0)),
        scratch_shapes=[
            pltpu.VMEM((WEIGHT_SLOTS, d, de), F32),
            pltpu.VMEM((WEIGHT_SLOTS, d, de), F32),
            pltpu.VMEM((WEIGHT_SLOTS, de, d), F32),
            pltpu.VMEM((d, de), BF16),
            pltpu.VMEM((d, de), BF16),
            pltpu.VMEM((de, d), BF16),
            pltpu.SemaphoreType.DMA((WEIGHT_SLOTS, 3)),
        ],
    )
    return pl.pallas_call(
        _expert_kernel,
        grid_spec=grid_spec,
        out_shape=jax.ShapeDtypeStruct((m_pad, d2), U32),
        compiler_params=_params("arbitrary"),
        name="experts",
    )(tile_expert, n_valid, next1, next2, slot, xs, w_gate, w_up, w_down)


def _final_kernel(x_ref, y0_ref, y1_ref, meta_ref, g_ref, gt_ref, o_ref):
    meta = meta_ref[...]
    gate0 = meta[:, META_GATE0:META_GATE0 + 1]
    gate1 = meta[:, META_GATE1:META_GATE1 + 1]
    d2 = y0_ref.shape[-1]
    halves = [a * gate0 + b * gate1 for a, b in zip(_unpack_halves(y0_ref[...]), _unpack_halves(y1_ref[...]))]
    ssq = sum(jnp.sum(h * h, axis=-1, keepdims=True) for h in halves)
    inv = lax.rsqrt(ssq * (1.0 / (2 * d2)) + RMS_EPS)
    for j, h in enumerate(halves):
        cols = slice(j * d2, (j + 1) * d2)
        o_ref[:, cols] = x_ref[:, cols] + gt_ref[:, cols] * (h * inv * g_ref[:, cols])


def _final(x1, y0, y1, meta, g, gt, tm=512):
    bsz, seq, d = x1.shape
    row = lambda b, i: (b, i, 0)
    blk = pl.BlockSpec((None, tm, d), row)
    packed = pl.BlockSpec((None, tm, d // 2), row)
    return pl.pallas_call(
        _final_kernel,
        grid=(bsz, seq // tm),
        in_specs=[blk, packed, packed,
                  pl.BlockSpec((None, tm, LANES), row),
                  pl.BlockSpec((1, d), lambda b, i: (0, 0)),
                  pl.BlockSpec((None, 1, d), lambda b, i: (b, 0, 0))],
        out_specs=blk,
        out_shape=jax.ShapeDtypeStruct((bsz, seq, d), F32),
        compiler_params=_params("arbitrary", "arbitrary"),
        name="final",
    )(x1, y0, y1, meta, g.reshape(1, d), gt)


def _dispatch(meta, counts, tm):
    n_tok = meta.shape[0]
    n_rows = n_tok * TOP_K
    n_tiles = n_rows // tm + N_EXPERTS
    expert = meta[:, META_E0:META_E1 + 1].astype(jnp.int32)
    rank = meta[:, META_RANK0:META_RANK1 + 1].astype(jnp.int32)
    counts = counts[0, :N_EXPERTS].astype(jnp.int32)
    tiles_e = (counts + tm - 1) // tm
    tile_end = jnp.cumsum(tiles_e)
    pad_start = (tile_end - tiles_e) * tm
    slot_pos = pad_start[expert] + rank
    n_valid = tile_end[-1]
    t_idx = jnp.arange(n_tiles, dtype=jnp.int32)
    tile_expert = jnp.searchsorted(tile_end, jnp.minimum(t_idx, n_valid - 1), side='right')
    tile_expert = jnp.minimum(tile_expert, N_EXPERTS - 1).astype(jnp.int32)
    tok = jnp.broadcast_to(jnp.arange(n_tok, dtype=jnp.int32)[:, None], (n_tok, TOP_K))
    filler = jnp.arange(n_tiles * tm, dtype=jnp.int32) % n_tok
    src_tok = filler.at[slot_pos.reshape(-1)].set(tok.reshape(-1))
    ids = jnp.arange(N_EXPERTS, dtype=jnp.int32)
    used = tiles_e > 0
    first_used_from = lax.cummin(jnp.where(used, ids, N_EXPERTS), reverse=True)
    next_used = jnp.concatenate([first_used_from[1:], jnp.full((1,), N_EXPERTS, jnp.int32)])
    next_used = jnp.where(next_used >= N_EXPERTS, -1, next_used)
    next2_used = jnp.where(next_used >= 0, next_used[jnp.maximum(next_used, 0)], -1)
    run_slot = (jnp.cumsum(used.astype(jnp.int32)) - 1) % WEIGHT_SLOTS
    per_tile = lambda table: table[tile_expert].astype(jnp.int32)
    return (src_tok, slot_pos, tile_expert, n_valid.reshape(1).astype(jnp.int32),
            per_tile(next_used), per_tile(next2_used), per_tile(run_slot))


def _pad_rows(w, rows_before, rows_total):
    return jnp.pad(w, ((rows_before, rows_total - rows_before - w.shape[0]), (0, 0))).astype(BF16)


def kernel(x, c, w_ada, b_ada, g_pre_mix, g_post_mix, g_pre_ffn, g_post_ffn, w_in, w_out,
           attn_out_gain, rwkv_shift_mix, rwkv_w0, rwkv_w2, rwkv_a0, rwkv_a2, rwkv_g2,
           rwkv_k_k, rwkv_k_a, rwkv_r_k, rwkv_ln_w, rwkv_ln_b, router_group_w, router_group_b,
           router_expert_w, router_expert_b, expert_w_gate, expert_w_up, expert_w_down):
    bsz, seq, d = x.shape
    depth = w_ada.shape[0]
    tm_moe = 256
    W = RWKV_WIDTH
    for l in range(depth):
        mod = _adaln(c, w_ada[l], b_ada[l])
        sh_m, sc_m, gt_m, sh_f, sc_f, gt_f = jnp.split(mod[:, None, :], N_MOD, axis=-1)

        w = w_in[l]
        n_att = 3 * ATT_WIDTH
        n_rkv = 3 * W
        n_w, n_a, n_g = rwkv_w2.shape[1], rwkv_a2.shape[1], rwkv_g2.shape[1]
        assert n_w + n_a == LANES and w.shape[1] == n_att + n_rkv + n_w + n_a + n_g
        n_lora = LANES + -(-n_g // LANES) * LANES
        pad_l = n_lora - (n_w + n_a + n_g)
        w_all = jnp.pad(w, ((0, 0), (0, pad_l))).astype(BF16)
        qkv, p_rkv, p_lora = _in_proj(x, g_pre_mix[l], sc_m, sh_m, w_all, n_att, n_rkv)
        o_att = _attention(qkv, attn_out_gain[l])
        mix = rwkv_shift_mix[l]
        row = lambda t: t.reshape(1, -1)
        o_rwkv = _rwkv(
            p_rkv, p_lora, row(mix[:n_rkv]), row(jnp.pad(mix[n_rkv:], (0, pad_l))),
            _pad_rows(rwkv_w2[l], 0, LANES), _pad_rows(rwkv_a2[l], n_w, LANES),
            _pad_rows(rwkv_g2[l], 0, n_lora - LANES),
            row(rwkv_w0[l]), row(rwkv_a0[l]), row(rwkv_k_k[l]), row(rwkv_k_a[l]), row(rwkv_r_k[l]),
            row(rwkv_ln_w[l]), row(rwkv_ln_b[l]))

        n_route = N_GROUPS + N_EXPERTS
        w_router = jnp.concatenate([router_group_w[l], router_expert_w[l]], axis=1)
        w_router = jnp.pad(w_router, ((0, 0), (0, LANES - n_route)))
        b_router = jnp.pad(jnp.concatenate([router_group_b[l], router_expert_b[l]]), (0, LANES - n_route))
        x1, h2w, meta, counts = _out_proj(o_att, o_rwkv, x, w_out[l].astype(BF16), g_post_mix[l], gt_m,
                                          g_pre_ffn[l], sc_f, sh_f, w_router, b_router.reshape(1, LANES))

        src_tok, slot_pos, tile_expert, n_valid, next1, next2, slot = _dispatch(
            meta.reshape(bsz * seq, LANES), counts, tm_moe)
        xs = h2w.reshape(bsz * seq, d // 2)[src_tok]
        rows = _experts(xs, tile_expert, n_valid, next1, next2, slot,
                        expert_w_gate[l], expert_w_up[l], expert_w_down[l], tm_moe)
        y0 = rows[slot_pos[:, 0]].reshape(bsz, seq, d // 2)
        y1 = rows[slot_pos[:, 1]].reshape(bsz, seq, d // 2)
        x = _final(x1, y0, y1, meta, g_post_ffn[l], gt_f)
    return x
```

```python
import functools

import jax
import jax.numpy as jnp
from jax import lax
from jax.experimental import pallas as pl
from jax.experimental.pallas import tpu as pltpu

F32 = jnp.float32
BF16 = jnp.bfloat16
U32 = jnp.uint32

LANES = 128
VMEM_LIMIT_BYTES = 56 * 1024 * 1024

ATT_HEADS = 8
ATT_HEAD_DIM = 128
ATT_WIDTH = ATT_HEADS * ATT_HEAD_DIM
RWKV_HEAD_DIM = 64
RWKV_WIDTH = 1024
RWKV_CHUNK = 64
RWKV_LN_EPS = 64e-5
RMS_EPS = 1e-6
LOG2_E = 1.4426950408889634
EXP2_CLAMP = 126.0
N_GROUPS = 8
EXPERTS_PER_GROUP = 8
N_EXPERTS = 64
TOP_K = 2
N_MOD = 6
NEG_BIG = -1e30
WEIGHT_SLOTS = 3
META_E0, META_E1, META_RANK0, META_RANK1, META_GATE0, META_GATE1 = range(6)


def _dot(a, b):
    return jnp.dot(a, b, preferred_element_type=F32)


def _dot_nt(a, b):
    return lax.dot_general(a, b, (((1,), (1,)), ((), ())), preferred_element_type=F32)


def _dot_tn(a, b):
    return lax.dot_general(a, b, (((0,), (0,)), ((), ())), preferred_element_type=F32)


def _split(x):
    hi = x.astype(BF16)
    lo = (x - hi.astype(F32)).astype(BF16)
    return hi, lo


def _dot3(a, b):
    ah, al = _split(a)
    bh, bl = _split(b)
    return _dot(ah, bh) + _dot(ah, bl) + _dot(al, bh)


def _dot2(x, m2):
    hi, lo = _split(x)
    return _dot(jnp.concatenate([hi, lo], axis=1), m2)


def _params(*sem):
    return pltpu.CompilerParams(dimension_semantics=sem, vmem_limit_bytes=VMEM_LIMIT_BYTES)


def _adaln_kernel(c_ref, w_ref, b_ref, o_ref):
    c = c_ref[...]
    s = (c * jax.nn.sigmoid(c)).astype(BF16)
    o_ref[...] = _dot(s, w_ref[...].astype(BF16)) + b_ref[...]


def _adaln(c, w, b):
    bsz, d = c.shape
    n = w.shape[1]
    rows = 8
    tn = 1536
    cp = jnp.zeros((rows, d), F32).at[:bsz].set(c)
    out = pl.pallas_call(
        _adaln_kernel,
        grid=(n // tn,),
        in_specs=[
            pl.BlockSpec((rows, d), lambda j: (0, 0)),
            pl.BlockSpec((d, tn), lambda j: (0, j)),
            pl.BlockSpec((1, tn), lambda j: (0, j)),
        ],
        out_specs=pl.BlockSpec((rows, tn), lambda j: (0, j)),
        out_shape=jax.ShapeDtypeStruct((rows, n), F32),
        compiler_params=_params("arbitrary"),
        name="adaln",
    )(cp, w, b.reshape(1, n))
    return out[:bsz]


def _in_proj_kernel(x_ref, g_ref, sc_ref, sh_ref, w_ref, qs_ref, oa_ref, or_ref, ol_ref):
    x = x_ref[...]
    ms = jnp.mean(x * x, axis=-1, keepdims=True)
    y = x * lax.rsqrt(ms + RMS_EPS) * g_ref[...]
    h = (y * (1.0 + sc_ref[...]) + sh_ref[...]).astype(BF16)
    na, nr = oa_ref.shape[-1], or_ref.shape[-1]
    oa_ref[...] = (_dot(h, w_ref[:, :na]) * qs_ref[...]).astype(oa_ref.dtype)
    or_ref[...] = _dot(h, w_ref[:, na:na + nr])
    ol_ref[...] = _dot(h, w_ref[:, na + nr:])


def _in_proj(x, g, sc, sh, w, na, nr, tm=256):
    bsz, seq, d = x.shape
    nl = w.shape[1] - na - nr
    q_scale = jnp.where(jnp.arange(na) < ATT_WIDTH, LOG2_E * ATT_HEAD_DIM ** -0.5, 1.0).astype(F32).reshape(1, na)
    const = lambda b, i: (0, 0)
    row = lambda b, i: (b, i, 0)
    per_b = lambda b, i: (b, 0, 0)
    return pl.pallas_call(
        _in_proj_kernel,
        grid=(bsz, seq // tm),
        in_specs=[
            pl.BlockSpec((None, tm, d), row),
            pl.BlockSpec((1, d), const),
            pl.BlockSpec((None, 1, d), per_b),
            pl.BlockSpec((None, 1, d), per_b),
            pl.BlockSpec((d, na + nr + nl), const, pipeline_mode=pl.Buffered(1)),
            pl.BlockSpec((1, na), const),
        ],
        out_specs=[
            pl.BlockSpec((None, tm, na), row),
            pl.BlockSpec((None, tm, nr), row),
            pl.BlockSpec((None, tm, nl), row),
        ],
        out_shape=[
            jax.ShapeDtypeStruct((bsz, seq, na), BF16),
            jax.ShapeDtypeStruct((bsz, seq, nr), F32),
            jax.ShapeDtypeStruct((bsz, seq, nl), F32),
        ],
        compiler_params=_params("arbitrary", "arbitrary"),
        name="in_proj",
    )(x, g.reshape(1, d), sc, sh, w, q_scale)


def _att_kernel(q_ref, k_ref, v_ref, gain_ref, o_ref, *, tq, tk):
    sub = LANES
    n_sub = tk // sub
    qi = pl.program_id(2)
    q = q_ref[...]
    r2 = lax.broadcasted_iota(jnp.int32, (2 * sub, 2 * sub), 0) & (sub - 1)
    c2 = lax.broadcasted_iota(jnp.int32, (2 * sub, 2 * sub), 1)
    tri_ones = jnp.where((c2 >= sub) | (r2 >= c2), 1.0, 0.0).astype(BF16)

    lower = (lax.broadcasted_iota(jnp.int32, (sub, sub), 1)
             < lax.broadcasted_iota(jnp.int32, (sub, sub), 0))

    def sweep(items):
        n = [it[1].shape[0] // sub for it in items]
        tiles = lambda x, m: [x[:, u * sub:(u + 1) * sub] for u in range(m)]
        z = [_dot_nt(it[0], it[1]) for it in items]
        sp = [jnp.maximum(jnp.log2(1.0 + jnp.exp2(jnp.minimum(x, EXP2_CLAMP))), x) for x in z]
        z = [tiles(x, m) for x, m in zip(z, n)]
        sp = [tiles(x, m) for x, m in zip(sp, n)]
        for i, it in enumerate(items):
            if it[5]:
                sp[i][-1] = jnp.where(lower, sp[i][-1], 0.0)
        cs = [[_dot(jnp.concatenate(_split(x), axis=1), tri_ones) for x in row] for row in sp]
        out = []
        for i, it in enumerate(items):
            carry, acc = (it[3], it[4]) if it[3] is not None else out[-1]
            w = [None] * n[i]
            for u in reversed(range(n[i])):
                w[u] = jnp.exp2(z[i][u] - (cs[i][u][:, :sub] + carry))
                carry = carry + cs[i][u][:, sub:]
            if it[5]:
                w[-1] = jnp.where(lower, w[-1], 0.0)
            acc = acc + _dot(jnp.concatenate(w, axis=1).astype(BF16), it[2])
            out.append((carry, acc))
        return out

    dh = ATT_HEAD_DIM
    heads = range(q.shape[1] // dh)
    hcols = lambda h: slice(h * dh, (h + 1) * dh)

    assert tq == tk
    start = pl.multiple_of(qi * tk, tk)
    zeros = jnp.zeros((sub, sub), F32)
    diag = sweep([(q[r * sub:(r + 1) * sub, hcols(h)], k_ref[pl.ds(start, (r + 1) * sub), hcols(h)],
                   v_ref[pl.ds(start, (r + 1) * sub), hcols(h)], zeros, zeros, True)
                  for h in heads for r in range(n_sub)])
    state = []
    for h in heads:
        blocks = diag[h * n_sub:(h + 1) * n_sub]
        state += [jnp.concatenate([c for c, _ in blocks], axis=0), jnp.concatenate([a for _, a in blocks], axis=0)]

    def keys(j, h):
        st = pl.multiple_of(j * tk, tk)
        return k_ref[pl.ds(st, tk), hcols(h)], v_ref[pl.ds(st, tk), hcols(h)]

    def flatten(results):
        return tuple(x for pair in results for x in pair)

    def two_tiles(i, state):
        j = qi - 1 - 2 * i
        items = []
        for h in heads:
            items += [(q[:, hcols(h)], *keys(j, h), state[2 * h], state[2 * h + 1], False),
                      (q[:, hcols(h)], *keys(j - 1, h), None, None, False)]
        return flatten(sweep(items)[1::2])

    def one_tile(i, state):
        return flatten(sweep([(q[:, hcols(h)], *keys(0, h), state[2 * h], state[2 * h + 1], False)
                              for h in heads]))

    state = lax.fori_loop(0, lax.shift_right_logical(qi, 1), two_tiles, tuple(state))
    state = lax.fori_loop(0, qi & 1, one_tile, state)
    for h in heads:
        acc = state[2 * h + 1]
        ms = jnp.mean(acc * acc, axis=-1, keepdims=True)
        o_ref[:, hcols(h)] = (acc * lax.rsqrt(ms + RMS_EPS) * gain_ref[:, hcols(h)]).astype(o_ref.dtype)


def _attention(qkv, gain, tq=512, tk=512, heads_per_step=2):
    bsz, seq, _ = qkv.shape
    ng = ATT_HEADS // heads_per_step
    wd = heads_per_step * ATT_HEAD_DIM
    assert tk % tq == 0 and seq % tk == 0
    kern = functools.partial(_att_kernel, tq=tq, tk=tk)
    return pl.pallas_call(
        kern,
        grid=(bsz, ng, seq // tq),
        in_specs=[
            pl.BlockSpec((None, tq, wd), lambda b, h, i: (b, i, h)),
            pl.BlockSpec((None, seq, wd), lambda b, h, i: (b, 0, ng + h)),
            pl.BlockSpec((None, seq, wd), lambda b, h, i: (b, 0, 2 * ng + h)),
            pl.BlockSpec((1, wd), lambda b, h, i: (0, h)),
        ],
        out_specs=pl.BlockSpec((None, tq, wd), lambda b, h, i: (b, i, h)),
        out_shape=jax.ShapeDtypeStruct((bsz, seq, ATT_WIDTH), BF16),
        compiler_params=_params("arbitrary", "arbitrary", "arbitrary"),
        name="attention",
    )(qkv, qkv, qkv, gain.reshape(1, ATT_WIDTH))


def _rwkv_kernel(pr_ref, pk_ref, pv_ref, pl_ref, mr_ref, mk_ref, mv_ref, ml_ref,
                 w2_ref, a2_ref, g2_ref, w0_ref, a0_ref, kk_ref, ka_ref, rk_ref, lnw_ref, lnb_ref,
                 o_ref, s_ref, prev_ref, prevl_ref, y_ref, *, ts):
    C = RWKV_CHUNK
    N = RWKV_HEAD_DIM
    n_chunks = ts // C

    @pl.when(pl.program_id(2) == 0)
    def _():
        s_ref[...] = jnp.zeros_like(s_ref)
        prev_ref[...] = jnp.zeros_like(prev_ref)
        prevl_ref[...] = jnp.zeros_like(prevl_ref)

    first_row = lax.broadcasted_iota(jnp.int32, (ts, 1), 0) == 0

    def shift(x, last_prev, mix):
        prev = jnp.where(first_row, last_prev, pltpu.roll(x, 1, 0))
        return x + (prev - x) * mix

    raw = [pr_ref[...], pk_ref[...], pv_ref[...]]
    raw_l = pl_ref[...]
    r_all = shift(raw[0], prev_ref[0:1, :], mr_ref[...])
    k_all = shift(raw[1], prev_ref[1:2, :], mk_ref[...])
    v_all = shift(raw[2], prev_ref[2:3, :], mv_ref[...])
    p_l = shift(raw_l, prevl_ref[...], ml_ref[...])
    for i in range(3):
        prev_ref[i:i + 1, :] = raw[i][ts - 1:ts, :]
    prevl_ref[...] = raw_l[ts - 1:ts, :]

    wd = o_ref.shape[-1]
    n_pairs = wd // LANES
    r128 = lax.broadcasted_iota(jnp.int32, (LANES, LANES), 0)
    c128 = lax.broadcasted_iota(jnp.int32, (LANES, LANES), 1)
    same_head = (r128 < N) == (c128 < N)
    rw = lax.broadcasted_iota(jnp.int32, (wd, wd), 0)
    cw = lax.broadcasted_iota(jnp.int32, (wd, wd), 1)
    head_ones = jnp.where((rw >> 6) == (cw >> 6), 1.0, 0.0).astype(BF16)
    head_ones2 = jnp.concatenate([head_ones, head_ones], axis=0)

    lw_pre = w0_ref[...] + _dot(jnp.tanh(p_l[:, :LANES]).astype(BF16), w2_ref[...])
    w_log = -(jnp.maximum(-lw_pre, 0.0) + jnp.log(1.0 + jnp.exp(-jnp.abs(lw_pre)))) - 0.5
    lw_all = -jnp.exp(w_log)
    rate = jax.nn.sigmoid(a0_ref[...] + _dot(p_l[:, :LANES].astype(BF16), a2_ref[...]))
    gate = _dot(jax.nn.sigmoid(p_l[:, LANES:]).astype(BF16), g2_ref[...])
    kk = k_all * kk_ref[...]
    norm = jnp.sqrt(_dot2(kk * kk, head_ones2))
    kk = kk / jnp.maximum(norm, 1e-12)
    k_all = k_all * (1.0 + (rate - 1.0) * ka_ref[...])
    a_all = -kk
    b_all = kk * rate

    row = lax.broadcasted_iota(jnp.int32, (C, LANES), 0)
    lane = lax.broadcasted_iota(jnp.int32, (C, LANES), 1)
    col = lane & (N - 1)
    head0 = lane < N
    strict = row > col
    incl = row >= col
    eye = jnp.where(row == col, 1.0, 0.0)
    blk16 = (row >> 4) == (col >> 4)
    blk32 = (row >> 5) == (col >> 5)
    tri2 = jnp.where(incl, 1.0, 0.0).astype(BF16)
    ones = jnp.ones((C, LANES), BF16)

    def bd(x):
        z = jnp.zeros_like(x)
        return jnp.concatenate([jnp.where(head0, x, z), jnp.where(head0, z, x)], axis=0)

    def mm3(x, y):
        xh, xl = _split(x)
        yh, yl = _split(y)
        byh, byl = bd(yh), bd(yl)
        rhs = jnp.concatenate([jnp.concatenate([byh, byl], axis=1),
                               jnp.concatenate([byh, jnp.zeros_like(byl)], axis=1)], axis=0)
        p = _dot(jnp.concatenate([xh, xl], axis=1), rhs)
        return p[:, :LANES] + p[:, LANES:]

    def mm1(x, y_bd):
        return _dot(x.astype(BF16), y_bd)

    cs = range(n_pairs * n_chunks)
    chunk = lambda x: [x[c * C:(c + 1) * C, j * LANES:(j + 1) * LANES]
                       for j in range(n_pairs) for c in range(n_chunks)]
    r, k, v, lw, a, b = (chunk(x) for x in (r_all, k_all, v_all, lw_all, a_all, b_all))
    lws = [_split(x) for x in lw]
    cum = [_dot(tri2, jnp.concatenate([h, l], axis=0)) for h, l in lws]
    wcol = [jnp.exp(_dot_tn(h, ones) + _dot_tn(l, ones)) for h, l in lws]
    e_neg = [jnp.exp(-x) for x in cum]
    e_rem = [jnp.exp(x[C - 1:C, :] - x) for x in cum]
    rt = [r[c] * jnp.exp(cum[c]) for c in cs]
    at = [a[c] * jnp.exp(cum[c] - lw[c]) for c in cs]
    kt = [k[c] * e_neg[c] for c in cs]
    bt = [b[c] * e_neg[c] for c in cs]
    kw = [k[c] * e_rem[c] for c in cs]
    bw = [b[c] * e_rem[c] for c in cs]

    at_b = [x.astype(BF16) for x in at]
    rt_b = [x.astype(BF16) for x in rt]
    lhs = [jnp.concatenate([at_b[c], rt_b[c]], axis=0) for c in cs]
    rhs = [jnp.concatenate([bd(bt[c].astype(BF16)), bd(kt[c].astype(BF16))], axis=0) for c in cs]
    aa = [_dot_nt(lhs[c], rhs[c]) for c in cs]
    L = LANES
    a_ab = [jnp.where(strict, x[:C, :L], 0.0) for x in aa]
    a_ak = [jnp.where(strict, x[:C, L:], 0.0) for x in aa]
    a_rb = [jnp.where(incl, x[C:, :L], 0.0) for x in aa]
    a_rk = [jnp.where(incl, x[C:, L:], 0.0) for x in aa]

    d = [jnp.where(blk16, x, 0.0) for x in a_ab]
    x = [eye + dd for dd in d]
    y = [mm3(dd, dd) for dd in d]
    for _ in range(2):
        x = [x[c] + mm3(x[c], y[c]) for c in cs]
        y = [mm3(yy, yy) for yy in y]
    x = [x[c] + mm3(x[c], y[c]) for c in cs]
    for in_block in (blk32, None):
        if in_block is blk32:
            off = [jnp.where(blk32 & jnp.logical_not(blk16), aa_, 0.0) for aa_ in a_ab]
        else:
            off = [jnp.where(blk32, 0.0, aa_) for aa_ in a_ab]
        xo = [mm1(x[c], bd(off[c].astype(BF16))) for c in cs]
        x = [x[c] + mm1(xo[c], bd(x[c].astype(BF16))) for c in cs]
    t_inv = x

    rhs = [jnp.concatenate([bd(at_b[c]), bd(a_ak[c].astype(BF16))], axis=1) for c in cs]
    tt = [mm1(t_inv[c], rhs[c]) for c in cs]
    ap = [x_[:, :L].astype(BF16) for x_ in tt]
    ta = [x_[:, L:] for x_ in tt]
    bv = [bd(x_.astype(BF16)) for x_ in v]
    u_loc = [mm1(ta[c], bv[c]).astype(BF16) for c in cs]
    y_loc = [mm1(a_rk[c], bv[c]) for c in cs]
    a_rb = [x_.astype(BF16) for x_ in a_rb]
    bw_b = [x_.astype(BF16) for x_ in bw]
    bkw = [jnp.concatenate([bw_b[c], kw[c].astype(BF16)], axis=0) for c in cs]
    p_f32 = [jnp.where(same_head, _dot_tn(bw_b[c], ap[c]), 0.0) for c in cs]
    p_mat = [x_.astype(BF16) for x_ in p_f32]
    g_mat = [jnp.where(same_head, _dot_tn(bkw[c], jnp.concatenate([u_loc[c], v[c].astype(BF16)], axis=0)),
                       0.0) for c in cs]
    q_mat = [(rt[c] + _dot(a_rb[c], bd(ap[c]))).astype(BF16) for c in cs]
    z_mat = [_dot(a_rb[c], bd(u_loc[c])) + y_loc[c] for c in cs]
    assert n_chunks % 2 == 0
    evens = [j * n_chunks + c for j in range(n_pairs) for c in range(0, n_chunks, 2)]
    pg = {i: _dot(p_mat[i + 1], jnp.concatenate([p_mat[i], g_mat[i].astype(BF16)], axis=1)) for i in evens}
    w_row = {i: jnp.exp(cum[i][C - 1:C, :]) for i in evens}
    p2 = {i: (wcol[i + 1] * p_f32[i] + p_f32[i + 1] * w_row[i] + pg[i][:, :L]).astype(BF16) for i in evens}
    g2 = {i: wcol[i + 1] * g_mat[i] + pg[i][:, L:] + g_mat[i + 1] for i in evens}
    w2 = {i: wcol[i] * wcol[i + 1] for i in evens}

    states = [s_ref[j] for j in range(n_pairs)]
    entry = [None] * len(cs)
    odd_entry = [None] * len(cs)
    for c in range(0, n_chunks, 2):
        for j in range(n_pairs):
            i = j * n_chunks + c
            s_b = states[j].astype(BF16)
            entry[i] = s_b
            odd_entry[i + 1] = (states[j], s_b)
            states[j] = states[j] * w2[i] + _dot(p2[i], s_b) + g2[i]
    for j in range(n_pairs):
        s_ref[j] = states[j]
    for i in evens:
        s0, s0_b = odd_entry[i + 1]
        entry[i + 1] = (s0 * wcol[i] + _dot(p_mat[i], s0_b) + g_mat[i]).astype(BF16)
    for j in range(n_pairs):
        for c in range(n_chunks):
            i = j * n_chunks + c
            y_ref[c * C:(c + 1) * C, j * LANES:(j + 1) * LANES] = _dot(q_mat[i], entry[i]) + z_mat[i]

    inv_n = 1.0 / N
    y_all = y_ref[...]
    mu = _dot2(y_all, head_ones2) * inv_n
    dev = y_all - mu
    var = _dot2(dev * dev, head_ones2) * inv_n
    yn = dev * lax.rsqrt(var + RWKV_LN_EPS) * lnw_ref[...] + lnb_ref[...]
    bonus = _dot2(r_all * k_all * rk_ref[...], head_ones2) * v_all
    o_ref[...] = ((yn + bonus) * gate).astype(o_ref.dtype)


def _rwkv(p_rkv, p_lora, mix_rkv, mix_lora, w2p, a2p, g2p, w0, a0, k_k, k_a, r_k, ln_w, ln_b,
          ts=512, pairs_per_step=2):
    bsz, seq, _ = p_rkv.shape
    W = RWKV_WIDTH
    nl = p_lora.shape[-1]
    wd = pairs_per_step * LANES
    npair = W // wd
    col = lambda j: pl.BlockSpec((None, ts, wd), lambda bb, p, t, j=j: (bb, t, j * npair + p))
    vec_col = lambda j: pl.BlockSpec((1, wd), lambda bb, p, t, j=j: (0, j * npair + p))
    vec = pl.BlockSpec((1, wd), lambda bb, p, t: (0, p))
    mat = lambda rows: pl.BlockSpec((rows, wd), lambda bb, p, t: (0, p))
    kern = functools.partial(_rwkv_kernel, ts=ts)
    return pl.pallas_call(
        kern,
        grid=(bsz, npair, seq // ts),
        in_specs=[
            col(0), col(1), col(2),
            pl.BlockSpec((None, ts, nl), lambda bb, p, t: (bb, t, 0)),
            vec_col(0), vec_col(1), vec_col(2),
            pl.BlockSpec((1, nl), lambda bb, p, t: (0, 0)),
            mat(LANES), mat(LANES), mat(nl - LANES),
            vec, vec, vec, vec, vec, vec, vec,
        ],
        out_specs=pl.BlockSpec((None, ts, wd), lambda bb, p, t: (bb, t, p)),
        out_shape=jax.ShapeDtypeStruct((bsz, seq, W), BF16),
        scratch_shapes=[
            pltpu.VMEM((pairs_per_step, LANES, LANES), F32),
            pltpu.VMEM((8, wd), F32),
            pltpu.VMEM((1, nl), F32),
            pltpu.VMEM((ts, wd), F32),
        ],
        compiler_params=_params("arbitrary", "arbitrary", "arbitrary"),
        name="rwkv",
    )(p_rkv, p_rkv, p_rkv, p_lora, mix_rkv, mix_rkv, mix_rkv, mix_lora,
      w2p, a2p, g2p, w0, a0, k_k, k_a, r_k, ln_w, ln_b)


def _out_proj_kernel(oa_ref, or_ref, x_ref, w_ref, gpost_ref, gt_ref, gpre_ref, sc_ref, sh_ref,
                     wr_ref, br_ref, x1_ref, h2_ref, meta_ref, cnt_ref, run_ref):
    tm = x_ref.shape[0]
    half = oa_ref.shape[-1]

    @pl.when((pl.program_id(0) == 0) & (pl.program_id(1) == 0))
    def _():
        run_ref[...] = jnp.zeros_like(run_ref)

    y = _dot(oa_ref[...], w_ref[:half, :]) + _dot(or_ref[...], w_ref[half:, :])
    ms = jnp.mean(y * y, axis=-1, keepdims=True)
    x1 = x_ref[...] + gt_ref[...] * (y * lax.rsqrt(ms + RMS_EPS) * gpost_ref[...])
    x1_ref[...] = x1
    ms1 = jnp.mean(x1 * x1, axis=-1, keepdims=True)
    h2 = (x1 * lax.rsqrt(ms1 + RMS_EPS) * gpre_ref[...]) * (1.0 + sc_ref[...]) + sh_ref[...]
    h2_ref[...] = _pack_halves(h2)

    lg = _dot3(h2, wr_ref[...]) + br_ref[...]
    lane_i = lax.broadcasted_iota(jnp.int32, lg.shape, 1)
    lane = lane_i.astype(F32)
    lane_group = ((lane_i - N_GROUPS) >> 3).astype(F32)
    first = lambda mask: jnp.min(jnp.where(mask, lane, 4.0 * LANES), axis=-1, keepdims=True)
    gl = jnp.where(lane_i < N_GROUPS, lg, NEG_BIG)
    gmax = jnp.max(gl, axis=-1, keepdims=True)
    g_sel = first(gl == gmax)
    p_group = 1.0 / jnp.sum(jnp.exp(gl - gmax), axis=-1, keepdims=True)
    in_group = (lane_i >= N_GROUPS) & (lane_group == g_sel)
    el = jnp.where(in_group, lg, NEG_BIG)
    m0 = jnp.max(el, axis=-1, keepdims=True)
    i0 = first(el == m0)
    el = jnp.where(lane == i0, NEG_BIG, el)
    m1 = jnp.max(el, axis=-1, keepdims=True)
    i1 = first(el == m1)
    t = jnp.exp(m1 - m0)
    gate0 = p_group / (1.0 + t)
    gate1 = p_group * t / (1.0 + t)
    e0 = i0 - N_GROUPS
    e1 = i1 - N_GROUPS
    hit = (lane == e0) | (lane == e1)
    onehot = jnp.where(hit, 1.0, 0.0).astype(BF16)
    rr = lax.broadcasted_iota(jnp.int32, (tm, tm), 0)
    cc = lax.broadcasted_iota(jnp.int32, (tm, tm), 1)
    before = jnp.where(cc < rr, 1.0, 0.0).astype(BF16)
    seen = _dot(before, onehot) + run_ref[...]
    rank0 = jnp.sum(jnp.where(lane == e0, seen, 0.0), axis=-1, keepdims=True)
    rank1 = jnp.sum(jnp.where(lane == e1, seen, 0.0), axis=-1, keepdims=True)
    total = seen[tm - 1:tm, :] + jnp.where(hit[tm - 1:tm, :], 1.0, 0.0)
    run_ref[...] = total
    cnt_ref[...] = total
    meta = jnp.zeros(lg.shape, F32)
    for idx, val in ((META_E0, e0.astype(F32)), (META_E1, e1.astype(F32)), (META_RANK0, rank0),
                     (META_RANK1, rank1), (META_GATE0, gate0), (META_GATE1, gate1)):
        meta = jnp.where(lane == idx, val, meta)
    meta_ref[...] = meta


def _out_proj(o_att, o_rwkv, x, w_out, g_post, gt, g_pre, sc, sh, w_router, b_router, tm=512):
    bsz, seq, d = x.shape
    half = o_att.shape[-1]
    const = lambda b, i: (0, 0)
    row = lambda b, i: (b, i, 0)
    per_b = lambda b, i: (b, 0, 0)
    vec = pl.BlockSpec((1, d), const)
    mod = pl.BlockSpec((None, 1, d), per_b)
    return pl.pallas_call(
        _out_proj_kernel,
        grid=(bsz, seq // tm),
        in_specs=[
            pl.BlockSpec((None, tm, half), row),
            pl.BlockSpec((None, tm, half), row),
            pl.BlockSpec((None, tm, d), row),
            pl.BlockSpec((d, d), const, pipeline_mode=pl.Buffered(1)),
            vec, mod, vec, mod, mod,
            pl.BlockSpec((d, LANES), const),
            pl.BlockSpec((1, LANES), const),
        ],
        out_specs=[
            pl.BlockSpec((None, tm, d), row),
            pl.BlockSpec((None, tm, d // 2), row),
            pl.BlockSpec((None, tm, LANES), row),
            pl.BlockSpec((1, LANES), const),
        ],
        out_shape=[
            jax.ShapeDtypeStruct((bsz, seq, d), F32),
            jax.ShapeDtypeStruct((bsz, seq, d // 2), U32),
            jax.ShapeDtypeStruct((bsz, seq, LANES), F32),
            jax.ShapeDtypeStruct((1, LANES), F32),
        ],
        scratch_shapes=[pltpu.VMEM((1, LANES), F32)],
        compiler_params=_params("arbitrary", "arbitrary"),
        name="out_proj",
    )(o_att, o_rwkv, x, w_out, g_post.reshape(1, d), gt, g_pre.reshape(1, d), sc, sh, w_router, b_router)


def _pack_halves(x):
    d2 = x.shape[-1] // 2
    lo_bits = lax.bitcast_convert_type(x[:, :d2].astype(BF16).astype(F32), U32)
    hi_bits = lax.bitcast_convert_type(x[:, d2:].astype(BF16).astype(F32), U32)
    return (lo_bits >> 16) | hi_bits


def _unpack_halves(words):
    lo = lax.bitcast_convert_type(words << 16, F32)
    hi = lax.bitcast_convert_type(words & jnp.uint32(0xFFFF0000), F32)
    return lo, hi


def _expert_kernel(te_ref, nv_ref, n1_ref, n2_ref, sl_ref, x_ref, wg_hbm, wu_hbm, wd_hbm, o_ref,
                   wg_f, wu_f, wd_f, wg_b, wu_b, wd_b, sem):
    i = pl.program_id(0)
    valid = i < nv_ref[0]
    prev = te_ref[jnp.maximum(i - 1, 0)]
    new_expert = (i == 0) | (te_ref[i] != prev)

    def weight_copies(e, slot):
        return (pltpu.make_async_copy(wg_hbm.at[e], wg_f.at[slot], sem.at[slot, 0]),
                pltpu.make_async_copy(wu_hbm.at[e], wu_f.at[slot], sem.at[slot, 1]),
                pltpu.make_async_copy(wd_hbm.at[e], wd_f.at[slot], sem.at[slot, 2]))

    def next_slot(slot, k):
        s = slot + k
        return jnp.where(s >= WEIGHT_SLOTS, s - WEIGHT_SLOTS, s)

    @pl.when(valid & (i == 0))
    def _():
        for cp in weight_copies(te_ref[0], sl_ref[0]):
            cp.start()

        @pl.when(n1_ref[0] >= 0)
        def _():
            for cp in weight_copies(n1_ref[0], next_slot(sl_ref[0], 1)):
                cp.start()

    @pl.when(valid & new_expert)
    def _():
        slot = sl_ref[i]
        for cp in weight_copies(te_ref[i], slot):
            cp.wait()

        @pl.when(n2_ref[i] >= 0)
        def _():
            for cp in weight_copies(n2_ref[i], next_slot(slot, 2)):
                cp.start()

        wg_b[...] = wg_f[slot].astype(BF16)
        wu_b[...] = wu_f[slot].astype(BF16)
        wd_b[...] = wd_f[slot].astype(BF16)

    @pl.when(valid)
    def _():
        x_lo, x_hi = _unpack_halves(x_ref[...])
        d2 = x_lo.shape[-1]
        x_lo, x_hi = x_lo.astype(BF16), x_hi.astype(BF16)
        g = _dot(x_lo, wg_b[:d2, :]) + _dot(x_hi, wg_b[d2:, :])
        u = _dot(x_lo, wu_b[:d2, :]) + _dot(x_hi, wu_b[d2:, :])
        hid = (g * jax.nn.sigmoid(g) * u).astype(BF16)
        o_ref[...] = _pack_halves(_dot(hid, wd_b[...]))

    @pl.when(jnp.logical_not(valid))
    def _():
        o_ref[...] = jnp.zeros_like(o_ref)


def _experts(xs, tile_expert, n_valid, next1, next2, slot, w_gate, w_up, w_down, tm):
    m_pad, d2 = xs.shape
    d = 2 * d2
    de = w_gate.shape[-1]
    n_tiles = m_pad // tm
    grid_spec = pltpu.PrefetchScalarGridSpec(
        num_scalar_prefetch=5,
        grid=(n_tiles,),
        in_specs=[
            pl.BlockSpec((tm, d2), lambda i, *_: (i, 0)),
            pl.BlockSpec(memory_space=pl.ANY),
            pl.BlockSpec(memory_space=pl.ANY),
            pl.BlockSpec(memory_space=pl.ANY),
        ],
        out_specs=pl.BlockSpec((tm, d2), lambda i, *_: (i, 0)),
        scratch_shapes=[
            pltpu.VMEM((WEIGHT_SLOTS, d, de), F32),
            pltpu.VMEM((WEIGHT_SLOTS, d, de), F32),
            pltpu.VMEM((WEIGHT_SLOTS, de, d), F32),
            pltpu.VMEM((d, de), BF16),
            pltpu.VMEM((d, de), BF16),
            pltpu.VMEM((de, d), BF16),
            pltpu.SemaphoreType.DMA((WEIGHT_SLOTS, 3)),
        ],
    )
    return pl.pallas_call(
        _expert_kernel,
        grid_spec=grid_spec,
        out_shape=jax.ShapeDtypeStruct((m_pad, d2), U32),
        compiler_params=_params("arbitrary"),
        name="experts",
    )(tile_expert, n_valid, next1, next2, slot, xs, w_gate, w_up, w_down)


def _final_kernel(x_ref, y0_ref, y1_ref, meta_ref, g_ref, gt_ref, o_ref):
    meta = meta_ref[...]
    gate0 = meta[:, META_GATE0:META_GATE0 + 1]
    gate1 = meta[:, META_GATE1:META_GATE1 + 1]
    d2 = y0_ref.shape[-1]
    halves = [a * gate0 + b * gate1 for a, b in zip(_unpack_halves(y0_ref[...]), _unpack_halves(y1_ref[...]))]
    ssq = sum(jnp.sum(h * h, axis=-1, keepdims=True) for h in halves)
    inv = lax.rsqrt(ssq * (1.0 / (2 * d2)) + RMS_EPS)
    for j, h in enumerate(halves):
        cols = slice(j * d2, (j + 1) * d2)
        o_ref[:, cols] = x_ref[:, cols] + gt_ref[:, cols] * (h * inv * g_ref[:, cols])


def _final(x1, y0, y1, meta, g, gt, tm=512):
    bsz, seq, d = x1.shape
    row = lambda b, i: (b, i, 0)
    blk = pl.BlockSpec((None, tm, d), row)
    packed = pl.BlockSpec((None, tm, d // 2), row)
    return pl.pallas_call(
        _final_kernel,
        grid=(bsz, seq // tm),
        in_specs=[blk, packed, packed,
                  pl.BlockSpec((None, tm, LANES), row),
                  pl.BlockSpec((1, d), lambda b, i: (0, 0)),
                  pl.BlockSpec((None, 1, d), lambda b, i: (b, 0, 0))],
        out_specs=blk,
        out_shape=jax.ShapeDtypeStruct((bsz, seq, d), F32),
        compiler_params=_params("arbitrary", "arbitrary"),
        name="final",
    )(x1, y0, y1, meta, g.reshape(1, d), gt)


def _dispatch(meta, counts, tm):
    n_tok = meta.shape[0]
    n_rows = n_tok * TOP_K
    n_tiles = n_rows // tm + N_EXPERTS
    expert = meta[:, META_E0:META_E1 + 1].astype(jnp.int32)
    rank = meta[:, META_RANK0:META_RANK1 + 1].astype(jnp.int32)
    counts = counts[0, :N_EXPERTS].astype(jnp.int32)
    tiles_e = (counts + tm - 1) // tm
    tile_end = jnp.cumsum(tiles_e)
    pad_start = (tile_end - tiles_e) * tm
    slot_pos = pad_start[expert] + rank
    n_valid = tile_end[-1]
    t_idx = jnp.arange(n_tiles, dtype=jnp.int32)
    tile_expert = jnp.searchsorted(tile_end, jnp.minimum(t_idx, n_valid - 1), side='right')
    tile_expert = jnp.minimum(tile_expert, N_EXPERTS - 1).astype(jnp.int32)
    tok = jnp.broadcast_to(jnp.arange(n_tok, dtype=jnp.int32)[:, None], (n_tok, TOP_K))
    filler = jnp.arange(n_tiles * tm, dtype=jnp.int32) % n_tok
    src_tok = filler.at[slot_pos.reshape(-1)].set(tok.reshape(-1))
    ids = jnp.arange(N_EXPERTS, dtype=jnp.int32)
    used = tiles_e > 0
    first_used_from = lax.cummin(jnp.where(used, ids, N_EXPERTS), reverse=True)
    next_used = jnp.concatenate([first_used_from[1:], jnp.full((1,), N_EXPERTS, jnp.int32)])
    next_used = jnp.where(next_used >= N_EXPERTS, -1, next_used)
    next2_used = jnp.where(next_used >= 0, next_used[jnp.maximum(next_used, 0)], -1)
    run_slot = (jnp.cumsum(used.astype(jnp.int32)) - 1) % WEIGHT_SLOTS
    per_tile = lambda table: table[tile_expert].astype(jnp.int32)
    return (src_tok, slot_pos, tile_expert, n_valid.reshape(1).astype(jnp.int32),
            per_tile(next_used), per_tile(next2_used), per_tile(run_slot))


def _pad_rows(w, rows_before, rows_total):
    return jnp.pad(w, ((rows_before, rows_total - rows_before - w.shape[0]), (0, 0))).astype(BF16)


def kernel(x, c, w_ada, b_ada, g_pre_mix, g_post_mix, g_pre_ffn, g_post_ffn, w_in, w_out,
           attn_out_gain, rwkv_shift_mix, rwkv_w0, rwkv_w2, rwkv_a0, rwkv_a2, rwkv_g2,
           rwkv_k_k, rwkv_k_a, rwkv_r_k, rwkv_ln_w, rwkv_ln_b, router_group_w, router_group_b,
           router_expert_w, router_expert_b, expert_w_gate, expert_w_up, expert_w_down):
    bsz, seq, d = x.shape
    depth = w_ada.shape[0]
    tm_moe = 256
    W = RWKV_WIDTH
    for l in range(depth):
        mod = _adaln(c, w_ada[l], b_ada[l])
        sh_m, sc_m, gt_m, sh_f, sc_f, gt_f = jnp.split(mod[:, None, :], N_MOD, axis=-1)

        w = w_in[l]
        n_att = 3 * ATT_WIDTH
        n_rkv = 3 * W
        n_w, n_a, n_g = rwkv_w2.shape[1], rwkv_a2.shape[1], rwkv_g2.shape[1]
        assert n_w + n_a == LANES and w.shape[1] == n_att + n_rkv + n_w + n_a + n_g
        n_lora = LANES + -(-n_g // LANES) * LANES
        pad_l = n_lora - (n_w + n_a + n_g)
        w_all = jnp.pad(w, ((0, 0), (0, pad_l))).astype(BF16)
        qkv, p_rkv, p_lora = _in_proj(x, g_pre_mix[l], sc_m, sh_m, w_all, n_att, n_rkv)
        o_att = _attention(qkv, attn_out_gain[l])
        mix = rwkv_shift_mix[l]
        row = lambda t: t.reshape(1, -1)
        o_rwkv = _rwkv(
            p_rkv, p_lora, row(mix[:n_rkv]), row(jnp.pad(mix[n_rkv:], (0, pad_l))),
            _pad_rows(rwkv_w2[l], 0, LANES), _pad_rows(rwkv_a2[l], n_w, LANES),
            _pad_rows(rwkv_g2[l], 0, n_lora - LANES),
            row(rwkv_w0[l]), row(rwkv_a0[l]), row(rwkv_k_k[l]), row(rwkv_k_a[l]), row(rwkv_r_k[l]),
            row(rwkv_ln_w[l]), row(rwkv_ln_b[l]))

        n_route = N_GROUPS + N_EXPERTS
        w_router = jnp.concatenate([router_group_w[l], router_expert_w[l]], axis=1)
        w_router = jnp.pad(w_router, ((0, 0), (0, LANES - n_route)))
        b_router = jnp.pad(jnp.concatenate([router_group_b[l], router_expert_b[l]]), (0, LANES - n_route))
        x1, h2w, meta, counts = _out_proj(o_att, o_rwkv, x, w_out[l].astype(BF16), g_post_mix[l], gt_m,
                                          g_pre_ffn[l], sc_f, sh_f, w_router, b_router.reshape(1, LANES))

        src_tok, slot_pos, tile_expert, n_valid, next1, next2, slot = _dispatch(
            meta.reshape(bsz * seq, LANES), counts, tm_moe)
        xs = h2w.reshape(bsz * seq, d // 2)[src_tok]
        rows = _experts(xs, tile_expert, n_valid, next1, next2, slot,
                        expert_w_gate[l], expert_w_up[l], expert_w_down[l], tm_moe)
        y0 = rows[slot_pos[:, 0]].reshape(bsz, seq, d // 2)
        y1 = rows[slot_pos[:, 1]].reshape(bsz, seq, d // 2)
        x = _final(x1, y0, y1, meta, g_post_ffn[l], gt_f)
    return x
```

```python
import functools

import jax
import jax.numpy as jnp
from jax import lax
from jax.experimental import pallas as pl
from jax.experimental.pallas import tpu as pltpu

F32 = jnp.float32
BF16 = jnp.bfloat16
U32 = jnp.uint32

LANES = 128
VMEM_LIMIT_BYTES = 56 * 1024 * 1024

ATT_HEADS = 8
ATT_HEAD_DIM = 128
ATT_WIDTH = ATT_HEADS * ATT_HEAD_DIM
RWKV_HEAD_DIM = 64
RWKV_WIDTH = 1024
RWKV_CHUNK = 64
RWKV_LN_EPS = 64e-5
RMS_EPS = 1e-6
LOG2_E = 1.4426950408889634
EXP2_CLAMP = 126.0
N_GROUPS = 8
EXPERTS_PER_GROUP = 8
N_EXPERTS = 64
TOP_K = 2
N_MOD = 6
NEG_BIG = -1e30
WEIGHT_SLOTS = 3
META_E0, META_E1, META_RANK0, META_RANK1, META_GATE0, META_GATE1 = range(6)
META_ROWS = 8


def _dot(a, b):
    return jnp.dot(a, b, preferred_element_type=F32)


def _dot_nt(a, b):
    return lax.dot_general(a, b, (((1,), (1,)), ((), ())), preferred_element_type=F32)


def _dot_tn(a, b):
    return lax.dot_general(a, b, (((0,), (0,)), ((), ())), preferred_element_type=F32)


def _split(x):
    hi = x.astype(BF16)
    lo = (x - hi.astype(F32)).astype(BF16)
    return hi, lo


def _dot3(a, b):
    ah, al = _split(a)
    bh, bl = _split(b)
    return _dot(ah, bh) + _dot(ah, bl) + _dot(al, bh)


def _dot2(x, m2):
    hi, lo = _split(x)
    return _dot(jnp.concatenate([hi, lo], axis=1), m2)


def _params(*sem):
    return pltpu.CompilerParams(dimension_semantics=sem, vmem_limit_bytes=VMEM_LIMIT_BYTES)


def _adaln_kernel(c_ref, w_ref, b_ref, o_ref):
    c = c_ref[...]
    s = (c * jax.nn.sigmoid(c)).astype(BF16)
    o_ref[...] = _dot(s, w_ref[...].astype(BF16)) + b_ref[...]


def _adaln(c, w, b):
    bsz, d = c.shape
    n = w.shape[1]
    rows = 8
    tn = 1536
    cp = jnp.zeros((rows, d), F32).at[:bsz].set(c)
    out = pl.pallas_call(
        _adaln_kernel,
        grid=(n // tn,),
        in_specs=[
            pl.BlockSpec((rows, d), lambda j: (0, 0)),
            pl.BlockSpec((d, tn), lambda j: (0, j)),
            pl.BlockSpec((1, tn), lambda j: (0, j)),
        ],
        out_specs=pl.BlockSpec((rows, tn), lambda j: (0, j)),
        out_shape=jax.ShapeDtypeStruct((rows, n), F32),
        compiler_params=_params("arbitrary"),
        name="adaln",
    )(cp, w, b.reshape(1, n))
    return out[:bsz]


def _in_proj_kernel(x_ref, g_ref, sc_ref, sh_ref, w_ref, qs_ref, oa_ref, or_ref, ol_ref):
    x = x_ref[...]
    ms = jnp.mean(x * x, axis=-1, keepdims=True)
    y = x * lax.rsqrt(ms + RMS_EPS) * g_ref[...]
    h = (y * (1.0 + sc_ref[...]) + sh_ref[...]).astype(BF16)
    na, nr = oa_ref.shape[-1], or_ref.shape[-1]
    oa_ref[...] = (_dot(h, w_ref[:, :na]) * qs_ref[...]).astype(oa_ref.dtype)
    or_ref[...] = _dot(h, w_ref[:, na:na + nr])
    ol_ref[...] = _dot(h, w_ref[:, na + nr:])


def _in_proj(x, g, sc, sh, w, na, nr, tm=256):
    bsz, seq, d = x.shape
    nl = w.shape[1] - na - nr
    q_scale = jnp.where(jnp.arange(na) < ATT_WIDTH, LOG2_E * ATT_HEAD_DIM ** -0.5, 1.0).astype(F32).reshape(1, na)
    const = lambda b, i: (0, 0)
    row = lambda b, i: (b, i, 0)
    per_b = lambda b, i: (b, 0, 0)
    return pl.pallas_call(
        _in_proj_kernel,
        grid=(bsz, seq // tm),
        in_specs=[
            pl.BlockSpec((None, tm, d), row),
            pl.BlockSpec((1, d), const),
            pl.BlockSpec((None, 1, d), per_b),
            pl.BlockSpec((None, 1, d), per_b),
            pl.BlockSpec((d, na + nr + nl), const, pipeline_mode=pl.Buffered(1)),
            pl.BlockSpec((1, na), const),
        ],
        out_specs=[
            pl.BlockSpec((None, tm, na), row),
            pl.BlockSpec((None, tm, nr), row),
            pl.BlockSpec((None, tm, nl), row),
        ],
        out_shape=[
            jax.ShapeDtypeStruct((bsz, seq, na), BF16),
            jax.ShapeDtypeStruct((bsz, seq, nr), F32),
            jax.ShapeDtypeStruct((bsz, seq, nl), F32),
        ],
        compiler_params=_params("arbitrary", "arbitrary"),
        name="in_proj",
    )(x, g.reshape(1, d), sc, sh, w, q_scale)


def _att_kernel(q_ref, k_ref, v_ref, gain_ref, o_ref, *, tq, tk):
    sub = LANES
    n_sub = tk // sub
    qi = pl.program_id(2)
    q = q_ref[...]
    r2 = lax.broadcasted_iota(jnp.int32, (2 * sub, 2 * sub), 0) & (sub - 1)
    c2 = lax.broadcasted_iota(jnp.int32, (2 * sub, 2 * sub), 1)
    tri_ones = jnp.where((c2 >= sub) | (r2 >= c2), 1.0, 0.0).astype(BF16)

    lower = (lax.broadcasted_iota(jnp.int32, (sub, sub), 1)
             < lax.broadcasted_iota(jnp.int32, (sub, sub), 0))

    def sweep(items):
        n = [it[1].shape[0] // sub for it in items]
        tiles = lambda x, m: [x[:, u * sub:(u + 1) * sub] for u in range(m)]
        z = [_dot_nt(it[0], it[1]) for it in items]
        sp = [jnp.maximum(jnp.log2(1.0 + jnp.exp2(jnp.minimum(x, EXP2_CLAMP))), x) for x in z]
        z = [tiles(x, m) for x, m in zip(z, n)]
        sp = [tiles(x, m) for x, m in zip(sp, n)]
        for i, it in enumerate(items):
            if it[5]:
                sp[i][-1] = jnp.where(lower, sp[i][-1], 0.0)
        cs = [[_dot(jnp.concatenate(_split(x), axis=1), tri_ones) for x in row] for row in sp]
        out = []
        for i, it in enumerate(items):
            carry, acc = (it[3], it[4]) if it[3] is not None else out[-1]
            w = [None] * n[i]
            for u in reversed(range(n[i])):
                w[u] = jnp.exp2(z[i][u] - (cs[i][u][:, :sub] + carry))
                carry = carry + cs[i][u][:, sub:]
            if it[5]:
                w[-1] = jnp.where(lower, w[-1], 0.0)
            acc = acc + _dot(jnp.concatenate(w, axis=1).astype(BF16), it[2])
            out.append((carry, acc))
        return out

    dh = ATT_HEAD_DIM
    heads = range(q.shape[1] // dh)
    hcols = lambda h: slice(h * dh, (h + 1) * dh)

    assert tq == tk
    start = pl.multiple_of(qi * tk, tk)
    zeros = jnp.zeros((sub, sub), F32)
    diag = sweep([(q[r * sub:(r + 1) * sub, hcols(h)], k_ref[pl.ds(start, (r + 1) * sub), hcols(h)],
                   v_ref[pl.ds(start, (r + 1) * sub), hcols(h)], zeros, zeros, True)
                  for h in heads for r in range(n_sub)])
    state = []
    for h in heads:
        blocks = diag[h * n_sub:(h + 1) * n_sub]
        state += [jnp.concatenate([c for c, _ in blocks], axis=0), jnp.concatenate([a for _, a in blocks], axis=0)]

    def keys(j, h):
        st = pl.multiple_of(j * tk, tk)
        return k_ref[pl.ds(st, tk), hcols(h)], v_ref[pl.ds(st, tk), hcols(h)]

    def flatten(results):
        return tuple(x for pair in results for x in pair)

    def two_tiles(i, state):
        j = qi - 1 - 2 * i
        items = []
        for h in heads:
            items += [(q[:, hcols(h)], *keys(j, h), state[2 * h], state[2 * h + 1], False),
                      (q[:, hcols(h)], *keys(j - 1, h), None, None, False)]
        return flatten(sweep(items)[1::2])

    def one_tile(i, state):
        return flatten(sweep([(q[:, hcols(h)], *keys(0, h), state[2 * h], state[2 * h + 1], False)
                              for h in heads]))

    state = lax.fori_loop(0, lax.shift_right_logical(qi, 1), two_tiles, tuple(state))
    state = lax.fori_loop(0, qi & 1, one_tile, state)
    for h in heads:
        acc = state[2 * h + 1]
        ms = jnp.mean(acc * acc, axis=-1, keepdims=True)
        o_ref[:, hcols(h)] = (acc * lax.rsqrt(ms + RMS_EPS) * gain_ref[:, hcols(h)]).astype(o_ref.dtype)


def _attention(qkv, gain, tq=512, tk=512, heads_per_step=2):
    bsz, seq, _ = qkv.shape
    ng = ATT_HEADS // heads_per_step
    wd = heads_per_step * ATT_HEAD_DIM
    assert tk % tq == 0 and seq % tk == 0
    kern = functools.partial(_att_kernel, tq=tq, tk=tk)
    return pl.pallas_call(
        kern,
        grid=(bsz, ng, seq // tq),
        in_specs=[
            pl.BlockSpec((None, tq, wd), lambda b, h, i: (b, i, h)),
            pl.BlockSpec((None, seq, wd), lambda b, h, i: (b, 0, ng + h)),
            pl.BlockSpec((None, seq, wd), lambda b, h, i: (b, 0, 2 * ng + h)),
            pl.BlockSpec((1, wd), lambda b, h, i: (0, h)),
        ],
        out_specs=pl.BlockSpec((None, tq, wd), lambda b, h, i: (b, i, h)),
        out_shape=jax.ShapeDtypeStruct((bsz, seq, ATT_WIDTH), BF16),
        compiler_params=_params("arbitrary", "arbitrary", "arbitrary"),
        name="attention",
    )(qkv, qkv, qkv, gain.reshape(1, ATT_WIDTH))


def _rwkv_kernel(pr_ref, pk_ref, pv_ref, pl_ref, mr_ref, mk_ref, mv_ref, ml_ref,
                 w2_ref, a2_ref, g2_ref, w0_ref, a0_ref, kk_ref, ka_ref, rk_ref, lnw_ref, lnb_ref,
                 o_ref, s_ref, prev_ref, prevl_ref, y_ref, *, ts):
    C = RWKV_CHUNK
    N = RWKV_HEAD_DIM
    n_chunks = ts // C

    @pl.when(pl.program_id(2) == 0)
    def _():
        s_ref[...] = jnp.zeros_like(s_ref)
        prev_ref[...] = jnp.zeros_like(prev_ref)
        prevl_ref[...] = jnp.zeros_like(prevl_ref)

    first_row = lax.broadcasted_iota(jnp.int32, (ts, 1), 0) == 0

    def shift(x, last_prev, mix):
        prev = jnp.where(first_row, last_prev, pltpu.roll(x, 1, 0))
        return x + (prev - x) * mix

    raw = [pr_ref[...], pk_ref[...], pv_ref[...]]
    raw_l = pl_ref[...]
    r_all = shift(raw[0], prev_ref[0:1, :], mr_ref[...])
    k_all = shift(raw[1], prev_ref[1:2, :], mk_ref[...])
    v_all = shift(raw[2], prev_ref[2:3, :], mv_ref[...])
    p_l = shift(raw_l, prevl_ref[...], ml_ref[...])
    for i in range(3):
        prev_ref[i:i + 1, :] = raw[i][ts - 1:ts, :]
    prevl_ref[...] = raw_l[ts - 1:ts, :]

    wd = o_ref.shape[-1]
    n_pairs = wd // LANES
    r128 = lax.broadcasted_iota(jnp.int32, (LANES, LANES), 0)
    c128 = lax.broadcasted_iota(jnp.int32, (LANES, LANES), 1)
    same_head = (r128 < N) == (c128 < N)
    rw = lax.broadcasted_iota(jnp.int32, (wd, wd), 0)
    cw = lax.broadcasted_iota(jnp.int32, (wd, wd), 1)
    head_ones = jnp.where((rw >> 6) == (cw >> 6), 1.0, 0.0).astype(BF16)
    head_ones2 = jnp.concatenate([head_ones, head_ones], axis=0)

    lw_pre = w0_ref[...] + _dot(jnp.tanh(p_l[:, :LANES]).astype(BF16), w2_ref[...])
    w_log = -(jnp.maximum(-lw_pre, 0.0) + jnp.log(1.0 + jnp.exp(-jnp.abs(lw_pre)))) - 0.5
    lw_all = -jnp.exp(w_log)
    rate = jax.nn.sigmoid(a0_ref[...] + _dot(p_l[:, :LANES].astype(BF16), a2_ref[...]))
    gate = _dot(jax.nn.sigmoid(p_l[:, LANES:]).astype(BF16), g2_ref[...])
    kk = k_all * kk_ref[...]
    norm = jnp.sqrt(_dot2(kk * kk, head_ones2))
    kk = kk / jnp.maximum(norm, 1e-12)
    k_all = k_all * (1.0 + (rate - 1.0) * ka_ref[...])
    a_all = -kk
    b_all = kk * rate

    row = lax.broadcasted_iota(jnp.int32, (C, LANES), 0)
    lane = lax.broadcasted_iota(jnp.int32, (C, LANES), 1)
    col = lane & (N - 1)
    head0 = lane < N
    strict = row > col
    incl = row >= col
    eye = jnp.where(row == col, 1.0, 0.0)
    blk16 = (row >> 4) == (col >> 4)
    blk32 = (row >> 5) == (col >> 5)
    tri2 = jnp.where(incl, 1.0, 0.0).astype(BF16)
    ones = jnp.ones((C, LANES), BF16)

    def bd(x):
        z = jnp.zeros_like(x)
        return jnp.concatenate([jnp.where(head0, x, z), jnp.where(head0, z, x)], axis=0)

    def mm3(x, y):
        xh, xl = _split(x)
        yh, yl = _split(y)
        byh, byl = bd(yh), bd(yl)
        rhs = jnp.concatenate([jnp.concatenate([byh, byl], axis=1),
                               jnp.concatenate([byh, jnp.zeros_like(byl)], axis=1)], axis=0)
        p = _dot(jnp.concatenate([xh, xl], axis=1), rhs)
        return p[:, :LANES] + p[:, LANES:]

    def mm1(x, y_bd):
        return _dot(x.astype(BF16), y_bd)

    cs = range(n_pairs * n_chunks)
    chunk = lambda x: [x[c * C:(c + 1) * C, j * LANES:(j + 1) * LANES]
                       for j in range(n_pairs) for c in range(n_chunks)]
    r, k, v, lw, a, b = (chunk(x) for x in (r_all, k_all, v_all, lw_all, a_all, b_all))
    lws = [_split(x) for x in lw]
    cum = [_dot(tri2, jnp.concatenate([h, l], axis=0)) for h, l in lws]
    wcol = [jnp.exp(_dot_tn(h, ones) + _dot_tn(l, ones)) for h, l in lws]
    e_neg = [jnp.exp(-x) for x in cum]
    e_rem = [jnp.exp(x[C - 1:C, :] - x) for x in cum]
    rt = [r[c] * jnp.exp(cum[c]) for c in cs]
    at = [a[c] * jnp.exp(cum[c] - lw[c]) for c in cs]
    kt = [k[c] * e_neg[c] for c in cs]
    bt = [b[c] * e_neg[c] for c in cs]
    kw = [k[c] * e_rem[c] for c in cs]
    bw = [b[c] * e_rem[c] for c in cs]

    at_b = [x.astype(BF16) for x in at]
    rt_b = [x.astype(BF16) for x in rt]
    lhs = [jnp.concatenate([at_b[c], rt_b[c]], axis=0) for c in cs]
    rhs = [jnp.concatenate([bd(bt[c].astype(BF16)), bd(kt[c].astype(BF16))], axis=0) for c in cs]
    aa = [_dot_nt(lhs[c], rhs[c]) for c in cs]
    L = LANES
    a_ab = [jnp.where(strict, x[:C, :L], 0.0) for x in aa]
    a_ak = [jnp.where(strict, x[:C, L:], 0.0) for x in aa]
    a_rb = [jnp.where(incl, x[C:, :L], 0.0) for x in aa]
    a_rk = [jnp.where(incl, x[C:, L:], 0.0) for x in aa]

    d = [jnp.where(blk16, x, 0.0) for x in a_ab]
    x = [eye + dd for dd in d]
    y = [mm3(dd, dd) for dd in d]
    for _ in range(2):
        x = [x[c] + mm3(x[c], y[c]) for c in cs]
        y = [mm3(yy, yy) for yy in y]
    x = [x[c] + mm3(x[c], y[c]) for c in cs]
    for in_block in (blk32, None):
        if in_block is blk32:
            off = [jnp.where(blk32 & jnp.logical_not(blk16), aa_, 0.0) for aa_ in a_ab]
        else:
            off = [jnp.where(blk32, 0.0, aa_) for aa_ in a_ab]
        xo = [mm1(x[c], bd(off[c].astype(BF16))) for c in cs]
        x = [x[c] + mm1(xo[c], bd(x[c].astype(BF16))) for c in cs]
    t_inv = x

    rhs = [jnp.concatenate([bd(at_b[c]), bd(a_ak[c].astype(BF16))], axis=1) for c in cs]
    tt = [mm1(t_inv[c], rhs[c]) for c in cs]
    ap = [x_[:, :L].astype(BF16) for x_ in tt]
    ta = [x_[:, L:] for x_ in tt]
    bv = [bd(x_.astype(BF16)) for x_ in v]
    u_loc = [mm1(ta[c], bv[c]).astype(BF16) for c in cs]
    y_loc = [mm1(a_rk[c], bv[c]) for c in cs]
    a_rb = [x_.astype(BF16) for x_ in a_rb]
    bw_b = [x_.astype(BF16) for x_ in bw]
    bkw = [jnp.concatenate([bw_b[c], kw[c].astype(BF16)], axis=0) for c in cs]
    p_f32 = [jnp.where(same_head, _dot_tn(bw_b[c], ap[c]), 0.0) for c in cs]
    p_mat = [x_.astype(BF16) for x_ in p_f32]
    g_mat = [jnp.where(same_head, _dot_tn(bkw[c], jnp.concatenate([u_loc[c], v[c].astype(BF16)], axis=0)),
                       0.0) for c in cs]
    q_mat = [(rt[c] + _dot(a_rb[c], bd(ap[c]))).astype(BF16) for c in cs]
    z_mat = [_dot(a_rb[c], bd(u_loc[c])) + y_loc[c] for c in cs]
    assert n_chunks % 2 == 0
    evens = [j * n_chunks + c for j in range(n_pairs) for c in range(0, n_chunks, 2)]
    pg = {i: _dot(p_mat[i + 1], jnp.concatenate([p_mat[i], g_mat[i].astype(BF16)], axis=1)) for i in evens}
    w_row = {i: jnp.exp(cum[i][C - 1:C, :]) for i in evens}
    p2 = {i: (wcol[i + 1] * p_f32[i] + p_f32[i + 1] * w_row[i] + pg[i][:, :L]).astype(BF16) for i in evens}
    g2 = {i: wcol[i + 1] * g_mat[i] + pg[i][:, L:] + g_mat[i + 1] for i in evens}
    w2 = {i: wcol[i] * wcol[i + 1] for i in evens}

    states = [s_ref[j] for j in range(n_pairs)]
    entry = [None] * len(cs)
    odd_entry = [None] * len(cs)
    for c in range(0, n_chunks, 2):
        for j in range(n_pairs):
            i = j * n_chunks + c
            s_b = states[j].astype(BF16)
            entry[i] = s_b
            odd_entry[i + 1] = (states[j], s_b)
            states[j] = states[j] * w2[i] + _dot(p2[i], s_b) + g2[i]
    for j in range(n_pairs):
        s_ref[j] = states[j]
    for i in evens:
        s0, s0_b = odd_entry[i + 1]
        entry[i + 1] = (s0 * wcol[i] + _dot(p_mat[i], s0_b) + g_mat[i]).astype(BF16)
    for j in range(n_pairs):
        for c in range(n_chunks):
            i = j * n_chunks + c
            y_ref[c * C:(c + 1) * C, j * LANES:(j + 1) * LANES] = _dot(q_mat[i], entry[i]) + z_mat[i]

    inv_n = 1.0 / N
    y_all = y_ref[...]
    mu = _dot2(y_all, head_ones2) * inv_n
    dev = y_all - mu
    var = _dot2(dev * dev, head_ones2) * inv_n
    yn = dev * lax.rsqrt(var + RWKV_LN_EPS) * lnw_ref[...] + lnb_ref[...]
    bonus = _dot2(r_all * k_all * rk_ref[...], head_ones2) * v_all
    o_ref[...] = ((yn + bonus) * gate).astype(o_ref.dtype)


def _rwkv(p_rkv, p_lora, mix_rkv, mix_lora, w2p, a2p, g2p, w0, a0, k_k, k_a, r_k, ln_w, ln_b,
          ts=512, pairs_per_step=2):
    bsz, seq, _ = p_rkv.shape
    W = RWKV_WIDTH
    nl = p_lora.shape[-1]
    wd = pairs_per_step * LANES
    npair = W // wd
    col = lambda j: pl.BlockSpec((None, ts, wd), lambda bb, p, t, j=j: (bb, t, j * npair + p))
    vec_col = lambda j: pl.BlockSpec((1, wd), lambda bb, p, t, j=j: (0, j * npair + p))
    vec = pl.BlockSpec((1, wd), lambda bb, p, t: (0, p))
    mat = lambda rows: pl.BlockSpec((rows, wd), lambda bb, p, t: (0, p))
    kern = functools.partial(_rwkv_kernel, ts=ts)
    return pl.pallas_call(
        kern,
        grid=(bsz, npair, seq // ts),
        in_specs=[
            col(0), col(1), col(2),
            pl.BlockSpec((None, ts, nl), lambda bb, p, t: (bb, t, 0)),
            vec_col(0), vec_col(1), vec_col(2),
            pl.BlockSpec((1, nl), lambda bb, p, t: (0, 0)),
            mat(LANES), mat(LANES), mat(nl - LANES),
            vec, vec, vec, vec, vec, vec, vec,
        ],
        out_specs=pl.BlockSpec((None, ts, wd), lambda bb, p, t: (bb, t, p)),
        out_shape=jax.ShapeDtypeStruct((bsz, seq, W), BF16),
        scratch_shapes=[
            pltpu.VMEM((pairs_per_step, LANES, LANES), F32),
            pltpu.VMEM((8, wd), F32),
            pltpu.VMEM((1, nl), F32),
            pltpu.VMEM((ts, wd), F32),
        ],
        compiler_params=_params("arbitrary", "arbitrary", "arbitrary"),
        name="rwkv",
    )(p_rkv, p_rkv, p_rkv, p_lora, mix_rkv, mix_rkv, mix_rkv, mix_lora,
      w2p, a2p, g2p, w0, a0, k_k, k_a, r_k, ln_w, ln_b)


def _out_proj_kernel(oa_ref, or_ref, x_ref, w_ref, gpost_ref, gt_ref, gpre_ref, sc_ref, sh_ref,
                     wr_ref, br_ref, x1_ref, h2_ref, meta_ref, meta_t_ref, cnt_ref, run_ref):
    tm = x_ref.shape[0]
    half = oa_ref.shape[-1]

    @pl.when((pl.program_id(0) == 0) & (pl.program_id(1) == 0))
    def _():
        run_ref[...] = jnp.zeros_like(run_ref)

    y = _dot(oa_ref[...], w_ref[:half, :]) + _dot(or_ref[...], w_ref[half:, :])
    ms = jnp.mean(y * y, axis=-1, keepdims=True)
    x1 = x_ref[...] + gt_ref[...] * (y * lax.rsqrt(ms + RMS_EPS) * gpost_ref[...])
    x1_ref[...] = x1
    ms1 = jnp.mean(x1 * x1, axis=-1, keepdims=True)
    h2 = (x1 * lax.rsqrt(ms1 + RMS_EPS) * gpre_ref[...]) * (1.0 + sc_ref[...]) + sh_ref[...]
    h2_ref[...] = _pack_halves(h2)

    lg = _dot3(h2, wr_ref[...]) + br_ref[...]
    lane_i = lax.broadcasted_iota(jnp.int32, lg.shape, 1)
    lane = lane_i.astype(F32)
    lane_group = ((lane_i - N_GROUPS) >> 3).astype(F32)
    first = lambda mask: jnp.min(jnp.where(mask, lane, 4.0 * LANES), axis=-1, keepdims=True)
    gl = jnp.where(lane_i < N_GROUPS, lg, NEG_BIG)
    gmax = jnp.max(gl, axis=-1, keepdims=True)
    g_sel = first(gl == gmax)
    p_group = 1.0 / jnp.sum(jnp.exp(gl - gmax), axis=-1, keepdims=True)
    in_group = (lane_i >= N_GROUPS) & (lane_group == g_sel)
    el = jnp.where(in_group, lg, NEG_BIG)
    m0 = jnp.max(el, axis=-1, keepdims=True)
    i0 = first(el == m0)
    el = jnp.where(lane == i0, NEG_BIG, el)
    m1 = jnp.max(el, axis=-1, keepdims=True)
    i1 = first(el == m1)
    t = jnp.exp(m1 - m0)
    gate0 = p_group / (1.0 + t)
    gate1 = p_group * t / (1.0 + t)
    e0 = i0 - N_GROUPS
    e1 = i1 - N_GROUPS
    hit = (lane == e0) | (lane == e1)
    onehot = jnp.where(hit, 1.0, 0.0).astype(BF16)
    rr = lax.broadcasted_iota(jnp.int32, (tm, tm), 0)
    cc = lax.broadcasted_iota(jnp.int32, (tm, tm), 1)
    before = jnp.where(cc < rr, 1.0, 0.0).astype(BF16)
    seen = _dot(before, onehot) + run_ref[...]
    rank0 = jnp.sum(jnp.where(lane == e0, seen, 0.0), axis=-1, keepdims=True)
    rank1 = jnp.sum(jnp.where(lane == e1, seen, 0.0), axis=-1, keepdims=True)
    total = seen[tm - 1:tm, :] + jnp.where(hit[tm - 1:tm, :], 1.0, 0.0)
    run_ref[...] = total
    cnt_ref[...] = total
    meta = jnp.zeros(lg.shape, F32)
    for idx, val in ((META_E0, e0.astype(F32)), (META_E1, e1.astype(F32)), (META_RANK0, rank0),
                     (META_RANK1, rank1), (META_GATE0, gate0), (META_GATE1, gate1)):
        meta = jnp.where(lane == idx, val, meta)
    meta_ref[...] = meta
    meta_t_ref[...] = meta.T[:META_ROWS, :]


def _out_proj(o_att, o_rwkv, x, w_out, g_post, gt, g_pre, sc, sh, w_router, b_router, tm=512):
    bsz, seq, d = x.shape
    half = o_att.shape[-1]
    const = lambda b, i: (0, 0)
    row = lambda b, i: (b, i, 0)
    per_b = lambda b, i: (b, 0, 0)
    vec = pl.BlockSpec((1, d), const)
    mod = pl.BlockSpec((None, 1, d), per_b)
    return pl.pallas_call(
        _out_proj_kernel,
        grid=(bsz, seq // tm),
        in_specs=[
            pl.BlockSpec((None, tm, half), row),
            pl.BlockSpec((None, tm, half), row),
            pl.BlockSpec((None, tm, d), row),
            pl.BlockSpec((d, d), const, pipeline_mode=pl.Buffered(1)),
            vec, mod, vec, mod, mod,
            pl.BlockSpec((d, LANES), const),
            pl.BlockSpec((1, LANES), const),
        ],
        out_specs=[
            pl.BlockSpec((None, tm, d), row),
            pl.BlockSpec((None, tm, d // 2), row),
            pl.BlockSpec((None, tm, LANES), row),
            pl.BlockSpec((META_ROWS, tm), lambda b, i: (0, b * (seq // tm) + i)),
            pl.BlockSpec((1, LANES), const),
        ],
        out_shape=[
            jax.ShapeDtypeStruct((bsz, seq, d), F32),
            jax.ShapeDtypeStruct((bsz, seq, d // 2), U32),
            jax.ShapeDtypeStruct((bsz, seq, LANES), F32),
            jax.ShapeDtypeStruct((META_ROWS, bsz * seq), F32),
            jax.ShapeDtypeStruct((1, LANES), F32),
        ],
        scratch_shapes=[pltpu.VMEM((1, LANES), F32)],
        compiler_params=_params("arbitrary", "arbitrary"),
        name="out_proj",
    )(o_att, o_rwkv, x, w_out, g_post.reshape(1, d), gt, g_pre.reshape(1, d), sc, sh, w_router, b_router)


def _pack_halves(x):
    d2 = x.shape[-1] // 2
    lo_bits = lax.bitcast_convert_type(x[:, :d2].astype(BF16).astype(F32), U32)
    hi_bits = lax.bitcast_convert_type(x[:, d2:].astype(BF16).astype(F32), U32)
    return (lo_bits >> 16) | hi_bits


def _unpack_halves(words):
    lo = lax.bitcast_convert_type(words << 16, F32)
    hi = lax.bitcast_convert_type(words & jnp.uint32(0xFFFF0000), F32)
    return lo, hi


def _expert_kernel(te_ref, nv_ref, n1_ref, n2_ref, sl_ref, x_ref, wg_hbm, wu_hbm, wd_hbm, o_ref,
                   wg_f, wu_f, wd_f, wg_b, wu_b, wd_b, sem):
    i = pl.program_id(0)
    valid = i < nv_ref[0]
    prev = te_ref[jnp.maximum(i - 1, 0)]
    new_expert = (i == 0) | (te_ref[i] != prev)

    def weight_copies(e, slot):
        return (pltpu.make_async_copy(wg_hbm.at[e], wg_f.at[slot], sem.at[slot, 0]),
                pltpu.make_async_copy(wu_hbm.at[e], wu_f.at[slot], sem.at[slot, 1]),
                pltpu.make_async_copy(wd_hbm.at[e], wd_f.at[slot], sem.at[slot, 2]))

    def next_slot(slot, k):
        s = slot + k
        return jnp.where(s >= WEIGHT_SLOTS, s - WEIGHT_SLOTS, s)

    @pl.when(valid & (i == 0))
    def _():
        for cp in weight_copies(te_ref[0], sl_ref[0]):
            cp.start()

        @pl.when(n1_ref[0] >= 0)
        def _():
            for cp in weight_copies(n1_ref[0], next_slot(sl_ref[0], 1)):
                cp.start()

    @pl.when(valid & new_expert)
    def _():
        slot = sl_ref[i]
        for cp in weight_copies(te_ref[i], slot):
            cp.wait()

        @pl.when(n2_ref[i] >= 0)
        def _():
            for cp in weight_copies(n2_ref[i], next_slot(slot, 2)):
                cp.start()

        wg_b[...] = wg_f[slot].astype(BF16)
        wu_b[...] = wu_f[slot].astype(BF16)
        wd_b[...] = wd_f[slot].astype(BF16)

    @pl.when(valid)
    def _():
        x_lo, x_hi = _unpack_halves(x_ref[...])
        d2 = x_lo.shape[-1]
        x_lo, x_hi = x_lo.astype(BF16), x_hi.astype(BF16)
        g = _dot(x_lo, wg_b[:d2, :]) + _dot(x_hi, wg_b[d2:, :])
        u = _dot(x_lo, wu_b[:d2, :]) + _dot(x_hi, wu_b[d2:, :])
        hid = (g * jax.nn.sigmoid(g) * u).astype(BF16)
        o_ref[...] = _pack_halves(_dot(hid, wd_b[...]))

    @pl.when(jnp.logical_not(valid))
    def _():
        o_ref[...] = jnp.zeros_like(o_ref)


def _experts(xs, tile_expert, n_valid, next1, next2, slot, w_gate, w_up, w_down, tm):
    m_pad, d2 = xs.shape
    d = 2 * d2
    de = w_gate.shape[-1]
    n_tiles = m_pad // tm
    grid_spec = pltpu.PrefetchScalarGridSpec(
        num_scalar_prefetch=5,
        grid=(n_tiles,),
        in_specs=[
            pl.BlockSpec((tm, d2), lambda i, *_: (i, 0)),
            pl.BlockSpec(memory_space=pl.ANY),
            pl.BlockSpec(memory_space=pl.ANY),
            pl.BlockSpec(memory_space=pl.ANY),
        ],
        out_specs=pl.BlockSpec((tm, d2), lambda i, *_: (i, 0)),
        scratch_shapes=[
            pltpu.VMEM((WEIGHT_SLOTS, d, de), F32),
            pltpu.VMEM((WEIGHT_SLOTS, d, de), F32),
            pltpu.VMEM((WEIGHT_SLOTS, de, d), F32),
            pltpu.VMEM((d, de), BF16),
            pltpu.VMEM((d, de), BF16),
            pltpu.VMEM((de, d), BF16),
            pltpu.SemaphoreType.DMA((WEIGHT_SLOTS, 3)),
        ],
    )
    return pl.pallas_call(
        _expert_kernel,
        grid_spec=grid_spec,
        out_shape=jax.ShapeDtypeStruct((m_pad, d2), U32),
        compiler_params=_params("arbitrary"),
        name="experts",
    )(tile_expert, n_valid, next1, next2, slot, xs, w_gate, w_up, w_down)


def _final_kernel(x_ref, y0_ref, y1_ref, meta_ref, g_ref, gt_ref, o_ref):
    meta = meta_ref[...]
    gate0 = meta[:, META_GATE0:META_GATE0 + 1]
    gate1 = meta[:, META_GATE1:META_GATE1 + 1]
    d2 = y0_ref.shape[-1]
    halves = [a * gate0 + b * gate1 for a, b in zip(_unpack_halves(y0_ref[...]), _unpack_halves(y1_ref[...]))]
    ssq = sum(jnp.sum(h * h, axis=-1, keepdims=True) for h in halves)
    inv = lax.rsqrt(ssq * (1.0 / (2 * d2)) + RMS_EPS)
    for j, h in enumerate(halves):
        cols = slice(j * d2, (j + 1) * d2)
        o_ref[:, cols] = x_ref[:, cols] + gt_ref[:, cols] * (h * inv * g_ref[:, cols])


def _final(x1, y0, y1, meta, g, gt, tm=512):
    bsz, seq, d = x1.shape
    row = lambda b, i: (b, i, 0)
    blk = pl.BlockSpec((None, tm, d), row)
    packed = pl.BlockSpec((None, tm, d // 2), row)
    return pl.pallas_call(
        _final_kernel,
        grid=(bsz, seq // tm),
        in_specs=[blk, packed, packed,
                  pl.BlockSpec((None, tm, LANES), row),
                  pl.BlockSpec((1, d), lambda b, i: (0, 0)),
                  pl.BlockSpec((None, 1, d), lambda b, i: (b, 0, 0))],
        out_specs=blk,
        out_shape=jax.ShapeDtypeStruct((bsz, seq, d), F32),
        compiler_params=_params("arbitrary", "arbitrary"),
        name="final",
    )(x1, y0, y1, meta, g.reshape(1, d), gt)


def _dispatch(meta_t, counts, tm):
    n_tok = meta_t.shape[1]
    n_rows = n_tok * TOP_K
    n_tiles = n_rows // tm + N_EXPERTS
    expert = meta_t[META_E0:META_E1 + 1].astype(jnp.int32)
    rank = meta_t[META_RANK0:META_RANK1 + 1].astype(jnp.int32)
    counts = counts[0, :N_EXPERTS].astype(jnp.int32)
    tiles_e = (counts + tm - 1) // tm
    tile_end = jnp.cumsum(tiles_e)
    pad_start = (tile_end - tiles_e) * tm
    slot_pos = pad_start[expert] + rank
    n_valid = tile_end[-1]
    t_idx = jnp.arange(n_tiles, dtype=jnp.int32)
    tile_expert = jnp.searchsorted(tile_end, jnp.minimum(t_idx, n_valid - 1), side='right')
    tile_expert = jnp.minimum(tile_expert, N_EXPERTS - 1).astype(jnp.int32)
    tok = jnp.broadcast_to(jnp.arange(n_tok, dtype=jnp.int32)[None, :], (TOP_K, n_tok))
    filler = jnp.arange(n_tiles * tm, dtype=jnp.int32) % n_tok
    src_tok = filler.at[slot_pos.reshape(-1)].set(tok.reshape(-1))
    ids = jnp.arange(N_EXPERTS, dtype=jnp.int32)
    used = tiles_e > 0
    first_used_from = lax.cummin(jnp.where(used, ids, N_EXPERTS), reverse=True)
    next_used = jnp.concatenate([first_used_from[1:], jnp.full((1,), N_EXPERTS, jnp.int32)])
    next_used = jnp.where(next_used >= N_EXPERTS, -1, next_used)
    next2_used = jnp.where(next_used >= 0, next_used[jnp.maximum(next_used, 0)], -1)
    run_slot = (jnp.cumsum(used.astype(jnp.int32)) - 1) % WEIGHT_SLOTS
    per_tile = lambda table: table[tile_expert].astype(jnp.int32)
    return (src_tok, slot_pos, tile_expert, n_valid.reshape(1).astype(jnp.int32),
            per_tile(next_used), per_tile(next2_used), per_tile(run_slot))


def _pad_rows(w, rows_before, rows_total):
    return jnp.pad(w, ((rows_before, rows_total - rows_before - w.shape[0]), (0, 0))).astype(BF16)


def kernel(x, c, w_ada, b_ada, g_pre_mix, g_post_mix, g_pre_ffn, g_post_ffn, w_in, w_out,
           attn_out_gain, rwkv_shift_mix, rwkv_w0, rwkv_w2, rwkv_a0, rwkv_a2, rwkv_g2,
           rwkv_k_k, rwkv_k_a, rwkv_r_k, rwkv_ln_w, rwkv_ln_b, router_group_w, router_group_b,
           router_expert_w, router_expert_b, expert_w_gate, expert_w_up, expert_w_down):
    bsz, seq, d = x.shape
    depth = w_ada.shape[0]
    tm_moe = 256
    W = RWKV_WIDTH
    for l in range(depth):
        mod = _adaln(c, w_ada[l], b_ada[l])
        sh_m, sc_m, gt_m, sh_f, sc_f, gt_f = jnp.split(mod[:, None, :], N_MOD, axis=-1)

        w = w_in[l]
        n_att = 3 * ATT_WIDTH
        n_rkv = 3 * W
        n_w, n_a, n_g = rwkv_w2.shape[1], rwkv_a2.shape[1], rwkv_g2.shape[1]
        assert n_w + n_a == LANES and w.shape[1] == n_att + n_rkv + n_w + n_a + n_g
        n_lora = LANES + -(-n_g // LANES) * LANES
        pad_l = n_lora - (n_w + n_a + n_g)
        w_all = jnp.pad(w, ((0, 0), (0, pad_l))).astype(BF16)
        qkv, p_rkv, p_lora = _in_proj(x, g_pre_mix[l], sc_m, sh_m, w_all, n_att, n_rkv)
        o_att = _attention(qkv, attn_out_gain[l])
        mix = rwkv_shift_mix[l]
        row = lambda t: t.reshape(1, -1)
        o_rwkv = _rwkv(
            p_rkv, p_lora, row(mix[:n_rkv]), row(jnp.pad(mix[n_rkv:], (0, pad_l))),
            _pad_rows(rwkv_w2[l], 0, LANES), _pad_rows(rwkv_a2[l], n_w, LANES),
            _pad_rows(rwkv_g2[l], 0, n_lora - LANES),
            row(rwkv_w0[l]), row(rwkv_a0[l]), row(rwkv_k_k[l]), row(rwkv_k_a[l]), row(rwkv_r_k[l]),
            row(rwkv_ln_w[l]), row(rwkv_ln_b[l]))

        n_route = N_GROUPS + N_EXPERTS
        w_router = jnp.concatenate([router_group_w[l], router_expert_w[l]], axis=1)
        w_router = jnp.pad(w_router, ((0, 0), (0, LANES - n_route)))
        b_router = jnp.pad(jnp.concatenate([router_group_b[l], router_expert_b[l]]), (0, LANES - n_route))
        x1, h2w, meta, meta_t, counts = _out_proj(o_att, o_rwkv, x, w_out[l].astype(BF16), g_post_mix[l], gt_m,
                                          g_pre_ffn[l], sc_f, sh_f, w_router, b_router.reshape(1, LANES))

        src_tok, slot_pos, tile_expert, n_valid, next1, next2, slot = _dispatch(meta_t, counts, tm_moe)
        xs = h2w.reshape(bsz * seq, d // 2)[src_tok]
        rows = _experts(xs, tile_expert, n_valid, next1, next2, slot,
                        expert_w_gate[l], expert_w_up[l], expert_w_down[l], tm_moe)
        y0 = rows[slot_pos[0]].reshape(bsz, seq, d // 2)
        y1 = rows[slot_pos[1]].reshape(bsz, seq, d // 2)
        x = _final(x1, y0, y1, meta, g_post_ffn[l], gt_f)
    return x
```

```python
import functools

import jax
import jax.numpy as jnp
from jax import lax
from jax.experimental import pallas as pl
from jax.experimental.pallas import tpu as pltpu

F32 = jnp.float32
BF16 = jnp.bfloat16
U32 = jnp.uint32

LANES = 128
VMEM_LIMIT_BYTES = 56 * 1024 * 1024

ATT_HEADS = 8
ATT_HEAD_DIM = 128
ATT_WIDTH = ATT_HEADS * ATT_HEAD_DIM
RWKV_HEAD_DIM = 64
RWKV_WIDTH = 1024
RWKV_CHUNK = 64
RWKV_LN_EPS = 64e-5
RMS_EPS = 1e-6
LOG2_E = 1.4426950408889634
EXP2_CLAMP = 126.0
N_GROUPS = 8
EXPERTS_PER_GROUP = 8
N_EXPERTS = 64
TOP_K = 2
N_MOD = 6
NEG_BIG = -1e30
WEIGHT_SLOTS = 3
META_E0, META_E1, META_RANK0, META_RANK1, META_GATE0, META_GATE1 = range(6)
META_ROWS = 8


def _dot(a, b):
    return jnp.dot(a, b, preferred_element_type=F32)


def _dot_nt(a, b):
    return lax.dot_general(a, b, (((1,), (1,)), ((), ())), preferred_element_type=F32)


def _dot_tn(a, b):
    return lax.dot_general(a, b, (((0,), (0,)), ((), ())), preferred_element_type=F32)


def _split(x):
    hi = x.astype(BF16)
    lo = (x - hi.astype(F32)).astype(BF16)
    return hi, lo


def _dot3(a, b):
    ah, al = _split(a)
    bh, bl = _split(b)
    return _dot(ah, bh) + _dot(ah, bl) + _dot(al, bh)


def _dot2(x, m2):
    hi, lo = _split(x)
    return _dot(jnp.concatenate([hi, lo], axis=1), m2)


def _params(*sem):
    return pltpu.CompilerParams(dimension_semantics=sem, vmem_limit_bytes=VMEM_LIMIT_BYTES)


def _adaln_kernel(c_ref, w_ref, b_ref, o_ref):
    c = c_ref[...]
    s = (c * jax.nn.sigmoid(c)).astype(BF16)
    o_ref[...] = _dot(s, w_ref[...].astype(BF16)) + b_ref[...]


def _adaln(c, w, b):
    bsz, d = c.shape
    n = w.shape[1]
    rows = 8
    tn = 1536
    cp = jnp.zeros((rows, d), F32).at[:bsz].set(c)
    out = pl.pallas_call(
        _adaln_kernel,
        grid=(n // tn,),
        in_specs=[
            pl.BlockSpec((rows, d), lambda j: (0, 0)),
            pl.BlockSpec((d, tn), lambda j: (0, j)),
            pl.BlockSpec((1, tn), lambda j: (0, j)),
        ],
        out_specs=pl.BlockSpec((rows, tn), lambda j: (0, j)),
        out_shape=jax.ShapeDtypeStruct((rows, n), F32),
        compiler_params=_params("arbitrary"),
        name="adaln",
    )(cp, w, b.reshape(1, n))
    return out[:bsz]


def _in_proj_kernel(x_ref, g_ref, sc_ref, sh_ref, w_ref, qs_ref, oa_ref, or_ref, ol_ref):
    x = x_ref[...]
    ms = jnp.mean(x * x, axis=-1, keepdims=True)
    y = x * lax.rsqrt(ms + RMS_EPS) * g_ref[...]
    h = (y * (1.0 + sc_ref[...]) + sh_ref[...]).astype(BF16)
    na, nr = oa_ref.shape[-1], or_ref.shape[-1]
    oa_ref[...] = (_dot(h, w_ref[:, :na]) * qs_ref[...]).astype(oa_ref.dtype)
    or_ref[...] = _dot(h, w_ref[:, na:na + nr])
    ol_ref[...] = _dot(h, w_ref[:, na + nr:])


def _in_proj(x, g, sc, sh, w, na, nr, tm=256):
    bsz, seq, d = x.shape
    nl = w.shape[1] - na - nr
    q_scale = jnp.where(jnp.arange(na) < ATT_WIDTH, LOG2_E * ATT_HEAD_DIM ** -0.5, 1.0).astype(F32).reshape(1, na)
    const = lambda b, i: (0, 0)
    row = lambda b, i: (b, i, 0)
    per_b = lambda b, i: (b, 0, 0)
    return pl.pallas_call(
        _in_proj_kernel,
        grid=(bsz, seq // tm),
        in_specs=[
            pl.BlockSpec((None, tm, d), row),
            pl.BlockSpec((1, d), const),
            pl.BlockSpec((None, 1, d), per_b),
            pl.BlockSpec((None, 1, d), per_b),
            pl.BlockSpec((d, na + nr + nl), const, pipeline_mode=pl.Buffered(1)),
            pl.BlockSpec((1, na), const),
        ],
        out_specs=[
            pl.BlockSpec((None, tm, na), row),
            pl.BlockSpec((None, tm, nr), row),
            pl.BlockSpec((None, tm, nl), row),
        ],
        out_shape=[
            jax.ShapeDtypeStruct((bsz, seq, na), BF16),
            jax.ShapeDtypeStruct((bsz, seq, nr), F32),
            jax.ShapeDtypeStruct((bsz, seq, nl), F32),
        ],
        compiler_params=_params("arbitrary", "arbitrary"),
        name="in_proj",
    )(x, g.reshape(1, d), sc, sh, w, q_scale)


def _att_kernel(q_ref, k_ref, v_ref, gain_ref, o_ref, *, tq, tk):
    sub = LANES
    n_sub = tk // sub
    qi = pl.program_id(2)
    q = q_ref[...]
    r2 = lax.broadcasted_iota(jnp.int32, (2 * sub, 2 * sub), 0) & (sub - 1)
    c2 = lax.broadcasted_iota(jnp.int32, (2 * sub, 2 * sub), 1)
    tri_ones = jnp.where((c2 >= sub) | (r2 >= c2), 1.0, 0.0).astype(BF16)

    lower = (lax.broadcasted_iota(jnp.int32, (sub, sub), 1)
             < lax.broadcasted_iota(jnp.int32, (sub, sub), 0))

    def sweep(items):
        n = [it[1].shape[0] // sub for it in items]
        tiles = lambda x, m: [x[:, u * sub:(u + 1) * sub] for u in range(m)]
        z = [_dot_nt(it[0], it[1]) for it in items]
        sp = [jnp.maximum(jnp.log2(1.0 + jnp.exp2(jnp.minimum(x, EXP2_CLAMP))), x) for x in z]
        z = [tiles(x, m) for x, m in zip(z, n)]
        sp = [tiles(x, m) for x, m in zip(sp, n)]
        for i, it in enumerate(items):
            if it[5]:
                sp[i][-1] = jnp.where(lower, sp[i][-1], 0.0)
        cs = [[_dot(jnp.concatenate(_split(x), axis=1), tri_ones) for x in row] for row in sp]
        out = []
        for i, it in enumerate(items):
            carry, acc = (it[3], it[4]) if it[3] is not None else out[-1]
            w = [None] * n[i]
            for u in reversed(range(n[i])):
                w[u] = jnp.exp2(z[i][u] - (cs[i][u][:, :sub] + carry))
                carry = carry + cs[i][u][:, sub:]
            if it[5]:
                w[-1] = jnp.where(lower, w[-1], 0.0)
            acc = acc + _dot(jnp.concatenate(w, axis=1).astype(BF16), it[2])
            out.append((carry, acc))
        return out

    dh = ATT_HEAD_DIM
    heads = range(q.shape[1] // dh)
    hcols = lambda h: slice(h * dh, (h + 1) * dh)

    assert tq == tk
    start = pl.multiple_of(qi * tk, tk)
    zeros = jnp.zeros((sub, sub), F32)
    diag = sweep([(q[r * sub:(r + 1) * sub, hcols(h)], k_ref[pl.ds(start, (r + 1) * sub), hcols(h)],
                   v_ref[pl.ds(start, (r + 1) * sub), hcols(h)], zeros, zeros, True)
                  for h in heads for r in range(n_sub)])
    state = []
    for h in heads:
        blocks = diag[h * n_sub:(h + 1) * n_sub]
        state += [jnp.concatenate([c for c, _ in blocks], axis=0), jnp.concatenate([a for _, a in blocks], axis=0)]

    def keys(j, h):
        st = pl.multiple_of(j * tk, tk)
        return k_ref[pl.ds(st, tk), hcols(h)], v_ref[pl.ds(st, tk), hcols(h)]

    def flatten(results):
        return tuple(x for pair in results for x in pair)

    def two_tiles(i, state):
        j = qi - 1 - 2 * i
        items = []
        for h in heads:
            items += [(q[:, hcols(h)], *keys(j, h), state[2 * h], state[2 * h + 1], False),
                      (q[:, hcols(h)], *keys(j - 1, h), None, None, False)]
        return flatten(sweep(items)[1::2])

    def one_tile(i, state):
        return flatten(sweep([(q[:, hcols(h)], *keys(0, h), state[2 * h], state[2 * h + 1], False)
                              for h in heads]))

    state = lax.fori_loop(0, lax.shift_right_logical(qi, 1), two_tiles, tuple(state))
    state = lax.fori_loop(0, qi & 1, one_tile, state)
    for h in heads:
        acc = state[2 * h + 1]
        ms = jnp.mean(acc * acc, axis=-1, keepdims=True)
        o_ref[:, hcols(h)] = (acc * lax.rsqrt(ms + RMS_EPS) * gain_ref[:, hcols(h)]).astype(o_ref.dtype)


def _attention(qkv, gain, tq=512, tk=512, heads_per_step=2):
    bsz, seq, _ = qkv.shape
    ng = ATT_HEADS // heads_per_step
    wd = heads_per_step * ATT_HEAD_DIM
    assert tk % tq == 0 and seq % tk == 0
    kern = functools.partial(_att_kernel, tq=tq, tk=tk)
    return pl.pallas_call(
        kern,
        grid=(bsz, ng, seq // tq),
        in_specs=[
            pl.BlockSpec((None, tq, wd), lambda b, h, i: (b, i, h)),
            pl.BlockSpec((None, seq, wd), lambda b, h, i: (b, 0, ng + h)),
            pl.BlockSpec((None, seq, wd), lambda b, h, i: (b, 0, 2 * ng + h)),
            pl.BlockSpec((1, wd), lambda b, h, i: (0, h)),
        ],
        out_specs=pl.BlockSpec((None, tq, wd), lambda b, h, i: (b, i, h)),
        out_shape=jax.ShapeDtypeStruct((bsz, seq, ATT_WIDTH), BF16),
        compiler_params=_params("arbitrary", "arbitrary", "arbitrary"),
        name="attention",
    )(qkv, qkv, qkv, gain.reshape(1, ATT_WIDTH))


def _rwkv_kernel(pr_ref, pk_ref, pv_ref, pl_ref, mr_ref, mk_ref, mv_ref, ml_ref,
                 w2_ref, a2_ref, g2_ref, w0_ref, a0_ref, kk_ref, ka_ref, rk_ref, lnw_ref, lnb_ref,
                 o_ref, s_ref, prev_ref, prevl_ref, y_ref, *, ts):
    C = RWKV_CHUNK
    N = RWKV_HEAD_DIM
    n_chunks = ts // C

    @pl.when(pl.program_id(2) == 0)
    def _():
        s_ref[...] = jnp.zeros_like(s_ref)
        prev_ref[...] = jnp.zeros_like(prev_ref)
        prevl_ref[...] = jnp.zeros_like(prevl_ref)

    first_row = lax.broadcasted_iota(jnp.int32, (ts, 1), 0) == 0

    def shift(x, last_prev, mix):
        prev = jnp.where(first_row, last_prev, pltpu.roll(x, 1, 0))
        return x + (prev - x) * mix

    raw = [pr_ref[...], pk_ref[...], pv_ref[...]]
    raw_l = pl_ref[...]
    r_all = shift(raw[0], prev_ref[0:1, :], mr_ref[...])
    k_all = shift(raw[1], prev_ref[1:2, :], mk_ref[...])
    v_all = shift(raw[2], prev_ref[2:3, :], mv_ref[...])
    p_l = shift(raw_l, prevl_ref[...], ml_ref[...])
    for i in range(3):
        prev_ref[i:i + 1, :] = raw[i][ts - 1:ts, :]
    prevl_ref[...] = raw_l[ts - 1:ts, :]

    wd = o_ref.shape[-1]
    n_pairs = wd // LANES
    r128 = lax.broadcasted_iota(jnp.int32, (LANES, LANES), 0)
    c128 = lax.broadcasted_iota(jnp.int32, (LANES, LANES), 1)
    same_head = (r128 < N) == (c128 < N)
    rw = lax.broadcasted_iota(jnp.int32, (wd, wd), 0)
    cw = lax.broadcasted_iota(jnp.int32, (wd, wd), 1)
    head_ones = jnp.where((rw >> 6) == (cw >> 6), 1.0, 0.0).astype(BF16)
    head_ones2 = jnp.concatenate([head_ones, head_ones], axis=0)

    lw_pre = w0_ref[...] + _dot(jnp.tanh(p_l[:, :LANES]).astype(BF16), w2_ref[...])
    w_log = -(jnp.maximum(-lw_pre, 0.0) + jnp.log(1.0 + jnp.exp(-jnp.abs(lw_pre)))) - 0.5
    lw_all = -jnp.exp(w_log)
    rate = jax.nn.sigmoid(a0_ref[...] + _dot(p_l[:, :LANES].astype(BF16), a2_ref[...]))
    gate = _dot(jax.nn.sigmoid(p_l[:, LANES:]).astype(BF16), g2_ref[...])
    kk = k_all * kk_ref[...]
    norm = jnp.sqrt(_dot2(kk * kk, head_ones2))
    kk = kk / jnp.maximum(norm, 1e-12)
    k_all = k_all * (1.0 + (rate - 1.0) * ka_ref[...])
    a_all = -kk
    b_all = kk * rate

    row = lax.broadcasted_iota(jnp.int32, (C, LANES), 0)
    lane = lax.broadcasted_iota(jnp.int32, (C, LANES), 1)
    col = lane & (N - 1)
    head0 = lane < N
    strict = row > col
    incl = row >= col
    eye = jnp.where(row == col, 1.0, 0.0)
    blk16 = (row >> 4) == (col >> 4)
    blk32 = (row >> 5) == (col >> 5)
    tri2 = jnp.where(incl, 1.0, 0.0).astype(BF16)
    ones = jnp.ones((C, LANES), BF16)

    def bd(x):
        z = jnp.zeros_like(x)
        return jnp.concatenate([jnp.where(head0, x, z), jnp.where(head0, z, x)], axis=0)

    def mm3(x, y):
        xh, xl = _split(x)
        yh, yl = _split(y)
        byh, byl = bd(yh), bd(yl)
        rhs = jnp.concatenate([jnp.concatenate([byh, byl], axis=1),
                               jnp.concatenate([byh, jnp.zeros_like(byl)], axis=1)], axis=0)
        p = _dot(jnp.concatenate([xh, xl], axis=1), rhs)
        return p[:, :LANES] + p[:, LANES:]

    def mm1(x, y_bd):
        return _dot(x.astype(BF16), y_bd)

    cs = range(n_pairs * n_chunks)
    chunk = lambda x: [x[c * C:(c + 1) * C, j * LANES:(j + 1) * LANES]
                       for j in range(n_pairs) for c in range(n_chunks)]
    r, k, v, lw, a, b = (chunk(x) for x in (r_all, k_all, v_all, lw_all, a_all, b_all))
    lws = [_split(x) for x in lw]
    cum = [_dot(tri2, jnp.concatenate([h, l], axis=0)) for h, l in lws]
    wcol = [jnp.exp(_dot_tn(h, ones) + _dot_tn(l, ones)) for h, l in lws]
    e_neg = [jnp.exp(-x) for x in cum]
    e_rem = [jnp.exp(x[C - 1:C, :] - x) for x in cum]
    rt = [r[c] * jnp.exp(cum[c]) for c in cs]
    at = [a[c] * jnp.exp(cum[c] - lw[c]) for c in cs]
    kt = [k[c] * e_neg[c] for c in cs]
    bt = [b[c] * e_neg[c] for c in cs]
    kw = [k[c] * e_rem[c] for c in cs]
    bw = [b[c] * e_rem[c] for c in cs]

    at_b = [x.astype(BF16) for x in at]
    rt_b = [x.astype(BF16) for x in rt]
    lhs = [jnp.concatenate([at_b[c], rt_b[c]], axis=0) for c in cs]
    rhs = [jnp.concatenate([bd(bt[c].astype(BF16)), bd(kt[c].astype(BF16))], axis=0) for c in cs]
    aa = [_dot_nt(lhs[c], rhs[c]) for c in cs]
    L = LANES
    a_ab = [jnp.where(strict, x[:C, :L], 0.0) for x in aa]
    a_ak = [jnp.where(strict, x[:C, L:], 0.0) for x in aa]
    a_rb = [jnp.where(incl, x[C:, :L], 0.0) for x in aa]
    a_rk = [jnp.where(incl, x[C:, L:], 0.0) for x in aa]

    d = [jnp.where(blk16, x, 0.0) for x in a_ab]
    x = [eye + dd for dd in d]
    y = [mm3(dd, dd) for dd in d]
    for _ in range(2):
        x = [x[c] + mm3(x[c], y[c]) for c in cs]
        y = [mm3(yy, yy) for yy in y]
    x = [x[c] + mm3(x[c], y[c]) for c in cs]
    for in_block in (blk32, None):
        if in_block is blk32:
            off = [jnp.where(blk32 & jnp.logical_not(blk16), aa_, 0.0) for aa_ in a_ab]
        else:
            off = [jnp.where(blk32, 0.0, aa_) for aa_ in a_ab]
        xo = [mm1(x[c], bd(off[c].astype(BF16))) for c in cs]
        x = [x[c] + mm1(xo[c], bd(x[c].astype(BF16))) for c in cs]
    t_inv = x

    rhs = [jnp.concatenate([bd(at_b[c]), bd(a_ak[c].astype(BF16))], axis=1) for c in cs]
    tt = [mm1(t_inv[c], rhs[c]) for c in cs]
    ap = [x_[:, :L].astype(BF16) for x_ in tt]
    ta = [x_[:, L:] for x_ in tt]
    bv = [bd(x_.astype(BF16)) for x_ in v]
    u_loc = [mm1(ta[c], bv[c]).astype(BF16) for c in cs]
    y_loc = [mm1(a_rk[c], bv[c]) for c in cs]
    a_rb = [x_.astype(BF16) for x_ in a_rb]
    bw_b = [x_.astype(BF16) for x_ in bw]
    bkw = [jnp.concatenate([bw_b[c], kw[c].astype(BF16)], axis=0) for c in cs]
    p_f32 = [jnp.where(same_head, _dot_tn(bw_b[c], ap[c]), 0.0) for c in cs]
    p_mat = [x_.astype(BF16) for x_ in p_f32]
    g_mat = [jnp.where(same_head, _dot_tn(bkw[c], jnp.concatenate([u_loc[c], v[c].astype(BF16)], axis=0)),
                       0.0) for c in cs]
    q_mat = [(rt[c] + _dot(a_rb[c], bd(ap[c]))).astype(BF16) for c in cs]
    z_mat = [_dot(a_rb[c], bd(u_loc[c])) + y_loc[c] for c in cs]
    assert n_chunks % 2 == 0
    evens = [j * n_chunks + c for j in range(n_pairs) for c in range(0, n_chunks, 2)]
    pg = {i: _dot(p_mat[i + 1], jnp.concatenate([p_mat[i], g_mat[i].astype(BF16)], axis=1)) for i in evens}
    w_row = {i: jnp.exp(cum[i][C - 1:C, :]) for i in evens}
    p2 = {i: (wcol[i + 1] * p_f32[i] + p_f32[i + 1] * w_row[i] + pg[i][:, :L]).astype(BF16) for i in evens}
    g2 = {i: wcol[i + 1] * g_mat[i] + pg[i][:, L:] + g_mat[i + 1] for i in evens}
    w2 = {i: wcol[i] * wcol[i + 1] for i in evens}

    states = [s_ref[j] for j in range(n_pairs)]
    entry = [None] * len(cs)
    odd_entry = [None] * len(cs)
    for c in range(0, n_chunks, 2):
        for j in range(n_pairs):
            i = j * n_chunks + c
            s_b = states[j].astype(BF16)
            entry[i] = s_b
            odd_entry[i + 1] = (states[j], s_b)
            states[j] = states[j] * w2[i] + _dot(p2[i], s_b) + g2[i]
    for j in range(n_pairs):
        s_ref[j] = states[j]
    for i in evens:
        s0, s0_b = odd_entry[i + 1]
        entry[i + 1] = (s0 * wcol[i] + _dot(p_mat[i], s0_b) + g_mat[i]).astype(BF16)
    for j in range(n_pairs):
        for c in range(n_chunks):
            i = j * n_chunks + c
            y_ref[c * C:(c + 1) * C, j * LANES:(j + 1) * LANES] = _dot(q_mat[i], entry[i]) + z_mat[i]

    inv_n = 1.0 / N
    y_all = y_ref[...]
    mu = _dot2(y_all, head_ones2) * inv_n
    dev = y_all - mu
    var = _dot2(dev * dev, head_ones2) * inv_n
    yn = dev * lax.rsqrt(var + RWKV_LN_EPS) * lnw_ref[...] + lnb_ref[...]
    bonus = _dot2(r_all * k_all * rk_ref[...], head_ones2) * v_all
    o_ref[...] = ((yn + bonus) * gate).astype(o_ref.dtype)


def _rwkv(p_rkv, p_lora, mix_rkv, mix_lora, w2p, a2p, g2p, w0, a0, k_k, k_a, r_k, ln_w, ln_b,
          ts=512, pairs_per_step=2):
    bsz, seq, _ = p_rkv.shape
    W = RWKV_WIDTH
    nl = p_lora.shape[-1]
    wd = pairs_per_step * LANES
    npair = W // wd
    col = lambda j: pl.BlockSpec((None, ts, wd), lambda bb, p, t, j=j: (bb, t, j * npair + p))
    vec_col = lambda j: pl.BlockSpec((1, wd), lambda bb, p, t, j=j: (0, j * npair + p))
    vec = pl.BlockSpec((1, wd), lambda bb, p, t: (0, p))
    mat = lambda rows: pl.BlockSpec((rows, wd), lambda bb, p, t: (0, p))
    kern = functools.partial(_rwkv_kernel, ts=ts)
    return pl.pallas_call(
        kern,
        grid=(bsz, npair, seq // ts),
        in_specs=[
            col(0), col(1), col(2),
            pl.BlockSpec((None, ts, nl), lambda bb, p, t: (bb, t, 0)),
            vec_col(0), vec_col(1), vec_col(2),
            pl.BlockSpec((1, nl), lambda bb, p, t: (0, 0)),
            mat(LANES), mat(LANES), mat(nl - LANES),
            vec, vec, vec, vec, vec, vec, vec,
        ],
        out_specs=pl.BlockSpec((None, ts, wd), lambda bb, p, t: (bb, t, p)),
        out_shape=jax.ShapeDtypeStruct((bsz, seq, W), BF16),
        scratch_shapes=[
            pltpu.VMEM((pairs_per_step, LANES, LANES), F32),
            pltpu.VMEM((8, wd), F32),
            pltpu.VMEM((1, nl), F32),
            pltpu.VMEM((ts, wd), F32),
        ],
        compiler_params=_params("arbitrary", "arbitrary", "arbitrary"),
        name="rwkv",
    )(p_rkv, p_rkv, p_rkv, p_lora, mix_rkv, mix_rkv, mix_rkv, mix_lora,
      w2p, a2p, g2p, w0, a0, k_k, k_a, r_k, ln_w, ln_b)


def _out_proj_kernel(oa_ref, or_ref, x_ref, w_ref, gpost_ref, gt_ref, gpre_ref, sc_ref, sh_ref,
                     wr_ref, br_ref, x1_ref, h2_ref, meta_ref, meta_t_ref, cnt_ref, run_ref):
    tm = x_ref.shape[0]
    half = oa_ref.shape[-1]

    @pl.when((pl.program_id(0) == 0) & (pl.program_id(1) == 0))
    def _():
        run_ref[...] = jnp.zeros_like(run_ref)

    y = _dot(oa_ref[...], w_ref[:half, :]) + _dot(or_ref[...], w_ref[half:, :])
    ms = jnp.mean(y * y, axis=-1, keepdims=True)
    x1 = x_ref[...] + gt_ref[...] * (y * lax.rsqrt(ms + RMS_EPS) * gpost_ref[...])
    x1_ref[...] = x1
    ms1 = jnp.mean(x1 * x1, axis=-1, keepdims=True)
    h2 = (x1 * lax.rsqrt(ms1 + RMS_EPS) * gpre_ref[...]) * (1.0 + sc_ref[...]) + sh_ref[...]
    h2_ref[...] = _pack_halves(h2)

    lg = _dot3(h2, wr_ref[...]) + br_ref[...]
    lane_i = lax.broadcasted_iota(jnp.int32, lg.shape, 1)
    lane = lane_i.astype(F32)
    lane_group = ((lane_i - N_GROUPS) >> 3).astype(F32)
    first = lambda mask: jnp.min(jnp.where(mask, lane, 4.0 * LANES), axis=-1, keepdims=True)
    gl = jnp.where(lane_i < N_GROUPS, lg, NEG_BIG)
    gmax = jnp.max(gl, axis=-1, keepdims=True)
    g_sel = first(gl == gmax)
    p_group = 1.0 / jnp.sum(jnp.exp(gl - gmax), axis=-1, keepdims=True)
    in_group = (lane_i >= N_GROUPS) & (lane_group == g_sel)
    el = jnp.where(in_group, lg, NEG_BIG)
    m0 = jnp.max(el, axis=-1, keepdims=True)
    i0 = first(el == m0)
    el = jnp.where(lane == i0, NEG_BIG, el)
    m1 = jnp.max(el, axis=-1, keepdims=True)
    i1 = first(el == m1)
    t = jnp.exp(m1 - m0)
    gate0 = p_group / (1.0 + t)
    gate1 = p_group * t / (1.0 + t)
    e0 = i0 - N_GROUPS
    e1 = i1 - N_GROUPS
    hit = (lane == e0) | (lane == e1)
    onehot = jnp.where(hit, 1.0, 0.0).astype(BF16)
    rr = lax.broadcasted_iota(jnp.int32, (tm, tm), 0)
    cc = lax.broadcasted_iota(jnp.int32, (tm, tm), 1)
    before = jnp.where(cc < rr, 1.0, 0.0).astype(BF16)
    seen = _dot(before, onehot) + run_ref[...]
    rank0 = jnp.sum(jnp.where(lane == e0, seen, 0.0), axis=-1, keepdims=True)
    rank1 = jnp.sum(jnp.where(lane == e1, seen, 0.0), axis=-1, keepdims=True)
    total = seen[tm - 1:tm, :] + jnp.where(hit[tm - 1:tm, :], 1.0, 0.0)
    run_ref[...] = total
    cnt_ref[...] = total
    meta = jnp.zeros(lg.shape, F32)
    for idx, val in ((META_E0, e0.astype(F32)), (META_E1, e1.astype(F32)), (META_RANK0, rank0),
                     (META_RANK1, rank1), (META_GATE0, gate0), (META_GATE1, gate1)):
        meta = jnp.where(lane == idx, val, meta)
    meta_ref[...] = meta
    meta_t_ref[...] = meta.T[:META_ROWS, :]


def _out_proj(o_att, o_rwkv, x, w_out, g_post, gt, g_pre, sc, sh, w_router, b_router, tm=512):
    bsz, seq, d = x.shape
    half = o_att.shape[-1]
    const = lambda b, i: (0, 0)
    row = lambda b, i: (b, i, 0)
    per_b = lambda b, i: (b, 0, 0)
    vec = pl.BlockSpec((1, d), const)
    mod = pl.BlockSpec((None, 1, d), per_b)
    return pl.pallas_call(
        _out_proj_kernel,
        grid=(bsz, seq // tm),
        in_specs=[
            pl.BlockSpec((None, tm, half), row),
            pl.BlockSpec((None, tm, half), row),
            pl.BlockSpec((None, tm, d), row),
            pl.BlockSpec((d, d), const, pipeline_mode=pl.Buffered(1)),
            vec, mod, vec, mod, mod,
            pl.BlockSpec((d, LANES), const),
            pl.BlockSpec((1, LANES), const),
        ],
        out_specs=[
            pl.BlockSpec((None, tm, d), row),
            pl.BlockSpec((None, tm, d // 2), row),
            pl.BlockSpec((None, tm, LANES), row),
            pl.BlockSpec((META_ROWS, tm), lambda b, i: (0, b * (seq // tm) + i)),
            pl.BlockSpec((1, LANES), const),
        ],
        out_shape=[
            jax.ShapeDtypeStruct((bsz, seq, d), F32),
            jax.ShapeDtypeStruct((bsz, seq, d // 2), U32),
            jax.ShapeDtypeStruct((bsz, seq, LANES), F32),
            jax.ShapeDtypeStruct((META_ROWS, bsz * seq), F32),
            jax.ShapeDtypeStruct((1, LANES), F32),
        ],
        scratch_shapes=[pltpu.VMEM((1, LANES), F32)],
        compiler_params=_params("arbitrary", "arbitrary"),
        name="out_proj",
    )(o_att, o_rwkv, x, w_out, g_post.reshape(1, d), gt, g_pre.reshape(1, d), sc, sh, w_router, b_router)


def _pack_halves(x):
    d2 = x.shape[-1] // 2
    lo_bits = lax.bitcast_convert_type(x[:, :d2].astype(BF16).astype(F32), U32)
    hi_bits = lax.bitcast_convert_type(x[:, d2:].astype(BF16).astype(F32), U32)
    return (lo_bits >> 16) | hi_bits


def _unpack_halves(words):
    lo = lax.bitcast_convert_type(words << 16, F32)
    hi = lax.bitcast_convert_type(words & jnp.uint32(0xFFFF0000), F32)
    return lo, hi


def _expert_kernel(te_ref, nv_ref, n1_ref, n2_ref, sl_ref, x_ref, wg_hbm, wu_hbm, wd_hbm, o_ref,
                   wg_f, wu_f, wd_f, wg_b, wu_b, wd_b, sem):
    i = pl.program_id(0)
    valid = i < nv_ref[0]
    prev = te_ref[jnp.maximum(i - 1, 0)]
    new_expert = (i == 0) | (te_ref[i] != prev)

    def weight_copies(e, slot):
        return (pltpu.make_async_copy(wg_hbm.at[e], wg_f.at[slot], sem.at[slot, 0]),
                pltpu.make_async_copy(wu_hbm.at[e], wu_f.at[slot], sem.at[slot, 1]),
                pltpu.make_async_copy(wd_hbm.at[e], wd_f.at[slot], sem.at[slot, 2]))

    def next_slot(slot, k):
        s = slot + k
        return jnp.where(s >= WEIGHT_SLOTS, s - WEIGHT_SLOTS, s)

    @pl.when(valid & (i == 0))
    def _():
        for cp in weight_copies(te_ref[0], sl_ref[0]):
            cp.start()

        @pl.when(n1_ref[0] >= 0)
        def _():
            for cp in weight_copies(n1_ref[0], next_slot(sl_ref[0], 1)):
                cp.start()

    @pl.when(valid & new_expert)
    def _():
        slot = sl_ref[i]
        for cp in weight_copies(te_ref[i], slot):
            cp.wait()

        @pl.when(n2_ref[i] >= 0)
        def _():
            for cp in weight_copies(n2_ref[i], next_slot(slot, 2)):
                cp.start()

        wg_b[...] = wg_f[slot].astype(BF16)
        wu_b[...] = wu_f[slot].astype(BF16)
        wd_b[...] = wd_f[slot].astype(BF16)

    @pl.when(valid)
    def _():
        x_lo, x_hi = _unpack_halves(x_ref[...])
        d2 = x_lo.shape[-1]
        x_lo, x_hi = x_lo.astype(BF16), x_hi.astype(BF16)
        g = _dot(x_lo, wg_b[:d2, :]) + _dot(x_hi, wg_b[d2:, :])
        u = _dot(x_lo, wu_b[:d2, :]) + _dot(x_hi, wu_b[d2:, :])
        hid = (g * jax.nn.sigmoid(g) * u).astype(BF16)
        o_ref[...] = _pack_halves(_dot(hid, wd_b[...]))

    @pl.when(jnp.logical_not(valid))
    def _():
        o_ref[...] = jnp.zeros_like(o_ref)


def _experts(xs, tile_expert, n_valid, next1, next2, slot, w_gate, w_up, w_down, tm):
    m_pad, d2 = xs.shape
    d = 2 * d2
    de = w_gate.shape[-1]
    n_tiles = m_pad // tm
    grid_spec = pltpu.PrefetchScalarGridSpec(
        num_scalar_prefetch=5,
        grid=(n_tiles,),
        in_specs=[
            pl.BlockSpec((tm, d2), lambda i, *_: (i, 0)),
            pl.BlockSpec(memory_space=pl.ANY),
            pl.BlockSpec(memory_space=pl.ANY),
            pl.BlockSpec(memory_space=pl.ANY),
        ],
        out_specs=pl.BlockSpec((tm, d2), lambda i, *_: (i, 0)),
        scratch_shapes=[
            pltpu.VMEM((WEIGHT_SLOTS, d, de), F32),
            pltpu.VMEM((WEIGHT_SLOTS, d, de), F32),
            pltpu.VMEM((WEIGHT_SLOTS, de, d), F32),
            pltpu.VMEM((d, de), BF16),
            pltpu.VMEM((d, de), BF16),
            pltpu.VMEM((de, d), BF16),
            pltpu.SemaphoreType.DMA((WEIGHT_SLOTS, 3)),
        ],
    )
    return pl.pallas_call(
        _expert_kernel,
        grid_spec=grid_spec,
        out_shape=jax.ShapeDtypeStruct((m_pad, d2), U32),
        compiler_params=_params("arbitrary"),
        name="experts",
    )(tile_expert, n_valid, next1, next2, slot, xs, w_gate, w_up, w_down)


def _final_kernel(x_ref, y0_ref, y1_ref, meta_ref, g_ref, gt_ref, o_ref):
    meta = meta_ref[...]
    gate0 = meta[:, META_GATE0:META_GATE0 + 1]
    gate1 = meta[:, META_GATE1:META_GATE1 + 1]
    d2 = y0_ref.shape[-1]
    halves = [a * gate0 + b * gate1 for a, b in zip(_unpack_halves(y0_ref[...]), _unpack_halves(y1_ref[...]))]
    ssq = sum(jnp.sum(h * h, axis=-1, keepdims=True) for h in halves)
    inv = lax.rsqrt(ssq * (1.0 / (2 * d2)) + RMS_EPS)
    for j, h in enumerate(halves):
        cols = slice(j * d2, (j + 1) * d2)
        o_ref[:, cols] = x_ref[:, cols] + gt_ref[:, cols] * (h * inv * g_ref[:, cols])


def _final(x1, y0, y1, meta, g, gt, tm=512):
    bsz, seq, d = x1.shape
    row = lambda b, i: (b, i, 0)
    blk = pl.BlockSpec((None, tm, d), row)
    packed = pl.BlockSpec((None, tm, d // 2), row)
    return pl.pallas_call(
        _final_kernel,
        grid=(bsz, seq // tm),
        in_specs=[blk, packed, packed,
                  pl.BlockSpec((None, tm, LANES), row),
                  pl.BlockSpec((1, d), lambda b, i: (0, 0)),
                  pl.BlockSpec((None, 1, d), lambda b, i: (b, 0, 0))],
        out_specs=blk,
        out_shape=jax.ShapeDtypeStruct((bsz, seq, d), F32),
        compiler_params=_params("arbitrary", "arbitrary"),
        name="final",
    )(x1, y0, y1, meta, g.reshape(1, d), gt)


def _dispatch(meta_t, counts, tm):
    n_tok = meta_t.shape[1]
    n_rows = n_tok * TOP_K
    n_tiles = n_rows // tm + N_EXPERTS
    expert = meta_t[META_E0:META_E1 + 1].astype(jnp.int32)
    rank = meta_t[META_RANK0:META_RANK1 + 1].astype(jnp.int32)
    counts = counts[0, :N_EXPERTS].astype(jnp.int32)
    tiles_e = (counts + tm - 1) // tm
    tile_end = jnp.cumsum(tiles_e)
    pad_start = (tile_end - tiles_e) * tm
    slot_pos = rank + sum(jnp.where(expert == e, pad_start[e], 0) for e in range(N_EXPERTS))
    n_valid = tile_end[-1]
    t_idx = jnp.arange(n_tiles, dtype=jnp.int32)
    tile_expert = jnp.searchsorted(tile_end, jnp.minimum(t_idx, n_valid - 1), side='right')
    tile_expert = jnp.minimum(tile_expert, N_EXPERTS - 1).astype(jnp.int32)
    tok = jnp.broadcast_to(jnp.arange(n_tok, dtype=jnp.int32)[None, :], (TOP_K, n_tok))
    filler = jnp.arange(n_tiles * tm, dtype=jnp.int32) % n_tok
    src_tok = filler.at[slot_pos.reshape(-1)].set(tok.reshape(-1))
    ids = jnp.arange(N_EXPERTS, dtype=jnp.int32)
    used = tiles_e > 0
    first_used_from = lax.cummin(jnp.where(used, ids, N_EXPERTS), reverse=True)
    next_used = jnp.concatenate([first_used_from[1:], jnp.full((1,), N_EXPERTS, jnp.int32)])
    next_used = jnp.where(next_used >= N_EXPERTS, -1, next_used)
    next2_used = jnp.where(next_used >= 0, next_used[jnp.maximum(next_used, 0)], -1)
    run_slot = (jnp.cumsum(used.astype(jnp.int32)) - 1) % WEIGHT_SLOTS
    per_tile = lambda table: table[tile_expert].astype(jnp.int32)
    return (src_tok, slot_pos, tile_expert, n_valid.reshape(1).astype(jnp.int32),
            per_tile(next_used), per_tile(next2_used), per_tile(run_slot))


def _pad_rows(w, rows_before, rows_total):
    return jnp.pad(w, ((rows_before, rows_total - rows_before - w.shape[0]), (0, 0))).astype(BF16)


def kernel(x, c, w_ada, b_ada, g_pre_mix, g_post_mix, g_pre_ffn, g_post_ffn, w_in, w_out,
           attn_out_gain, rwkv_shift_mix, rwkv_w0, rwkv_w2, rwkv_a0, rwkv_a2, rwkv_g2,
           rwkv_k_k, rwkv_k_a, rwkv_r_k, rwkv_ln_w, rwkv_ln_b, router_group_w, router_group_b,
           router_expert_w, router_expert_b, expert_w_gate, expert_w_up, expert_w_down):
    bsz, seq, d = x.shape
    depth = w_ada.shape[0]
    tm_moe = 256
    W = RWKV_WIDTH
    for l in range(depth):
        mod = _adaln(c, w_ada[l], b_ada[l])
        sh_m, sc_m, gt_m, sh_f, sc_f, gt_f = jnp.split(mod[:, None, :], N_MOD, axis=-1)

        w = w_in[l]
        n_att = 3 * ATT_WIDTH
        n_rkv = 3 * W
        n_w, n_a, n_g = rwkv_w2.shape[1], rwkv_a2.shape[1], rwkv_g2.shape[1]
        assert n_w + n_a == LANES and w.shape[1] == n_att + n_rkv + n_w + n_a + n_g
        n_lora = LANES + -(-n_g // LANES) * LANES
        pad_l = n_lora - (n_w + n_a + n_g)
        w_all = jnp.pad(w, ((0, 0), (0, pad_l))).astype(BF16)
        qkv, p_rkv, p_lora = _in_proj(x, g_pre_mix[l], sc_m, sh_m, w_all, n_att, n_rkv)
        o_att = _attention(qkv, attn_out_gain[l])
        mix = rwkv_shift_mix[l]
        row = lambda t: t.reshape(1, -1)
        o_rwkv = _rwkv(
            p_rkv, p_lora, row(mix[:n_rkv]), row(jnp.pad(mix[n_rkv:], (0, pad_l))),
            _pad_rows(rwkv_w2[l], 0, LANES), _pad_rows(rwkv_a2[l], n_w, LANES),
            _pad_rows(rwkv_g2[l], 0, n_lora - LANES),
            row(rwkv_w0[l]), row(rwkv_a0[l]), row(rwkv_k_k[l]), row(rwkv_k_a[l]), row(rwkv_r_k[l]),
            row(rwkv_ln_w[l]), row(rwkv_ln_b[l]))

        n_route = N_GROUPS + N_EXPERTS
        w_router = jnp.concatenate([router_group_w[l], router_expert_w[l]], axis=1)
        w_router = jnp.pad(w_router, ((0, 0), (0, LANES - n_route)))
        b_router = jnp.pad(jnp.concatenate([router_group_b[l], router_expert_b[l]]), (0, LANES - n_route))
        x1, h2w, meta, meta_t, counts = _out_proj(o_att, o_rwkv, x, w_out[l].astype(BF16), g_post_mix[l], gt_m,
                                          g_pre_ffn[l], sc_f, sh_f, w_router, b_router.reshape(1, LANES))

        src_tok, slot_pos, tile_expert, n_valid, next1, next2, slot = _dispatch(meta_t, counts, tm_moe)
        xs = h2w.reshape(bsz * seq, d // 2)[src_tok]
        rows = _experts(xs, tile_expert, n_valid, next1, next2, slot,
                        expert_w_gate[l], expert_w_up[l], expert_w_down[l], tm_moe)
        y0 = rows[slot_pos[0]].reshape(bsz, seq, d // 2)
        y1 = rows[slot_pos[1]].reshape(bsz, seq, d // 2)
        x = _final(x1, y0, y1, meta, g_post_ffn[l], gt_f)
    return x
```

```python
import functools

import jax
import jax.numpy as jnp
from jax import lax
from jax.experimental import pallas as pl
from jax.experimental.pallas import tpu as pltpu

F32 = jnp.float32
BF16 = jnp.bfloat16
U32 = jnp.uint32

LANES = 128
VMEM_LIMIT_BYTES = 56 * 1024 * 1024

ATT_HEADS = 8
ATT_HEAD_DIM = 128
ATT_WIDTH = ATT_HEADS * ATT_HEAD_DIM
RWKV_HEAD_DIM = 64
RWKV_WIDTH = 1024
RWKV_CHUNK = 64
RWKV_LN_EPS = 64e-5
RMS_EPS = 1e-6
LOG2_E = 1.4426950408889634
EXP2_CLAMP = 126.0
N_GROUPS = 8
EXPERTS_PER_GROUP = 8
N_EXPERTS = 64
TOP_K = 2
N_MOD = 6
NEG_BIG = -1e30
WEIGHT_SLOTS = 3
META_E0, META_E1, META_RANK0, META_RANK1, META_GATE0, META_GATE1 = range(6)
META_ROWS = 8


def _dot(a, b):
    return jnp.dot(a, b, preferred_element_type=F32)


def _dot_nt(a, b):
    return lax.dot_general(a, b, (((1,), (1,)), ((), ())), preferred_element_type=F32)


def _dot_tn(a, b):
    return lax.dot_general(a, b, (((0,), (0,)), ((), ())), preferred_element_type=F32)


def _split(x):
    hi = x.astype(BF16)
    lo = (x - hi.astype(F32)).astype(BF16)
    return hi, lo


def _dot3(a, b):
    ah, al = _split(a)
    bh, bl = _split(b)
    return _dot(ah, bh) + _dot(ah, bl) + _dot(al, bh)


def _dot2(x, m2):
    hi, lo = _split(x)
    return _dot(jnp.concatenate([hi, lo], axis=1), m2)


def _params(*sem):
    return pltpu.CompilerParams(dimension_semantics=sem, vmem_limit_bytes=VMEM_LIMIT_BYTES)


def _adaln_kernel(c_ref, w_ref, b_ref, o_ref):
    c = c_ref[...]
    s = (c * jax.nn.sigmoid(c)).astype(BF16)
    o_ref[...] = _dot(s, w_ref[...].astype(BF16)) + b_ref[...]


def _adaln(c, w, b):
    bsz, d = c.shape
    n = w.shape[1]
    rows = 8
    tn = 1536
    cp = jnp.zeros((rows, d), F32).at[:bsz].set(c)
    out = pl.pallas_call(
        _adaln_kernel,
        grid=(n // tn,),
        in_specs=[
            pl.BlockSpec((rows, d), lambda j: (0, 0)),
            pl.BlockSpec((d, tn), lambda j: (0, j)),
            pl.BlockSpec((1, tn), lambda j: (0, j)),
        ],
        out_specs=pl.BlockSpec((rows, tn), lambda j: (0, j)),
        out_shape=jax.ShapeDtypeStruct((rows, n), F32),
        compiler_params=_params("arbitrary"),
        name="adaln",
    )(cp, w, b.reshape(1, n))
    return out[:bsz]


def _in_proj_kernel(x_ref, g_ref, sc_ref, sh_ref, w_ref, qs_ref, oa_ref, or_ref, ol_ref):
    x = x_ref[...]
    ms = jnp.mean(x * x, axis=-1, keepdims=True)
    y = x * lax.rsqrt(ms + RMS_EPS) * g_ref[...]
    h = (y * (1.0 + sc_ref[...]) + sh_ref[...]).astype(BF16)
    na, nr = oa_ref.shape[-1], or_ref.shape[-1]
    oa_ref[...] = (_dot(h, w_ref[:, :na]) * qs_ref[...]).astype(oa_ref.dtype)
    or_ref[...] = _dot(h, w_ref[:, na:na + nr])
    ol_ref[...] = _dot(h, w_ref[:, na + nr:])


def _in_proj(x, g, sc, sh, w, na, nr, tm=256):
    bsz, seq, d = x.shape
    nl = w.shape[1] - na - nr
    q_scale = jnp.where(jnp.arange(na) < ATT_WIDTH, LOG2_E * ATT_HEAD_DIM ** -0.5, 1.0).astype(F32).reshape(1, na)
    const = lambda b, i: (0, 0)
    row = lambda b, i: (b, i, 0)
    per_b = lambda b, i: (b, 0, 0)
    return pl.pallas_call(
        _in_proj_kernel,
        grid=(bsz, seq // tm),
        in_specs=[
            pl.BlockSpec((None, tm, d), row),
            pl.BlockSpec((1, d), const),
            pl.BlockSpec((None, 1, d), per_b),
            pl.BlockSpec((None, 1, d), per_b),
            pl.BlockSpec((d, na + nr + nl), const, pipeline_mode=pl.Buffered(1)),
            pl.BlockSpec((1, na), const),
        ],
        out_specs=[
            pl.BlockSpec((None, tm, na), row),
            pl.BlockSpec((None, tm, nr), row),
            pl.BlockSpec((None, tm, nl), row),
        ],
        out_shape=[
            jax.ShapeDtypeStruct((bsz, seq, na), BF16),
            jax.ShapeDtypeStruct((bsz, seq, nr), F32),
            jax.ShapeDtypeStruct((bsz, seq, nl), F32),
        ],
        compiler_params=_params("arbitrary", "arbitrary"),
        name="in_proj",
    )(x, g.reshape(1, d), sc, sh, w, q_scale)


def _att_kernel(q_ref, k_ref, v_ref, gain_ref, o_ref, *, tq, tk):
    sub = LANES
    n_sub = tk // sub
    qi = pl.program_id(2)
    q = q_ref[...]
    r2 = lax.broadcasted_iota(jnp.int32, (2 * sub, 2 * sub), 0) & (sub - 1)
    c2 = lax.broadcasted_iota(jnp.int32, (2 * sub, 2 * sub), 1)
    tri_ones = jnp.where((c2 >= sub) | (r2 >= c2), 1.0, 0.0).astype(BF16)

    lower = (lax.broadcasted_iota(jnp.int32, (sub, sub), 1)
             < lax.broadcasted_iota(jnp.int32, (sub, sub), 0))

    def sweep(items):
        n = [it[1].shape[0] // sub for it in items]
        tiles = lambda x, m: [x[:, u * sub:(u + 1) * sub] for u in range(m)]
        z = [_dot_nt(it[0], it[1]) for it in items]
        sp = [jnp.maximum(jnp.log2(1.0 + jnp.exp2(jnp.minimum(x, EXP2_CLAMP))), x) for x in z]
        z = [tiles(x, m) for x, m in zip(z, n)]
        sp = [tiles(x, m) for x, m in zip(sp, n)]
        for i, it in enumerate(items):
            if it[5]:
                sp[i][-1] = jnp.where(lower, sp[i][-1], 0.0)
        cs = [[_dot(jnp.concatenate(_split(x), axis=1), tri_ones) for x in row] for row in sp]
        out = []
        for i, it in enumerate(items):
            carry, acc = (it[3], it[4]) if it[3] is not None else out[-1]
            w = [None] * n[i]
            for u in reversed(range(n[i])):
                w[u] = jnp.exp2(z[i][u] - (cs[i][u][:, :sub] + carry))
                carry = carry + cs[i][u][:, sub:]
            if it[5]:
                w[-1] = jnp.where(lower, w[-1], 0.0)
            acc = acc + _dot(jnp.concatenate(w, axis=1).astype(BF16), it[2])
            out.append((carry, acc))
        return out

    dh = ATT_HEAD_DIM
    heads = range(q.shape[1] // dh)
    hcols = lambda h: slice(h * dh, (h + 1) * dh)

    assert tq == tk
    start = pl.multiple_of(qi * tk, tk)
    zeros = jnp.zeros((sub, sub), F32)
    diag = sweep([(q[r * sub:(r + 1) * sub, hcols(h)], k_ref[pl.ds(start, (r + 1) * sub), hcols(h)],
                   v_ref[pl.ds(start, (r + 1) * sub), hcols(h)], zeros, zeros, True)
                  for h in heads for r in range(n_sub)])
    state = []
    for h in heads:
        blocks = diag[h * n_sub:(h + 1) * n_sub]
        state += [jnp.concatenate([c for c, _ in blocks], axis=0), jnp.concatenate([a for _, a in blocks], axis=0)]

    def keys(j, h):
        st = pl.multiple_of(j * tk, tk)
        return k_ref[pl.ds(st, tk), hcols(h)], v_ref[pl.ds(st, tk), hcols(h)]

    def flatten(results):
        return tuple(x for pair in results for x in pair)

    def two_tiles(i, state):
        j = qi - 1 - 2 * i
        items = []
        for h in heads:
            items += [(q[:, hcols(h)], *keys(j, h), state[2 * h], state[2 * h + 1], False),
                      (q[:, hcols(h)], *keys(j - 1, h), None, None, False)]
        return flatten(sweep(items)[1::2])

    def one_tile(i, state):
        return flatten(sweep([(q[:, hcols(h)], *keys(0, h), state[2 * h], state[2 * h + 1], False)
                              for h in heads]))

    state = lax.fori_loop(0, lax.shift_right_logical(qi, 1), two_tiles, tuple(state))
    state = lax.fori_loop(0, qi & 1, one_tile, state)
    for h in heads:
        acc = state[2 * h + 1]
        ms = jnp.mean(acc * acc, axis=-1, keepdims=True)
        o_ref[:, hcols(h)] = (acc * lax.rsqrt(ms + RMS_EPS) * gain_ref[:, hcols(h)]).astype(o_ref.dtype)


def _attention(qkv, gain, tq=512, tk=512, heads_per_step=4):
    bsz, seq, _ = qkv.shape
    ng = ATT_HEADS // heads_per_step
    wd = heads_per_step * ATT_HEAD_DIM
    assert tk % tq == 0 and seq % tk == 0
    kern = functools.partial(_att_kernel, tq=tq, tk=tk)
    return pl.pallas_call(
        kern,
        grid=(bsz, ng, seq // tq),
        in_specs=[
            pl.BlockSpec((None, tq, wd), lambda b, h, i: (b, i, h)),
            pl.BlockSpec((None, seq, wd), lambda b, h, i: (b, 0, ng + h)),
            pl.BlockSpec((None, seq, wd), lambda b, h, i: (b, 0, 2 * ng + h)),
            pl.BlockSpec((1, wd), lambda b, h, i: (0, h)),
        ],
        out_specs=pl.BlockSpec((None, tq, wd), lambda b, h, i: (b, i, h)),
        out_shape=jax.ShapeDtypeStruct((bsz, seq, ATT_WIDTH), BF16),
        compiler_params=_params("arbitrary", "arbitrary", "arbitrary"),
        name="attention",
    )(qkv, qkv, qkv, gain.reshape(1, ATT_WIDTH))


def _rwkv_kernel(pr_ref, pk_ref, pv_ref, pl_ref, mr_ref, mk_ref, mv_ref, ml_ref,
                 w2_ref, a2_ref, g2_ref, w0_ref, a0_ref, kk_ref, ka_ref, rk_ref, lnw_ref, lnb_ref,
                 o_ref, s_ref, prev_ref, prevl_ref, y_ref, *, ts):
    C = RWKV_CHUNK
    N = RWKV_HEAD_DIM
    n_chunks = ts // C

    @pl.when(pl.program_id(2) == 0)
    def _():
        s_ref[...] = jnp.zeros_like(s_ref)
        prev_ref[...] = jnp.zeros_like(prev_ref)
        prevl_ref[...] = jnp.zeros_like(prevl_ref)

    first_row = lax.broadcasted_iota(jnp.int32, (ts, 1), 0) == 0

    def shift(x, last_prev, mix):
        prev = jnp.where(first_row, last_prev, pltpu.roll(x, 1, 0))
        return x + (prev - x) * mix

    raw = [pr_ref[...], pk_ref[...], pv_ref[...]]
    raw_l = pl_ref[...]
    r_all = shift(raw[0], prev_ref[0:1, :], mr_ref[...])
    k_all = shift(raw[1], prev_ref[1:2, :], mk_ref[...])
    v_all = shift(raw[2], prev_ref[2:3, :], mv_ref[...])
    p_l = shift(raw_l, prevl_ref[...], ml_ref[...])
    for i in range(3):
        prev_ref[i:i + 1, :] = raw[i][ts - 1:ts, :]
    prevl_ref[...] = raw_l[ts - 1:ts, :]

    wd = o_ref.shape[-1]
    n_pairs = wd // LANES
    r128 = lax.broadcasted_iota(jnp.int32, (LANES, LANES), 0)
    c128 = lax.broadcasted_iota(jnp.int32, (LANES, LANES), 1)
    same_head = (r128 < N) == (c128 < N)
    rw = lax.broadcasted_iota(jnp.int32, (wd, wd), 0)
    cw = lax.broadcasted_iota(jnp.int32, (wd, wd), 1)
    head_ones = jnp.where((rw >> 6) == (cw >> 6), 1.0, 0.0).astype(BF16)
    head_ones2 = jnp.concatenate([head_ones, head_ones], axis=0)

    lw_pre = w0_ref[...] + _dot(jnp.tanh(p_l[:, :LANES]).astype(BF16), w2_ref[...])
    w_log = -(jnp.maximum(-lw_pre, 0.0) + jnp.log(1.0 + jnp.exp(-jnp.abs(lw_pre)))) - 0.5
    lw_all = -jnp.exp(w_log)
    rate = jax.nn.sigmoid(a0_ref[...] + _dot(p_l[:, :LANES].astype(BF16), a2_ref[...]))
    gate = _dot(jax.nn.sigmoid(p_l[:, LANES:]).astype(BF16), g2_ref[...])
    kk = k_all * kk_ref[...]
    norm = jnp.sqrt(_dot2(kk * kk, head_ones2))
    kk = kk / jnp.maximum(norm, 1e-12)
    k_all = k_all * (1.0 + (rate - 1.0) * ka_ref[...])
    a_all = -kk
    b_all = kk * rate

    row = lax.broadcasted_iota(jnp.int32, (C, LANES), 0)
    lane = lax.broadcasted_iota(jnp.int32, (C, LANES), 1)
    col = lane & (N - 1)
    head0 = lane < N
    strict = row > col
    incl = row >= col
    eye = jnp.where(row == col, 1.0, 0.0)
    blk16 = (row >> 4) == (col >> 4)
    blk32 = (row >> 5) == (col >> 5)
    tri2 = jnp.where(incl, 1.0, 0.0).astype(BF16)
    ones = jnp.ones((C, LANES), BF16)

    def bd(x):
        z = jnp.zeros_like(x)
        return jnp.concatenate([jnp.where(head0, x, z), jnp.where(head0, z, x)], axis=0)

    def mm3(x, y):
        xh, xl = _split(x)
        yh, yl = _split(y)
        byh, byl = bd(yh), bd(yl)
        rhs = jnp.concatenate([jnp.concatenate([byh, byl], axis=1),
                               jnp.concatenate([byh, jnp.zeros_like(byl)], axis=1)], axis=0)
        p = _dot(jnp.concatenate([xh, xl], axis=1), rhs)
        return p[:, :LANES] + p[:, LANES:]

    def mm1(x, y_bd):
        return _dot(x.astype(BF16), y_bd)

    cs = range(n_pairs * n_chunks)
    chunk = lambda x: [x[c * C:(c + 1) * C, j * LANES:(j + 1) * LANES]
                       for j in range(n_pairs) for c in range(n_chunks)]
    r, k, v, lw, a, b = (chunk(x) for x in (r_all, k_all, v_all, lw_all, a_all, b_all))
    lws = [_split(x) for x in lw]
    cum = [_dot(tri2, jnp.concatenate([h, l], axis=0)) for h, l in lws]
    wcol = [jnp.exp(_dot_tn(h, ones) + _dot_tn(l, ones)) for h, l in lws]
    e_neg = [jnp.exp(-x) for x in cum]
    e_rem = [jnp.exp(x[C - 1:C, :] - x) for x in cum]
    rt = [r[c] * jnp.exp(cum[c]) for c in cs]
    at = [a[c] * jnp.exp(cum[c] - lw[c]) for c in cs]
    kt = [k[c] * e_neg[c] for c in cs]
    bt = [b[c] * e_neg[c] for c in cs]
    kw = [k[c] * e_rem[c] for c in cs]
    bw = [b[c] * e_rem[c] for c in cs]

    at_b = [x.astype(BF16) for x in at]
    rt_b = [x.astype(BF16) for x in rt]
    lhs = [jnp.concatenate([at_b[c], rt_b[c]], axis=0) for c in cs]
    rhs = [jnp.concatenate([bd(bt[c].astype(BF16)), bd(kt[c].astype(BF16))], axis=0) for c in cs]
    aa = [_dot_nt(lhs[c], rhs[c]) for c in cs]
    L = LANES
    a_ab = [jnp.where(strict, x[:C, :L], 0.0) for x in aa]
    a_ak = [jnp.where(strict, x[:C, L:], 0.0) for x in aa]
    a_rb = [jnp.where(incl, x[C:, :L], 0.0) for x in aa]
    a_rk = [jnp.where(incl, x[C:, L:], 0.0) for x in aa]

    d = [jnp.where(blk16, x, 0.0) for x in a_ab]
    x = [eye + dd for dd in d]
    y = [mm3(dd, dd) for dd in d]
    for _ in range(2):
        x = [x[c] + mm3(x[c], y[c]) for c in cs]
        y = [mm3(yy, yy) for yy in y]
    x = [x[c] + mm3(x[c], y[c]) for c in cs]
    for in_block in (blk32, None):
        if in_block is blk32:
            off = [jnp.where(blk32 & jnp.logical_not(blk16), aa_, 0.0) for aa_ in a_ab]
        else:
            off = [jnp.where(blk32, 0.0, aa_) for aa_ in a_ab]
        xo = [mm1(x[c], bd(off[c].astype(BF16))) for c in cs]
        x = [x[c] + mm1(xo[c], bd(x[c].astype(BF16))) for c in cs]
    t_inv = x

    rhs = [jnp.concatenate([bd(at_b[c]), bd(a_ak[c].astype(BF16))], axis=1) for c in cs]
    tt = [mm1(t_inv[c], rhs[c]) for c in cs]
    ap = [x_[:, :L].astype(BF16) for x_ in tt]
    ta = [x_[:, L:] for x_ in tt]
    bv = [bd(x_.astype(BF16)) for x_ in v]
    u_loc = [mm1(ta[c], bv[c]).astype(BF16) for c in cs]
    y_loc = [mm1(a_rk[c], bv[c]) for c in cs]
    a_rb = [x_.astype(BF16) for x_ in a_rb]
    bw_b = [x_.astype(BF16) for x_ in bw]
    bkw = [jnp.concatenate([bw_b[c], kw[c].astype(BF16)], axis=0) for c in cs]
    p_f32 = [jnp.where(same_head, _dot_tn(bw_b[c], ap[c]), 0.0) for c in cs]
    p_mat = [x_.astype(BF16) for x_ in p_f32]
    g_mat = [jnp.where(same_head, _dot_tn(bkw[c], jnp.concatenate([u_loc[c], v[c].astype(BF16)], axis=0)),
                       0.0) for c in cs]
    q_mat = [(rt[c] + _dot(a_rb[c], bd(ap[c]))).astype(BF16) for c in cs]
    z_mat = [_dot(a_rb[c], bd(u_loc[c])) + y_loc[c] for c in cs]
    assert n_chunks % 2 == 0
    evens = [j * n_chunks + c for j in range(n_pairs) for c in range(0, n_chunks, 2)]
    pg = {i: _dot(p_mat[i + 1], jnp.concatenate([p_mat[i], g_mat[i].astype(BF16)], axis=1)) for i in evens}
    w_row = {i: jnp.exp(cum[i][C - 1:C, :]) for i in evens}
    p2 = {i: (wcol[i + 1] * p_f32[i] + p_f32[i + 1] * w_row[i] + pg[i][:, :L]).astype(BF16) for i in evens}
    g2 = {i: wcol[i + 1] * g_mat[i] + pg[i][:, L:] + g_mat[i + 1] for i in evens}
    w2 = {i: wcol[i] * wcol[i + 1] for i in evens}

    states = [s_ref[j] for j in range(n_pairs)]
    entry = [None] * len(cs)
    odd_entry = [None] * len(cs)
    for c in range(0, n_chunks, 2):
        for j in range(n_pairs):
            i = j * n_chunks + c
            s_b = states[j].astype(BF16)
            entry[i] = s_b
            odd_entry[i + 1] = (states[j], s_b)
            states[j] = states[j] * w2[i] + _dot(p2[i], s_b) + g2[i]
    for j in range(n_pairs):
        s_ref[j] = states[j]
    for i in evens:
        s0, s0_b = odd_entry[i + 1]
        entry[i + 1] = (s0 * wcol[i] + _dot(p_mat[i], s0_b) + g_mat[i]).astype(BF16)
    for j in range(n_pairs):
        for c in range(n_chunks):
            i = j * n_chunks + c
            y_ref[c * C:(c + 1) * C, j * LANES:(j + 1) * LANES] = _dot(q_mat[i], entry[i]) + z_mat[i]

    inv_n = 1.0 / N
    y_all = y_ref[...]
    mu = _dot2(y_all, head_ones2) * inv_n
    dev = y_all - mu
    var = _dot2(dev * dev, head_ones2) * inv_n
    yn = dev * lax.rsqrt(var + RWKV_LN_EPS) * lnw_ref[...] + lnb_ref[...]
    bonus = _dot2(r_all * k_all * rk_ref[...], head_ones2) * v_all
    o_ref[...] = ((yn + bonus) * gate).astype(o_ref.dtype)


def _rwkv(p_rkv, p_lora, mix_rkv, mix_lora, w2p, a2p, g2p, w0, a0, k_k, k_a, r_k, ln_w, ln_b,
          ts=512, pairs_per_step=2):
    bsz, seq, _ = p_rkv.shape
    W = RWKV_WIDTH
    nl = p_lora.shape[-1]
    wd = pairs_per_step * LANES
    npair = W // wd
    col = lambda j: pl.BlockSpec((None, ts, wd), lambda bb, p, t, j=j: (bb, t, j * npair + p))
    vec_col = lambda j: pl.BlockSpec((1, wd), lambda bb, p, t, j=j: (0, j * npair + p))
    vec = pl.BlockSpec((1, wd), lambda bb, p, t: (0, p))
    mat = lambda rows: pl.BlockSpec((rows, wd), lambda bb, p, t: (0, p))
    kern = functools.partial(_rwkv_kernel, ts=ts)
    return pl.pallas_call(
        kern,
        grid=(bsz, npair, seq // ts),
        in_specs=[
            col(0), col(1), col(2),
            pl.BlockSpec((None, ts, nl), lambda bb, p, t: (bb, t, 0)),
            vec_col(0), vec_col(1), vec_col(2),
            pl.BlockSpec((1, nl), lambda bb, p, t: (0, 0)),
            mat(LANES), mat(LANES), mat(nl - LANES),
            vec, vec, vec, vec, vec, vec, vec,
        ],
        out_specs=pl.BlockSpec((None, ts, wd), lambda bb, p, t: (bb, t, p)),
        out_shape=jax.ShapeDtypeStruct((bsz, seq, W), BF16),
        scratch_shapes=[
            pltpu.VMEM((pairs_per_step, LANES, LANES), F32),
            pltpu.VMEM((8, wd), F32),
            pltpu.VMEM((1, nl), F32),
            pltpu.VMEM((ts, wd), F32),
        ],
        compiler_params=_params("arbitrary", "arbitrary", "arbitrary"),
        name="rwkv",
    )(p_rkv, p_rkv, p_rkv, p_lora, mix_rkv, mix_rkv, mix_rkv, mix_lora,
      w2p, a2p, g2p, w0, a0, k_k, k_a, r_k, ln_w, ln_b)


def _out_proj_kernel(oa_ref, or_ref, x_ref, w_ref, gpost_ref, gt_ref, gpre_ref, sc_ref, sh_ref,
                     wr_ref, br_ref, x1_ref, h2_ref, meta_ref, meta_t_ref, cnt_ref, run_ref):
    tm = x_ref.shape[0]
    half = oa_ref.shape[-1]

    @pl.when((pl.program_id(0) == 0) & (pl.program_id(1) == 0))
    def _():
        run_ref[...] = jnp.zeros_like(run_ref)

    y = _dot(oa_ref[...], w_ref[:half, :]) + _dot(or_ref[...], w_ref[half:, :])
    ms = jnp.mean(y * y, axis=-1, keepdims=True)
    x1 = x_ref[...] + gt_ref[...] * (y * lax.rsqrt(ms + RMS_EPS) * gpost_ref[...])
    x1_ref[...] = x1
    ms1 = jnp.mean(x1 * x1, axis=-1, keepdims=True)
    h2 = (x1 * lax.rsqrt(ms1 + RMS_EPS) * gpre_ref[...]) * (1.0 + sc_ref[...]) + sh_ref[...]
    h2_ref[...] = _pack_halves(h2)

    lg = _dot3(h2, wr_ref[...]) + br_ref[...]
    lane_i = lax.broadcasted_iota(jnp.int32, lg.shape, 1)
    lane = lane_i.astype(F32)
    lane_group = ((lane_i - N_GROUPS) >> 3).astype(F32)
    first = lambda mask: jnp.min(jnp.where(mask, lane, 4.0 * LANES), axis=-1, keepdims=True)
    gl = jnp.where(lane_i < N_GROUPS, lg, NEG_BIG)
    gmax = jnp.max(gl, axis=-1, keepdims=True)
    g_sel = first(gl == gmax)
    p_group = 1.0 / jnp.sum(jnp.exp(gl - gmax), axis=-1, keepdims=True)
    in_group = (lane_i >= N_GROUPS) & (lane_group == g_sel)
    el = jnp.where(in_group, lg, NEG_BIG)
    m0 = jnp.max(el, axis=-1, keepdims=True)
    i0 = first(el == m0)
    el = jnp.where(lane == i0, NEG_BIG, el)
    m1 = jnp.max(el, axis=-1, keepdims=True)
    i1 = first(el == m1)
    t = jnp.exp(m1 - m0)
    gate0 = p_group / (1.0 + t)
    gate1 = p_group * t / (1.0 + t)
    e0 = i0 - N_GROUPS
    e1 = i1 - N_GROUPS
    hit = (lane == e0) | (lane == e1)
    onehot = jnp.where(hit, 1.0, 0.0).astype(BF16)
    rr = lax.broadcasted_iota(jnp.int32, (tm, tm), 0)
    cc = lax.broadcasted_iota(jnp.int32, (tm, tm), 1)
    before = jnp.where(cc < rr, 1.0, 0.0).astype(BF16)
    seen = _dot(before, onehot) + run_ref[...]
    rank0 = jnp.sum(jnp.where(lane == e0, seen, 0.0), axis=-1, keepdims=True)
    rank1 = jnp.sum(jnp.where(lane == e1, seen, 0.0), axis=-1, keepdims=True)
    total = seen[tm - 1:tm, :] + jnp.where(hit[tm - 1:tm, :], 1.0, 0.0)
    run_ref[...] = total
    cnt_ref[...] = total
    meta = jnp.zeros(lg.shape, F32)
    for idx, val in ((META_E0, e0.astype(F32)), (META_E1, e1.astype(F32)), (META_RANK0, rank0),
                     (META_RANK1, rank1), (META_GATE0, gate0), (META_GATE1, gate1)):
        meta = jnp.where(lane == idx, val, meta)
    meta_ref[...] = meta
    meta_t_ref[...] = meta.T[:META_ROWS, :]


def _out_proj(o_att, o_rwkv, x, w_out, g_post, gt, g_pre, sc, sh, w_router, b_router, tm=512):
    bsz, seq, d = x.shape
    half = o_att.shape[-1]
    const = lambda b, i: (0, 0)
    row = lambda b, i: (b, i, 0)
    per_b = lambda b, i: (b, 0, 0)
    vec = pl.BlockSpec((1, d), const)
    mod = pl.BlockSpec((None, 1, d), per_b)
    return pl.pallas_call(
        _out_proj_kernel,
        grid=(bsz, seq // tm),
        in_specs=[
            pl.BlockSpec((None, tm, half), row),
            pl.BlockSpec((None, tm, half), row),
            pl.BlockSpec((None, tm, d), row),
            pl.BlockSpec((d, d), const, pipeline_mode=pl.Buffered(1)),
            vec, mod, vec, mod, mod,
            pl.BlockSpec((d, LANES), const),
            pl.BlockSpec((1, LANES), const),
        ],
        out_specs=[
            pl.BlockSpec((None, tm, d), row),
            pl.BlockSpec((None, tm, d // 2), row),
            pl.BlockSpec((None, tm, LANES), row),
            pl.BlockSpec((META_ROWS, tm), lambda b, i: (0, b * (seq // tm) + i)),
            pl.BlockSpec((1, LANES), const),
        ],
        out_shape=[
            jax.ShapeDtypeStruct((bsz, seq, d), F32),
            jax.ShapeDtypeStruct((bsz, seq, d // 2), U32),
            jax.ShapeDtypeStruct((bsz, seq, LANES), F32),
            jax.ShapeDtypeStruct((META_ROWS, bsz * seq), F32),
            jax.ShapeDtypeStruct((1, LANES), F32),
        ],
        scratch_shapes=[pltpu.VMEM((1, LANES), F32)],
        compiler_params=_params("arbitrary", "arbitrary"),
        name="out_proj",
    )(o_att, o_rwkv, x, w_out, g_post.reshape(1, d), gt, g_pre.reshape(1, d), sc, sh, w_router, b_router)


def _pack_halves(x):
    d2 = x.shape[-1] // 2
    lo_bits = lax.bitcast_convert_type(x[:, :d2].astype(BF16).astype(F32), U32)
    hi_bits = lax.bitcast_convert_type(x[:, d2:].astype(BF16).astype(F32), U32)
    return (lo_bits >> 16) | hi_bits


def _unpack_halves(words):
    lo = lax.bitcast_convert_type(words << 16, F32)
    hi = lax.bitcast_convert_type(words & jnp.uint32(0xFFFF0000), F32)
    return lo, hi


def _expert_kernel(te_ref, nv_ref, n1_ref, n2_ref, sl_ref, x_ref, wg_hbm, wu_hbm, wd_hbm, o_ref,
                   wg_f, wu_f, wd_f, wg_b, wu_b, wd_b, sem):
    i = pl.program_id(0)
    valid = i < nv_ref[0]
    prev = te_ref[jnp.maximum(i - 1, 0)]
    new_expert = (i == 0) | (te_ref[i] != prev)

    def weight_copies(e, slot):
        return (pltpu.make_async_copy(wg_hbm.at[e], wg_f.at[slot], sem.at[slot, 0]),
                pltpu.make_async_copy(wu_hbm.at[e], wu_f.at[slot], sem.at[slot, 1]),
                pltpu.make_async_copy(wd_hbm.at[e], wd_f.at[slot], sem.at[slot, 2]))

    def next_slot(slot, k):
        s = slot + k
        return jnp.where(s >= WEIGHT_SLOTS, s - WEIGHT_SLOTS, s)

    @pl.when(valid & (i == 0))
    def _():
        for cp in weight_copies(te_ref[0], sl_ref[0]):
            cp.start()

        @pl.when(n1_ref[0] >= 0)
        def _():
            for cp in weight_copies(n1_ref[0], next_slot(sl_ref[0], 1)):
                cp.start()

    @pl.when(valid & new_expert)
    def _():
        slot = sl_ref[i]
        for cp in weight_copies(te_ref[i], slot):
            cp.wait()

        @pl.when(n2_ref[i] >= 0)
        def _():
            for cp in weight_copies(n2_ref[i], next_slot(slot, 2)):
                cp.start()

        wg_b[...] = wg_f[slot].astype(BF16)
        wu_b[...] = wu_f[slot].astype(BF16)
        wd_b[...] = wd_f[slot].astype(BF16)

    @pl.when(valid)
    def _():
        x_lo, x_hi = _unpack_halves(x_ref[...])
        d2 = x_lo.shape[-1]
        x_lo, x_hi = x_lo.astype(BF16), x_hi.astype(BF16)
        g = _dot(x_lo, wg_b[:d2, :]) + _dot(x_hi, wg_b[d2:, :])
        u = _dot(x_lo, wu_b[:d2, :]) + _dot(x_hi, wu_b[d2:, :])
        hid = (g * jax.nn.sigmoid(g) * u).astype(BF16)
        o_ref[...] = _pack_halves(_dot(hid, wd_b[...]))

    @pl.when(jnp.logical_not(valid))
    def _():
        o_ref[...] = jnp.zeros_like(o_ref)


def _experts(xs, tile_expert, n_valid, next1, next2, slot, w_gate, w_up, w_down, tm):
    m_pad, d2 = xs.shape
    d = 2 * d2
    de = w_gate.shape[-1]
    n_tiles = m_pad // tm
    grid_spec = pltpu.PrefetchScalarGridSpec(
        num_scalar_prefetch=5,
        grid=(n_tiles,),
        in_specs=[
            pl.BlockSpec((tm, d2), lambda i, *_: (i, 0)),
            pl.BlockSpec(memory_space=pl.ANY),
            pl.BlockSpec(memory_space=pl.ANY),
            pl.BlockSpec(memory_space=pl.ANY),
        ],
        out_specs=pl.BlockSpec((tm, d2), lambda i, *_: (i, 0)),
        scratch_shapes=[
            pltpu.VMEM((WEIGHT_SLOTS, d, de), F32),
            pltpu.VMEM((WEIGHT_SLOTS, d, de), F32),
            pltpu.VMEM((WEIGHT_SLOTS, de, d), F32),
            pltpu.VMEM((d, de), BF16),
            pltpu.VMEM((d, de), BF16),
            pltpu.VMEM((de, d), BF16),
            pltpu.SemaphoreType.DMA((WEIGHT_SLOTS, 3)),
        ],
    )
    return pl.pallas_call(
        _expert_kernel,
        grid_spec=grid_spec,
        out_shape=jax.ShapeDtypeStruct((m_pad, d2), U32),
        compiler_params=_params("arbitrary"),
        name="experts",
    )(tile_expert, n_valid, next1, next2, slot, xs, w_gate, w_up, w_down)


def _final_kernel(x_ref, y0_ref, y1_ref, meta_ref, g_ref, gt_ref, o_ref):
    meta = meta_ref[...]
    gate0 = meta[:, META_GATE0:META_GATE0 + 1]
    gate1 = meta[:, META_GATE1:META_GATE1 + 1]
    d2 = y0_ref.shape[-1]
    halves = [a * gate0 + b * gate1 for a, b in zip(_unpack_halves(y0_ref[...]), _unpack_halves(y1_ref[...]))]
    ssq = sum(jnp.sum(h * h, axis=-1, keepdims=True) for h in halves)
    inv = lax.rsqrt(ssq * (1.0 / (2 * d2)) + RMS_EPS)
    for j, h in enumerate(halves):
        cols = slice(j * d2, (j + 1) * d2)
        o_ref[:, cols] = x_ref[:, cols] + gt_ref[:, cols] * (h * inv * g_ref[:, cols])


def _final(x1, y0, y1, meta, g, gt, tm=512):
    bsz, seq, d = x1.shape
    row = lambda b, i: (b, i, 0)
    blk = pl.BlockSpec((None, tm, d), row)
    packed = pl.BlockSpec((None, tm, d // 2), row)
    return pl.pallas_call(
        _final_kernel,
        grid=(bsz, seq // tm),
        in_specs=[blk, packed, packed,
                  pl.BlockSpec((None, tm, LANES), row),
                  pl.BlockSpec((1, d), lambda b, i: (0, 0)),
                  pl.BlockSpec((None, 1, d), lambda b, i: (b, 0, 0))],
        out_specs=blk,
        out_shape=jax.ShapeDtypeStruct((bsz, seq, d), F32),
        compiler_params=_params("arbitrary", "arbitrary"),
        name="final",
    )(x1, y0, y1, meta, g.reshape(1, d), gt)


def _dispatch(meta_t, counts, tm):
    n_tok = meta_t.shape[1]
    n_rows = n_tok * TOP_K
    n_tiles = n_rows // tm + N_EXPERTS
    expert = meta_t[META_E0:META_E1 + 1].astype(jnp.int32)
    rank = meta_t[META_RANK0:META_RANK1 + 1].astype(jnp.int32)
    counts = counts[0, :N_EXPERTS].astype(jnp.int32)
    tiles_e = (counts + tm - 1) // tm
    tile_end = jnp.cumsum(tiles_e)
    pad_start = (tile_end - tiles_e) * tm
    slot_pos = rank + sum(jnp.where(expert == e, pad_start[e], 0) for e in range(N_EXPERTS))
    n_valid = tile_end[-1]
    t_idx = jnp.arange(n_tiles, dtype=jnp.int32)
    tile_expert = jnp.searchsorted(tile_end, jnp.minimum(t_idx, n_valid - 1), side='right')
    tile_expert = jnp.minimum(tile_expert, N_EXPERTS - 1).astype(jnp.int32)
    tok = jnp.broadcast_to(jnp.arange(n_tok, dtype=jnp.int32)[None, :], (TOP_K, n_tok))
    filler = jnp.arange(n_tiles * tm, dtype=jnp.int32) % n_tok
    src_tok = filler.at[slot_pos.reshape(-1)].set(tok.reshape(-1))
    ids = jnp.arange(N_EXPERTS, dtype=jnp.int32)
    used = tiles_e > 0
    first_used_from = lax.cummin(jnp.where(used, ids, N_EXPERTS), reverse=True)
    next_used = jnp.concatenate([first_used_from[1:], jnp.full((1,), N_EXPERTS, jnp.int32)])
    next_used = jnp.where(next_used >= N_EXPERTS, -1, next_used)
    next2_used = jnp.where(next_used >= 0, next_used[jnp.maximum(next_used, 0)], -1)
    run_slot = (jnp.cumsum(used.astype(jnp.int32)) - 1) % WEIGHT_SLOTS
    per_tile = lambda table: table[tile_expert].astype(jnp.int32)
    return (src_tok, slot_pos, tile_expert, n_valid.reshape(1).astype(jnp.int32),
            per_tile(next_used), per_tile(next2_used), per_tile(run_slot))


def _pad_rows(w, rows_before, rows_total):
    return jnp.pad(w, ((rows_before, rows_total - rows_before - w.shape[0]), (0, 0))).astype(BF16)


def kernel(x, c, w_ada, b_ada, g_pre_mix, g_post_mix, g_pre_ffn, g_post_ffn, w_in, w_out,
           attn_out_gain, rwkv_shift_mix, rwkv_w0, rwkv_w2, rwkv_a0, rwkv_a2, rwkv_g2,
           rwkv_k_k, rwkv_k_a, rwkv_r_k, rwkv_ln_w, rwkv_ln_b, router_group_w, router_group_b,
           router_expert_w, router_expert_b, expert_w_gate, expert_w_up, expert_w_down):
    bsz, seq, d = x.shape
    depth = w_ada.shape[0]
    tm_moe = 256
    W = RWKV_WIDTH
    for l in range(depth):
        mod = _adaln(c, w_ada[l], b_ada[l])
        sh_m, sc_m, gt_m, sh_f, sc_f, gt_f = jnp.split(mod[:, None, :], N_MOD, axis=-1)

        w = w_in[l]
        n_att = 3 * ATT_WIDTH
        n_rkv = 3 * W
        n_w, n_a, n_g = rwkv_w2.shape[1], rwkv_a2.shape[1], rwkv_g2.shape[1]
        assert n_w + n_a == LANES and w.shape[1] == n_att + n_rkv + n_w + n_a + n_g
        n_lora = LANES + -(-n_g // LANES) * LANES
        pad_l = n_lora - (n_w + n_a + n_g)
        w_all = jnp.pad(w, ((0, 0), (0, pad_l))).astype(BF16)
        qkv, p_rkv, p_lora = _in_proj(x, g_pre_mix[l], sc_m, sh_m, w_all, n_att, n_rkv)
        o_att = _attention(qkv, attn_out_gain[l])
        mix = rwkv_shift_mix[l]
        row = lambda t: t.reshape(1, -1)
        o_rwkv = _rwkv(
            p_rkv, p_lora, row(mix[:n_rkv]), row(jnp.pad(mix[n_rkv:], (0, pad_l))),
            _pad_rows(rwkv_w2[l], 0, LANES), _pad_rows(rwkv_a2[l], n_w, LANES),
            _pad_rows(rwkv_g2[l], 0, n_lora - LANES),
            row(rwkv_w0[l]), row(rwkv_a0[l]), row(rwkv_k_k[l]), row(rwkv_k_a[l]), row(rwkv_r_k[l]),
            row(rwkv_ln_w[l]), row(rwkv_ln_b[l]))

        n_route = N_GROUPS + N_EXPERTS
        w_router = jnp.concatenate([router_group_w[l], router_expert_w[l]], axis=1)
        w_router = jnp.pad(w_router, ((0, 0), (0, LANES - n_route)))
        b_router = jnp.pad(jnp.concatenate([router_group_b[l], router_expert_b[l]]), (0, LANES - n_route))
        x1, h2w, meta, meta_t, counts = _out_proj(o_att, o_rwkv, x, w_out[l].astype(BF16), g_post_mix[l], gt_m,
                                          g_pre_ffn[l], sc_f, sh_f, w_router, b_router.reshape(1, LANES))

        src_tok, slot_pos, tile_expert, n_valid, next1, next2, slot = _dispatch(meta_t, counts, tm_moe)
        xs = h2w.reshape(bsz * seq, d // 2)[src_tok]
        rows = _experts(xs, tile_expert, n_valid, next1, next2, slot,
                        expert_w_gate[l], expert_w_up[l], expert_w_down[l], tm_moe)
        y0 = rows[slot_pos[0]].reshape(bsz, seq, d // 2)
        y1 = rows[slot_pos[1]].reshape(bsz, seq, d // 2)
        x = _final(x1, y0, y1, meta, g_post_ffn[l], gt_f)
    return x
```
